```python
import jax, jax.numpy as jnp
from jax import lax
import numpy as np

D_MODEL = 1024
BATCH = 2
SEQ = 8192
DEPTH = 4
DEC_BATCH = 32
DEC_SEQ = 16
PAST_LEN = 1024

CHUNK = 64
MIX_WIDTH = D_MODEL
RMS_EPS = 1e-6
GLA_WIDTH = MIX_WIDTH // 2
GLA_HEADS = 4
GLA_DV = GLA_WIDTH // GLA_HEADS
GLA_DK = GLA_DV // 2
GLA_KEY = GLA_HEADS * GLA_DK
GLA_RANK = 16
GLA_TEMP = 16.0
GLA_COLS = 2 * GLA_KEY + 2 * GLA_WIDTH + GLA_RANK
RWKV_WIDTH = MIX_WIDTH - GLA_WIDTH
RWKV_HEAD = 64
RWKV_HEADS = RWKV_WIDTH // RWKV_HEAD
RWKV_RANK_W = 64
RWKV_RANK_A = 64
SHIFT_W = 3 * RWKV_WIDTH + RWKV_RANK_W + RWKV_RANK_A
RWKV_GN_EPS = 64e-5
RWKV_DECAY_SCALE = 0.606531
IN_COLS = GLA_COLS + SHIFT_W + RWKV_WIDTH

kernel_name = "gla_rwkv7_parallel_heads_streaming"


def rms_norm(x, g):
    xf = x.astype(jnp.float32)
    y = xf * lax.rsqrt(jnp.mean(xf * xf, -1, keepdims=True) + RMS_EPS)
    return (y * g.astype(jnp.float32)).astype(x.dtype)


def gla_recurrence(q, k, v, log_a, s0):
    B, T, H, _ = q.shape
    L = min(CHUNK, T)
    n = -(-T // L)
    pad = n * L - T

    def blocks(t):
        t = jnp.pad(t, ((0, 0), (0, pad), (0, 0), (0, 0)))
        return t.reshape(B, n, L, H, t.shape[-1]).transpose(1, 0, 3, 2, 4)

    qb, kb, vb, gb = blocks(q), blocks(k), blocks(v), blocks(log_a)
    causal = jnp.tril(jnp.ones((L, L), dtype=bool))

    def step(s, inp):
        qc, kc, vc, gc = inp
        b = jnp.cumsum(gc, axis=2)
        diff = b[:, :, :, None, :] - b[:, :, None, :, :]
        decay = jnp.exp(jnp.where(causal[:, :, None], diff, -jnp.inf))
        att = jnp.einsum('bhtd,bhtsd,bhsd->bhts', qc, decay, kc)
        o = (jnp.einsum('bhtd,bhde->bhte', qc * jnp.exp(b), s)
             + jnp.einsum('bhts,bhse->bhte', att, vc))
        b_last = b[:, :, -1:, :]
        s_new = (jnp.exp(b_last[:, :, 0, :])[..., None] * s
                 + jnp.einsum('bhsd,bhse->bhde', kc * jnp.exp(b_last - b), vc))
        return s_new, o

    s_fin, ob = lax.scan(step, s0, (qb, kb, vb, gb))
    o = ob.transpose(1, 0, 3, 2, 4).reshape(B, n * L, H, v.shape[-1])[:, :T]
    return o, s_fin


def rwkv7_recurrence(r, w, k, v, kk, a, s0):
    def step(s, inp):
        r_t, w_t, k_t, v_t, kk_t, a_t = inp
        sa = jnp.einsum('bhij,bhj->bhi', s, -kk_t)
        s = (s * w_t[:, :, None, :] + sa[..., None] * (kk_t * a_t)[:, :, None, :]
             + v_t[..., None] * k_t[:, :, None, :])
        y = jnp.einsum('bhij,bhj->bhi', s, r_t)
        return s, y

    xs = tuple(jnp.moveaxis(t, 1, 0) for t in (r, w, k, v, kk, a))
    s_fin, ys = lax.scan(step, s0, xs)
    return jnp.moveaxis(ys, 0, 1), s_fin


def hybrid_layer(x, s_gla, s_rwkv, s_shift, norm_pre, norm_post, w_in, gla_a_up,
                 gla_a_bias, gla_norm, rwkv_mu, rwkv_w_up, rwkv_w0, rwkv_a_up,
                 rwkv_a0, rwkv_k_k, rwkv_k_a, rwkv_r_k, rwkv_ln_w, rwkv_ln_b, w_out):
    B, T, _ = x.shape
    f32 = jnp.float32
    h = rms_norm(x, norm_pre)
    p = jnp.einsum('btd,dc->btc', h, w_in).astype(f32)
    p_gla = p[..., :GLA_COLS]
    p_shift = p[..., GLA_COLS:GLA_COLS + SHIFT_W]
    z_r = p[..., GLA_COLS + SHIFT_W:]

    q, k, v, z_g, gd = jnp.split(
        p_gla, [GLA_KEY, 2 * GLA_KEY, 2 * GLA_KEY + GLA_WIDTH, 2 * GLA_KEY + 2 * GLA_WIDTH], -1)
    q = q.reshape(B, T, GLA_HEADS, GLA_DK) * (GLA_DK ** -0.5)
    k = k.reshape(B, T, GLA_HEADS, GLA_DK)
    v = v.reshape(B, T, GLA_HEADS, GLA_DV)
    log_a = jax.nn.log_sigmoid(gd @ gla_a_up.astype(f32) + gla_a_bias.astype(f32)) / GLA_TEMP
    log_a = log_a.reshape(B, T, GLA_HEADS, GLA_DK)
    o, s_gla_new = gla_recurrence(q, k, v, log_a, s_gla.astype(f32))
    o = o * lax.rsqrt(jnp.mean(o * o, -1, keepdims=True) + RMS_EPS) * gla_norm.astype(f32)
    o_gla = o.reshape(B, T, GLA_WIDTH) * jax.nn.silu(z_g)

    prev = jnp.concatenate([s_shift.astype(f32)[:, None], p_shift[:, :-1]], axis=1)
    xm = p_shift + rwkv_mu.astype(f32) * (prev - p_shift)
    r, kr, vr, wd, ad = jnp.split(
        xm, [RWKV_WIDTH, 2 * RWKV_WIDTH, 3 * RWKV_WIDTH, 3 * RWKV_WIDTH + RWKV_RANK_W], -1)
    w = jnp.exp(-RWKV_DECAY_SCALE * jax.nn.sigmoid(
        rwkv_w0.astype(f32) + jnp.tanh(wd) @ rwkv_w_up.astype(f32)))
    a = jax.nn.sigmoid(rwkv_a0.astype(f32) + ad @ rwkv_a_up.astype(f32))
    kk = kr * rwkv_k_k.astype(f32)
    kr = kr * (1.0 + (a - 1.0) * rwkv_k_a.astype(f32))
    hd = lambda t: t.reshape(B, T, RWKV_HEADS, RWKV_HEAD)
    r, w, kr, vr, kk, a = hd(r), hd(w), hd(kr), hd(vr), hd(kk), hd(a)
    kk = kk / jnp.maximum(jnp.sqrt(jnp.sum(kk * kk, -1, keepdims=True)), 1e-12)
    y, s_rwkv_new = rwkv7_recurrence(r, w, kr, vr, kk, a, s_rwkv.astype(f32))
    mu = jnp.mean(y, -1, keepdims=True)
    var = jnp.mean(jnp.square(y - mu), -1, keepdims=True)
    y = ((y - mu) * lax.rsqrt(var + RWKV_GN_EPS)).reshape(B, T, RWKV_WIDTH)
    y = y * rwkv_ln_w.astype(f32) + rwkv_ln_b.astype(f32)
    bonus = jnp.sum(r * kr * rwkv_r_k.astype(f32), -1, keepdims=True) * vr
    o_rwkv = (y + bonus.reshape(B, T, RWKV_WIDTH)) * jax.nn.silu(z_r)

    m = jnp.concatenate([o_gla, o_rwkv], -1).astype(x.dtype)
    m = jnp.einsum('btc,cd->btd', m, w_out)
    x = x + rms_norm(m, norm_post)
    return x, s_gla_new, s_rwkv_new, p_shift[:, -1]


def setup_inputs(seed: int = 0) -> dict:
    key = jax.random.key(seed)
    ks = jax.random.split(key, 24)
    n = lambda i, shape, s: jax.random.normal(ks[i], shape, jnp.float32) * s
    return {
        "x_prompt": n(0, (BATCH, SEQ, D_MODEL), 1.0),
        "x_sample": n(1, (DEC_BATCH, DEC_SEQ, D_MODEL), 1.0),
        "state_gla": n(2, (DEPTH, DEC_BATCH, GLA_HEADS, GLA_DK, GLA_DV), 0.5),
        "state_rwkv": n(3, (DEPTH, DEC_BATCH, RWKV_HEADS, RWKV_HEAD, RWKV_HEAD), 0.5),
        "state_shift": n(4, (DEPTH, DEC_BATCH, SHIFT_W), 1.0),
        "norm_pre": 1.0 + n(5, (DEPTH, D_MODEL), 0.05),
        "norm_post": 1.0 + n(6, (DEPTH, D_MODEL), 0.05),
        "w_in": n(7, (DEPTH, D_MODEL, IN_COLS), D_MODEL ** -0.5),
        "gla_a_up": n(8, (DEPTH, GLA_RANK, GLA_KEY), GLA_RANK ** -0.5),
        "gla_a_bias": 1.0 + n(9, (DEPTH, GLA_KEY), 0.5),
        "gla_norm": 1.0 + n(10, (DEPTH, GLA_DV), 0.05),
        "rwkv_mu": jax.random.uniform(ks[11], (DEPTH, SHIFT_W), jnp.float32),
        "rwkv_w_up": n(12, (DEPTH, RWKV_RANK_W, RWKV_WIDTH), RWKV_RANK_W ** -0.5),
        "rwkv_w0": n(13, (DEPTH, RWKV_WIDTH), 0.5),
        "rwkv_a_up": n(14, (DEPTH, RWKV_RANK_A, RWKV_WIDTH), RWKV_RANK_A ** -0.5),
        "rwkv_a0": n(15, (DEPTH, RWKV_WIDTH), 0.5),
        "rwkv_k_k": 0.85 + n(16, (DEPTH, RWKV_WIDTH), 0.05),
        "rwkv_k_a": 1.0 + n(17, (DEPTH, RWKV_WIDTH), 0.05),
        "rwkv_r_k": n(18, (DEPTH, RWKV_HEADS, RWKV_HEAD), 0.1),
        "rwkv_ln_w": 1.0 + n(19, (DEPTH, RWKV_WIDTH), 0.05),
        "rwkv_ln_b": n(20, (DEPTH, RWKV_WIDTH), 0.02),
        "w_out": n(21, (DEPTH, MIX_WIDTH, D_MODEL), MIX_WIDTH ** -0.5),
    }


def reference(x_prompt, x_sample, state_gla, state_rwkv, state_shift, norm_pre, norm_post,
              w_in, gla_a_up, gla_a_bias, gla_norm, rwkv_mu, rwkv_w_up, rwkv_w0, rwkv_a_up,
              rwkv_a0, rwkv_k_k, rwkv_k_a, rwkv_r_k, rwkv_ln_w, rwkv_ln_b, w_out):
    bp = x_prompt.shape[0]
    yp, ys = x_prompt, x_sample
    gla_p, gla_s, rwkv_p, rwkv_s, shift_p, shift_s = [], [], [], [], [], []
    for l in range(DEPTH):
        params = (norm_pre[l], norm_post[l], w_in[l], gla_a_up[l], gla_a_bias[l], gla_norm[l],
                  rwkv_mu[l], rwkv_w_up[l], rwkv_w0[l], rwkv_a_up[l], rwkv_a0[l], rwkv_k_k[l],
                  rwkv_k_a[l], rwkv_r_k[l], rwkv_ln_w[l], rwkv_ln_b[l], w_out[l])
        z_gla = jnp.zeros((bp, GLA_HEADS, GLA_DK, GLA_DV), jnp.float32)
        z_rwkv = jnp.zeros((bp, RWKV_HEADS, RWKV_HEAD, RWKV_HEAD), jnp.float32)
        z_shift = jnp.zeros((bp, SHIFT_W), jnp.float32)
        yp, g1, r1, s1 = hybrid_layer(yp, z_gla, z_rwkv, z_shift, *params)
        ys, g2, r2, s2 = hybrid_layer(ys, state_gla[l], state_rwkv[l], state_shift[l], *params)
        gla_p.append(g1); rwkv_p.append(r1); shift_p.append(s1)
        gla_s.append(g2.astype(state_gla.dtype)); rwkv_s.append(r2.astype(state_rwkv.dtype))
        shift_s.append(s2.astype(state_shift.dtype))
    return (yp, ys, jnp.stack(gla_p), jnp.stack(gla_s), jnp.stack(rwkv_p), jnp.stack(rwkv_s),
            jnp.stack(shift_p), jnp.stack(shift_s))
```

```python
import functools

import numpy as np
import jax
import jax.numpy as jnp
from jax import lax
from jax.experimental import pallas as pl
from jax.experimental.pallas import tpu as pltpu

F32 = jnp.float32
BF16 = jnp.bfloat16
HIGHEST = lax.Precision.HIGHEST

D_MODEL = 1024
DEPTH = 4
CHUNK = 64
RMS_EPS = 1e-6
GLA_WIDTH = 512
GLA_HEADS = 4
GLA_DV = 128
GLA_DK = 64
GLA_KEY = 256
GLA_RANK = 16
GLA_TEMP = 16.0
RWKV_WIDTH = 512
RWKV_HEAD = 64
RWKV_HEADS = 8
RWKV_RANK = 64
SHIFT_W = 3 * RWKV_WIDTH + 2 * RWKV_RANK
RWKV_GN_EPS = 64e-5
RWKV_DECAY_SCALE = 0.606531

LANES = 128
N_GLA_PAIRS = GLA_HEADS // 2
N_RWKV_PAIRS = RWKV_HEADS // 2

C_Q = 0
C_K = C_Q + GLA_KEY
C_V = C_K + GLA_KEY
C_ZG = C_V + GLA_WIDTH
C_SH = C_ZG + GLA_WIDTH
C_ZR = C_SH + SHIFT_W
C_GD = C_ZR + RWKV_WIDTH
N_COLS = C_GD + LANES

VMEM_LIMIT_BYTES = 48 * 1024 * 1024


def _nt(a, b):
    return lax.dot_general(a, b, (((1,), (1,)), ((), ())), preferred_element_type=F32)


def _tn(a, b):
    return lax.dot_general(a, b, (((0,), (0,)), ((), ())), preferred_element_type=F32)


def _mm(a, b):
    return jnp.dot(a, b, preferred_element_type=F32)


def _levels(L):
    out, h = [], L // 2
    while h >= 1:
        out.append(h)
        h //= 2
    return out


def _gla_exponent_matrix(L):
    t = np.arange(L)[:, None]
    i = np.arange(L)[None, :]
    blocks = [i <= t, i > t]
    for h in _levels(L):
        mid = (t // (2 * h)) * (2 * h) + h
        second = t >= mid
        blocks.append(np.where(second, (i >= mid) & (i <= t), (i > t) & (i < mid)))
    return np.concatenate(blocks, 0).astype(np.float32)


def _layer_kernel(x_ref, sg_in, sr_in, ss_in, npre, npost, win, aup, abias, gnorm, mu_ref,
                  waup, w0, a0, kk_w, ka_w, rk_w, lnw, lnb, wout, e_ref, tri_ref, seg_ref,
                  y_ref, sg_out, sr_out, ss_out, pbuf, mbuf, *, tile_rows, L, per_seq):
    L2 = 2 * L
    levels = _levels(L)

    @pl.when(pl.program_id(1) == 0)
    def _():
        sg_out[...] = sg_in[...]
        sr_out[...] = sr_in[...]
        ss_out[...] = ss_in[...]

    x = x_ref[...]
    h = x * lax.rsqrt(jnp.mean(x * x, -1, keepdims=True) + RMS_EPS) * npre[...]
    pbuf[...] = _mm(h.astype(BF16), win[...])

    lane = lax.broadcasted_iota(jnp.int32, (1, LANES), 1)
    m0 = lane < RWKV_HEAD
    row = lax.broadcasted_iota(jnp.int32, (L2, L2), 0)
    col = lax.broadcasted_iota(jnp.int32, (L2, L2), 1)
    eye = row == col
    strict_lower = row > col
    lower = row >= col
    trow = lax.broadcasted_iota(jnp.int32, (L, 1), 0)

    def level_mask(hsz):
        sh = hsz.bit_length()
        same = (row >> sh) == (col >> sh)
        return same & (((row >> (sh - 1)) & 1) == 1) & (((col >> (sh - 1)) & 1) == 0)

    lmask = [level_mask(hsz) for hsz in levels]
    second_half = [((trow >> (hsz.bit_length() - 1)) & 1) == 1 for hsz in levels]

    def stack(v):
        return jnp.concatenate([jnp.where(m0, v, 0.0), jnp.where(m0, 0.0, v)], axis=0)

    def unstack_lanes(v, n):
        return jnp.concatenate([v[i * L:(i + 1) * L] for i in range(n)], axis=1)

    def segsum(v):
        vs = jnp.concatenate([v[:, i * LANES:(i + 1) * LANES] for i in range(4)], axis=0)
        hi = vs.astype(BF16)
        lo = (vs - hi.astype(F32)).astype(BF16)
        s = _mm(jnp.concatenate([hi, lo], axis=1), seg_ref[...])
        return unstack_lanes(s, 4)

    def sigmoid(v):
        return 1.0 / (1.0 + jnp.exp(-v))

    def unit_lower_inverse(a):
        d = jnp.where(eye, 1.0, 0.0) + jnp.where(lmask[-1], a, 0.0)
        for li in range(len(levels) - 2, -1, -1):
            m = jnp.where(lmask[li], a, 0.0).astype(BF16)
            db = d.astype(BF16)
            d = d + _mm(_mm(db, m).astype(BF16), db)
        return d

    def chunk(c, carry):
        r0 = pl.multiple_of(c * L, L)
        rows = pl.ds(r0, L)
        sidx = c if per_seq else 0

        q = pbuf[rows, C_Q:C_Q + GLA_KEY] * (GLA_DK ** -0.5)
        k = pbuf[rows, C_K:C_K + GLA_KEY]
        gd = pbuf[rows, C_GD:C_GD + LANES]
        gl = _mm(gd.astype(BF16), aup[...]) + abias[...]
        log_a = (jnp.minimum(gl, 0.0) - jnp.log1p(jnp.exp(-jnp.abs(gl)))) * (1.0 / GLA_TEMP)
        ex = jnp.exp(jnp.dot(e_ref[...], log_a, precision=HIGHEST, preferred_element_type=F32))
        eb = ex[0:L]
        eb_rest = ex[L:2 * L]
        eb_last = eb[L - 1:L]
        for j in range(N_GLA_PAIRS):
            ls = slice(j * LANES, (j + 1) * LANES)
            qj, kj = q[:, ls], k[:, ls]
            att = jnp.where(eye, _nt(stack(qj).astype(BF16), stack(kj).astype(BF16)), 0.0)
            for li in range(len(levels)):
                exl = ex[(2 + li) * L:(3 + li) * L, ls]
                ql = jnp.where(second_half[li], qj * exl, 0.0)
                kl = jnp.where(second_half[li], 0.0, kj * exl)
                att = att + jnp.where(
                    lmask[li], _nt(stack(ql).astype(BF16), stack(kl).astype(BF16)), 0.0)
            v0 = C_V + 2 * j * GLA_DV
            vs = jnp.concatenate(
                [pbuf[rows, v0:v0 + GLA_DV], pbuf[rows, v0 + GLA_DV:v0 + 2 * GLA_DV]], axis=0)
            vsb = vs.astype(BF16)
            gt = sg_out[sidx, j]
            o = _mm(att.astype(BF16), vsb) + _nt(stack(qj * eb[:, ls]).astype(BF16), gt.astype(BF16))
            sg_out[sidx, j] = gt * eb_last[:, ls] + _tn(vsb, stack(kj * eb_rest[:, ls]).astype(BF16))
            o = o * lax.rsqrt(jnp.mean(o * o, -1, keepdims=True) + RMS_EPS) * gnorm[...]
            for hh in range(2):
                c0 = (2 * j + hh) * GLA_DV
                zg = pbuf[rows, C_ZG + c0:C_ZG + c0 + GLA_DV]
                mbuf[rows, c0:c0 + GLA_DV] = (o[hh * L:(hh + 1) * L] * (zg * sigmoid(zg))).astype(BF16)

        cur = pbuf[rows, C_SH:C_SH + SHIFT_W]
        prev = jnp.where(trow == 0, ss_out[sidx], pltpu.roll(cur, 1, 0))
        ss_out[sidx] = cur[L - 1:L]
        xm = cur + mu_ref[...] * (prev - cur)
        W = RWKV_WIDTH
        r, kr, vr = xm[:, 0:W], xm[:, W:2 * W], xm[:, 2 * W:3 * W]
        wa = xm[:, 3 * W:3 * W + LANES]
        wa = jnp.where(m0, jnp.tanh(wa), wa)
        wa_pre = _mm(wa.astype(BF16), waup[...])
        log_w = -RWKV_DECAY_SCALE * sigmoid(w0[...] + wa_pre[:, 0:W])
        a = sigmoid(a0[...] + wa_pre[:, W:2 * W])
        kk = kr * kk_w[...]
        kr = kr * (1.0 + (a - 1.0) * ka_w[...])
        kk = kk / jnp.maximum(jnp.sqrt(segsum(kk * kk)), 1e-12)
        cum = jnp.dot(tri_ref[...], log_w, precision=HIGHEST, preferred_element_type=F32)
        cum_last = cum[L - 1:L]
        g_inv = jnp.exp(-cum)
        g_rest = jnp.exp(cum_last - cum)
        g_last = jnp.exp(cum_last)
        beta = a * kk
        alpha_t = -kk * jnp.exp(cum - log_w)
        r_t = r * jnp.exp(cum)
        beta_h, k_h = beta * g_inv, kr * g_inv
        beta_r, k_r = beta * g_rest, kr * g_rest
        ys = []
        for j in range(N_RWKV_PAIRS):
            ls = slice(j * LANES, (j + 1) * LANES)
            al_s = stack(alpha_t[:, ls]).astype(BF16)
            r_s = stack(r_t[:, ls]).astype(BF16)
            bh_s = stack(beta_h[:, ls]).astype(BF16)
            kh_s = stack(k_h[:, ls]).astype(BF16)
            v_s = stack(vr[:, ls]).astype(BF16)
            a_ab = jnp.where(strict_lower, _nt(al_s, bh_s), 0.0)
            a_ak = jnp.where(strict_lower, _nt(al_s, kh_s), 0.0)
            a_rb = jnp.where(lower, _nt(r_s, bh_s), 0.0)
            a_rk = jnp.where(lower, _nt(r_s, kh_s), 0.0)
            tinv = unit_lower_inverse(a_ab).astype(BF16)
            akv = _mm(a_ak.astype(BF16), v_s)
            wu = _mm(tinv, jnp.concatenate([al_s, akv.astype(BF16)], axis=1))
            st = sr_out[sidx, j]
            stb = st.astype(BF16)
            u = _nt(wu[:, 0:LANES].astype(BF16), stb) + wu[:, LANES:2 * LANES]
            ub = u.astype(BF16)
            y_s = _nt(r_s, stb) + _mm(a_rb.astype(BF16), ub) + _mm(a_rk.astype(BF16), v_s)
            ys.append(y_s[0:L] + y_s[L:L2])
            sr_out[sidx, j] = st * g_last[:, ls] + _tn(
                jnp.concatenate([ub, v_s], axis=0),
                jnp.concatenate([stack(beta_r[:, ls]), stack(k_r[:, ls])], axis=0).astype(BF16))
        y = jnp.concatenate(ys, axis=1)
        mean = segsum(y) * (1.0 / RWKV_HEAD)
        yc = y - mean
        var = segsum(yc * yc) * (1.0 / RWKV_HEAD)
        y = yc * lax.rsqrt(var + RWKV_GN_EPS) * lnw[...] + lnb[...]
        bonus = segsum(r * kr * rk_w[...]) * vr
        zr = pbuf[rows, C_ZR:C_ZR + W]
        mbuf[rows, GLA_WIDTH:GLA_WIDTH + W] = ((y + bonus) * (zr * sigmoid(zr))).astype(BF16)
        return carry

    lax.fori_loop(0, tile_rows // L, chunk, 0)

    o = _mm(mbuf[...], wout[...])
    o = o * lax.rsqrt(jnp.mean(o * o, -1, keepdims=True) + RMS_EPS) * npost[...]
    y_ref[...] = x_ref[...] + o


def _layer_call(layer, x2d, sg, sr, ss, params, consts, *, n_groups, tiles_per_group, tile_rows,
                L, per_seq):
    nseq = sg.shape[0] // n_groups
    e_mat, tri, seg = consts

    def row_spec(width):
        return pl.BlockSpec((tile_rows, width), lambda b, t: (b * tiles_per_group + t, 0))

    def state_spec(arr):
        nd = arr.ndim
        return pl.BlockSpec((nseq,) + arr.shape[1:], lambda b, t: (b,) + (0,) * (nd - 1))

    def layer_spec(arr):
        return pl.BlockSpec((None,) + arr.shape[1:], lambda b, t: (layer,) + (0,) * (arr.ndim - 1))

    def const_spec(arr):
        return pl.BlockSpec(arr.shape, lambda b, t: (0,) * arr.ndim)

    in_specs = ([row_spec(D_MODEL), state_spec(sg), state_spec(sr), state_spec(ss)]
                + [layer_spec(p) for p in params] + [const_spec(c) for c in consts])
    out_specs = [row_spec(D_MODEL), state_spec(sg), state_spec(sr), state_spec(ss)]
    out_shape = [jax.ShapeDtypeStruct(x2d.shape, F32), jax.ShapeDtypeStruct(sg.shape, F32),
                 jax.ShapeDtypeStruct(sr.shape, F32), jax.ShapeDtypeStruct(ss.shape, F32)]
    return pl.pallas_call(
        functools.partial(_layer_kernel, tile_rows=tile_rows, L=L, per_seq=per_seq),
        grid=(n_groups, tiles_per_group),
        in_specs=in_specs,
        out_specs=out_specs,
        out_shape=out_shape,
        scratch_shapes=[pltpu.VMEM((tile_rows, N_COLS), F32), pltpu.VMEM((tile_rows, D_MODEL), BF16)],
        compiler_params=pltpu.CompilerParams(
            dimension_semantics=("arbitrary", "arbitrary"), vmem_limit_bytes=VMEM_LIMIT_BYTES),
    )(x2d, sg, sr, ss, *params, *consts)


def _gla_state_to_kernel(s):
    n = s.shape[0]
    s = s.reshape(n, N_GLA_PAIRS, 2, GLA_DK, GLA_DV).transpose(0, 1, 4, 2, 3)
    return s.reshape(n, N_GLA_PAIRS, GLA_DV, LANES)


def _gla_state_from_kernel(s):
    n = s.shape[0]
    s = s.reshape(n, N_GLA_PAIRS, GLA_DV, 2, GLA_DK).transpose(0, 1, 3, 4, 2)
    return s.reshape(n, GLA_HEADS, GLA_DK, GLA_DV)


def _rwkv_state_to_kernel(s):
    n = s.shape[0]
    s = s.reshape(n, N_RWKV_PAIRS, 2, RWKV_HEAD, RWKV_HEAD)
    z = jnp.zeros_like(s[:, :, 0])
    top = jnp.concatenate([s[:, :, 0], z], axis=-1)
    bot = jnp.concatenate([z, s[:, :, 1]], axis=-1)
    return jnp.concatenate([top, bot], axis=-2)


def _rwkv_state_from_kernel(s):
    n = s.shape[0]
    H = RWKV_HEAD
    return jnp.stack([s[:, :, :H, :H], s[:, :, H:, H:]], axis=2).reshape(n, RWKV_HEADS, H, H)


def _row(p):
    return p.reshape(DEPTH, 1, -1)


def _constants(L):
    seg = np.kron(np.eye(2, dtype=np.float32), np.ones((RWKV_HEAD, RWKV_HEAD), np.float32))
    return (jnp.asarray(_gla_exponent_matrix(L)),
            jnp.asarray(np.tril(np.ones((L, L), np.float32))),
            jnp.asarray(np.concatenate([seg, seg], 0), dtype=BF16))


def kernel(x_prompt, x_sample, state_gla, state_rwkv, state_shift, norm_pre, norm_post, w_in,
           gla_a_up, gla_a_bias, gla_norm, rwkv_mu, rwkv_w_up, rwkv_w0, rwkv_a_up, rwkv_a0,
           rwkv_k_k, rwkv_k_a, rwkv_r_k, rwkv_ln_w, rwkv_ln_b, w_out):
    bp, seq, _ = x_prompt.shape
    bs, dec_seq, _ = x_sample.shape

    gd0 = 2 * GLA_KEY + 2 * GLA_WIDTH
    w_in_k = jnp.concatenate(
        [w_in[:, :, :gd0], w_in[:, :, gd0 + GLA_RANK:], w_in[:, :, gd0:gd0 + GLA_RANK],
         jnp.zeros((DEPTH, D_MODEL, LANES - GLA_RANK), w_in.dtype)], axis=-1).astype(BF16)
    aup = jnp.pad(gla_a_up, ((0, 0), (0, LANES - GLA_RANK), (0, 0))).astype(BF16)
    zpad = jnp.zeros_like(rwkv_w_up)
    waup = jnp.concatenate([jnp.concatenate([rwkv_w_up, zpad], 1),
                            jnp.concatenate([zpad, rwkv_a_up], 1)], 2).astype(BF16)
    params = (_row(norm_pre), _row(norm_post), w_in_k, aup, _row(gla_a_bias), _row(gla_norm),
              _row(rwkv_mu), waup, _row(rwkv_w0), _row(rwkv_a0), _row(rwkv_k_k), _row(rwkv_k_a),
              _row(rwkv_r_k), _row(rwkv_ln_w), _row(rwkv_ln_b), w_out.astype(BF16))

    prompt_tile = 256
    sample_seqs_per_tile = 8
    consts_p = _constants(CHUNK)
    consts_s = _constants(dec_seq)

    yp = x_prompt.reshape(bp * seq, D_MODEL)
    ys = x_sample.reshape(bs * dec_seq, D_MODEL)
    zg = jnp.zeros((bp, N_GLA_PAIRS, GLA_DV, LANES), F32)
    zr = jnp.zeros((bp, N_RWKV_PAIRS, LANES, LANES), F32)
    zs = jnp.zeros((bp, 1, SHIFT_W), F32)
    outs = [[] for _ in range(6)]
    for l in range(DEPTH):
        yp, g1, r1, s1 = _layer_call(
            l, yp, zg, zr, zs, params, consts_p, n_groups=bp, tiles_per_group=seq // prompt_tile,
            tile_rows=prompt_tile, L=CHUNK, per_seq=False)
        ys, g2, r2, s2 = _layer_call(
            l, ys, _gla_state_to_kernel(state_gla[l]), _rwkv_state_to_kernel(state_rwkv[l]),
            state_shift[l][:, None, :], params, consts_s, n_groups=bs // sample_seqs_per_tile,
            tiles_per_group=1, tile_rows=sample_seqs_per_tile * dec_seq, L=dec_seq, per_seq=True)
        for lst, val in zip(outs, (_gla_state_from_kernel(g1), _gla_state_from_kernel(g2),
                                   _rwkv_state_from_kernel(r1), _rwkv_state_from_kernel(r2),
                                   s1[:, 0], s2[:, 0])):
            lst.append(val)
    return (yp.reshape(bp, seq, D_MODEL), ys.reshape(bs, dec_seq, D_MODEL),
            jnp.stack(outs[0]), jnp.stack(outs[1]), jnp.stack(outs[2]), jnp.stack(outs[3]),
            jnp.stack(outs[4]), jnp.stack(outs[5]))
```

```python
import functools

import numpy as np
import jax
import jax.numpy as jnp
from jax import lax
from jax.experimental import pallas as pl
from jax.experimental.pallas import tpu as pltpu

F32 = jnp.float32
BF16 = jnp.bfloat16

D_MODEL = 1024
DEPTH = 4
CHUNK = 64
RMS_EPS = 1e-6
GLA_WIDTH = 512
GLA_HEADS = 4
GLA_DV = 128
GLA_DK = 64
GLA_KEY = 256
GLA_RANK = 16
GLA_TEMP = 16.0
RWKV_WIDTH = 512
RWKV_HEAD = 64
RWKV_HEADS = 8
RWKV_RANK = 64
SHIFT_W = 3 * RWKV_WIDTH + 2 * RWKV_RANK
RWKV_GN_EPS = 64e-5
RWKV_DECAY_SCALE = 0.606531

LANES = 128
N_GLA_PAIRS = GLA_HEADS // 2
N_RWKV_PAIRS = RWKV_HEADS // 2

C_Q = 0
C_K = C_Q + GLA_KEY
C_V = C_K + GLA_KEY
C_ZG = C_V + GLA_WIDTH
C_SH = C_ZG + GLA_WIDTH
C_ZR = C_SH + SHIFT_W
C_GD = C_ZR + RWKV_WIDTH
N_COLS = C_GD + LANES

VMEM_LIMIT_BYTES = 56 * 1024 * 1024


def _nt(a, b):
    return lax.dot_general(a, b, (((1,), (1,)), ((), ())), preferred_element_type=F32)


def _tn(a, b):
    return lax.dot_general(a, b, (((0,), (0,)), ((), ())), preferred_element_type=F32)


def _mm(a, b):
    return jnp.dot(a, b, preferred_element_type=F32)


def _bf(v):
    return v.astype(BF16)


def _split3(v):
    hi = _bf(v)
    r1 = v - hi.astype(F32)
    mid = _bf(r1)
    return hi, mid, _bf(r1 - mid.astype(F32))


def _levels(L):
    out, h = [], L // 2
    while h >= 1:
        out.append(h)
        h //= 2
    return out


def _gla_exponent_matrix(L):
    t = np.arange(L)[:, None]
    i = np.arange(L)[None, :]
    blocks = [i <= t, i > t]
    for h in _levels(L):
        mid = (t // (2 * h)) * (2 * h) + h
        second = t >= mid
        blocks.append(np.where(second, (i >= mid) & (i <= t), (i > t) & (i < mid)))
    return np.concatenate(blocks, 0).astype(np.float32)


def _layer_kernel(x_ref, sg_in, sr_in, ss_in, npre, npost, win, aup, abias, gnorm, mu_ref,
                  waup, w0, a0, kk_w, ka_w, rk_w, lnw, lnb, wout, e_ref, tri_ref, seg_ref,
                  y_ref, sg_out, sr_out, ss_out, pbuf, mbuf, *, tile_rows, L, per_seq):
    TT = tile_rows
    nc = TT // L
    L2 = 2 * L
    W = RWKV_WIDTH
    levels = _levels(L)
    chunks = range(nc)

    @pl.when(pl.program_id(1) == 0)
    def _():
        sg_out[...] = sg_in[...]
        sr_out[...] = sr_in[...]
        ss_out[...] = ss_in[...]

    x = x_ref[...]
    h = x * lax.rsqrt(jnp.mean(x * x, -1, keepdims=True) + RMS_EPS) * npre[...]
    pbuf[...] = _mm(_bf(h), win[...])

    lane = lax.broadcasted_iota(jnp.int32, (1, LANES), 1)
    m0 = lane < RWKV_HEAD
    row = lax.broadcasted_iota(jnp.int32, (L2, L2), 0)
    col = lax.broadcasted_iota(jnp.int32, (L2, L2), 1)
    eye = row == col
    strict_lower = row > col
    lower = row >= col
    crow = lax.broadcasted_iota(jnp.int32, (L, 1), 0)
    trow = lax.broadcasted_iota(jnp.int32, (TT, 1), 0)

    def level_mask(hsz):
        sh = hsz.bit_length()
        same = (row >> sh) == (col >> sh)
        return same & (((row >> (sh - 1)) & 1) == 1) & (((col >> (sh - 1)) & 1) == 0)

    lmask = [level_mask(hsz) for hsz in levels]
    second_half = [((crow >> (hsz.bit_length() - 1)) & 1) == 1 for hsz in levels]

    def rows(c):
        return slice(c * L, (c + 1) * L)

    def stack(v):
        return jnp.concatenate([jnp.where(m0, v, 0.0), jnp.where(m0, 0.0, v)], axis=0)

    def segsum(v):
        vs = jnp.concatenate([v[:, i * LANES:(i + 1) * LANES] for i in range(4)], axis=0)
        hi = _bf(vs)
        lo = _bf(vs - hi.astype(F32))
        s = _mm(jnp.concatenate([hi, lo], axis=1), seg_ref[...])
        return jnp.concatenate([s[i * TT:(i + 1) * TT] for i in range(4)], axis=1)

    def chunk_cumsum(mat_ref, v):
        parts = _split3(v)
        return [_mm(mat_ref[...], jnp.concatenate([p[rows(c)] for p in parts], axis=0))
                for c in chunks]

    def last_row_bcast(v):
        return jnp.concatenate(
            [jnp.broadcast_to(v[(c + 1) * L - 1:(c + 1) * L], (L, v.shape[1])) for c in chunks], 0)

    def sigmoid(v):
        return 1.0 / (1.0 + jnp.exp(-v))

    def unit_lower_inverse(a_list):
        eyef = jnp.where(eye, 1.0, 0.0)
        d = [eyef + jnp.where(lmask[-1], a, 0.0) for a in a_list]
        for li in range(len(levels) - 2, -1, -1):
            m = [_bf(jnp.where(lmask[li], a, 0.0)) for a in a_list]
            db = [_bf(v) for v in d]
            dm = [_bf(_mm(p, q)) for p, q in zip(db, m)]
            d = [v + _mm(p, q) for v, p, q in zip(d, dm, db)]
        return d

    cur = pbuf[:, C_SH:C_SH + SHIFT_W]
    if per_seq:
        first = jnp.concatenate([jnp.broadcast_to(ss_out[c], (L, SHIFT_W)) for c in chunks], 0)
        prev = jnp.where((trow & (L - 1)) == 0, first, pltpu.roll(cur, 1, 0))
        for c in chunks:
            ss_out[c] = cur[(c + 1) * L - 1:(c + 1) * L]
    else:
        prev = jnp.where(trow == 0, ss_out[0], pltpu.roll(cur, 1, 0))
        ss_out[0] = cur[TT - 1:TT]
    xm = cur + mu_ref[...] * (prev - cur)
    r, kr, vr = xm[:, 0:W], xm[:, W:2 * W], xm[:, 2 * W:3 * W]
    wa = xm[:, 3 * W:3 * W + LANES]
    wa = jnp.where(m0, jnp.tanh(wa), wa)
    wa_pre = _mm(_bf(wa), waup[...])
    log_w = -RWKV_DECAY_SCALE * sigmoid(w0[...] + wa_pre[:, 0:W])
    a = sigmoid(a0[...] + wa_pre[:, W:2 * W])
    kk = kr * kk_w[...]
    kr = kr * (1.0 + (a - 1.0) * ka_w[...])
    kk = kk / jnp.maximum(jnp.sqrt(segsum(kk * kk)), 1e-12)
    cum = jnp.concatenate(chunk_cumsum(tri_ref, log_w), axis=0)
    cum_last = last_row_bcast(cum)
    g_inv = jnp.exp(-cum)
    g_rest = jnp.exp(cum_last - cum)
    g_last = jnp.exp(cum_last)
    beta = a * kk
    alpha_t = -kk * jnp.exp(cum - log_w)
    r_t = r * jnp.exp(cum)
    beta_h, k_h = beta * g_inv, kr * g_inv
    beta_r, k_r = beta * g_rest, kr * g_rest

    items = [(c, j) for c in chunks for j in range(N_RWKV_PAIRS)]

    def pick(v, c, j):
        return v[rows(c), j * LANES:(j + 1) * LANES]

    al_s = [_bf(stack(pick(alpha_t, c, j))) for c, j in items]
    r_s = [_bf(stack(pick(r_t, c, j))) for c, j in items]
    bh_s = [_bf(stack(pick(beta_h, c, j))) for c, j in items]
    kh_s = [_bf(stack(pick(k_h, c, j))) for c, j in items]
    v_s = [_bf(stack(pick(vr, c, j))) for c, j in items]
    br_s = [_bf(stack(pick(beta_r, c, j))) for c, j in items]
    kr_s = [_bf(stack(pick(k_r, c, j))) for c, j in items]
    a_ab = [jnp.where(strict_lower, _nt(p, q), 0.0) for p, q in zip(al_s, bh_s)]
    a_ak = [_bf(jnp.where(strict_lower, _nt(p, q), 0.0)) for p, q in zip(al_s, kh_s)]
    a_rb = [_bf(jnp.where(lower, _nt(p, q), 0.0)) for p, q in zip(r_s, bh_s)]
    a_rk = [_bf(jnp.where(lower, _nt(p, q), 0.0)) for p, q in zip(r_s, kh_s)]
    tinv = [_bf(v) for v in unit_lower_inverse(a_ab)]
    akv = [_bf(_mm(p, q)) for p, q in zip(a_ak, v_s)]
    wu = [_mm(t, jnp.concatenate([p, q], axis=1)) for t, p, q in zip(tinv, al_s, akv)]
    w_t = [_bf(v[:, 0:LANES]) for v in wu]
    u_p = [v[:, LANES:2 * LANES] for v in wu]
    y_kv = [_mm(p, q) for p, q in zip(a_rk, v_s)]
    s_kv = [_tn(p, q) for p, q in zip(v_s, kr_s)]

    def rwkv_step(idx, states):
        stb = [_bf(s) for s in states]
        wr = [_nt(jnp.concatenate([w_t[i], r_s[i]], axis=0), sb) for i, sb in zip(idx, stb)]
        ub = [_bf(v[0:L2] + u_p[i]) for i, v in zip(idx, wr)]
        new = [s * pick(g_last, *items[i])[0:1] + _tn(u, br_s[i]) + s_kv[i]
               for i, s, u in zip(idx, states, ub)]
        ys = [v[L2:2 * L2] + _mm(a_rb[i], u) + y_kv[i] for i, v, u in zip(idx, wr, ub)]
        return [v[0:L] + v[L:L2] for v in ys], new

    y_items = [None] * len(items)
    if per_seq:
        idx = list(range(len(items)))
        outs, new = rwkv_step(idx, [sr_out[c, j] for c, j in items])
        for i, (c, j) in enumerate(items):
            y_items[i] = outs[i]
            sr_out[c, j] = new[i]
    else:
        states = [sr_out[0, j] for j in range(N_RWKV_PAIRS)]
        for c in chunks:
            idx = [c * N_RWKV_PAIRS + j for j in range(N_RWKV_PAIRS)]
            outs, states = rwkv_step(idx, states)
            for i, o in zip(idx, outs):
                y_items[i] = o
        for j in range(N_RWKV_PAIRS):
            sr_out[0, j] = states[j]
    y = jnp.concatenate(
        [jnp.concatenate([y_items[c * N_RWKV_PAIRS + j] for j in range(N_RWKV_PAIRS)], axis=1)
         for c in chunks], axis=0)
    mean = segsum(y) * (1.0 / RWKV_HEAD)
    yc = y - mean
    var = segsum(yc * yc) * (1.0 / RWKV_HEAD)
    y = yc * lax.rsqrt(var + RWKV_GN_EPS) * lnw[...] + lnb[...]
    bonus = segsum(r * kr * rk_w[...]) * vr
    zr = pbuf[:, C_ZR:C_ZR + W]
    mbuf[:, GLA_WIDTH:GLA_WIDTH + W] = _bf((y + bonus) * (zr * sigmoid(zr)))

    q = pbuf[:, C_Q:C_Q + GLA_KEY] * (GLA_DK ** -0.5)
    k = pbuf[:, C_K:C_K + GLA_KEY]
    gl = _mm(_bf(pbuf[:, C_GD:C_GD + LANES]), aup[...]) + abias[...]
    log_a = (jnp.minimum(gl, 0.0) - jnp.log1p(jnp.exp(-jnp.abs(gl)))) * (1.0 / GLA_TEMP)
    ex = [jnp.exp(v) for v in chunk_cumsum(e_ref, log_a)]
    gitems = [(c, j) for c in chunks for j in range(N_GLA_PAIRS)]
    qj = [pick(q, c, j) for c, j in gitems]
    kj = [pick(k, c, j) for c, j in gitems]

    def exl(blk, c, j):
        return ex[c][blk * L:(blk + 1) * L, j * LANES:(j + 1) * LANES]

    att = [jnp.where(eye, _nt(_bf(stack(p)), _bf(stack(t))), 0.0) for p, t in zip(qj, kj)]
    for li in range(len(levels)):
        ql = [_bf(stack(jnp.where(second_half[li], p * exl(2 + li, c, j), 0.0)))
              for p, (c, j) in zip(qj, gitems)]
        kl = [_bf(stack(jnp.where(second_half[li], 0.0, t * exl(2 + li, c, j))))
              for t, (c, j) in zip(kj, gitems)]
        att = [v + jnp.where(lmask[li], _nt(p, t), 0.0) for v, p, t in zip(att, ql, kl)]
    gv_s = []
    for c, j in gitems:
        v0 = C_V + 2 * j * GLA_DV
        gv_s.append(_bf(jnp.concatenate(
            [pbuf[rows(c), v0:v0 + GLA_DV], pbuf[rows(c), v0 + GLA_DV:v0 + 2 * GLA_DV]], axis=0)))
    att_v = [_mm(_bf(v), t) for v, t in zip(att, gv_s)]
    g_kv = [_tn(t, _bf(stack(p * exl(1, c, j)))) for t, p, (c, j) in zip(gv_s, kj, gitems)]
    q_e = [_bf(stack(p * exl(0, c, j))) for p, (c, j) in zip(qj, gitems)]
    decay = [exl(0, c, j)[L - 1:L] for c, j in gitems]

    g_start = [None] * len(gitems)
    if per_seq:
        for i, (c, j) in enumerate(gitems):
            g_start[i] = sg_out[c, j]
            sg_out[c, j] = g_start[i] * decay[i] + g_kv[i]
    else:
        for j in range(N_GLA_PAIRS):
            st = sg_out[0, j]
            for c in chunks:
                i = c * N_GLA_PAIRS + j
                g_start[i] = st
                st = st * decay[i] + g_kv[i]
            sg_out[0, j] = st
    o_inter = [_nt(p, _bf(s)) for p, s in zip(q_e, g_start)]
    for i, (c, j) in enumerate(gitems):
        o = att_v[i] + o_inter[i]
        o = o * lax.rsqrt(jnp.mean(o * o, -1, keepdims=True) + RMS_EPS) * gnorm[...]
        for hh in range(2):
            c0 = (2 * j + hh) * GLA_DV
            zg = pbuf[rows(c), C_ZG + c0:C_ZG + c0 + GLA_DV]
            mbuf[rows(c), c0:c0 + GLA_DV] = _bf(o[hh * L:(hh + 1) * L] * (zg * sigmoid(zg)))

    o = _mm(mbuf[...], wout[...])
    o = o * lax.rsqrt(jnp.mean(o * o, -1, keepdims=True) + RMS_EPS) * npost[...]
    y_ref[...] = x_ref[...] + o


def _layer_call(layer, x2d, sg, sr, ss, params, consts, *, n_groups, tiles_per_group, tile_rows,
                L, per_seq):
    nseq = sg.shape[0] // n_groups

    def row_spec(width):
        return pl.BlockSpec((tile_rows, width), lambda b, t: (b * tiles_per_group + t, 0))

    def state_spec(arr):
        nd = arr.ndim
        return pl.BlockSpec((nseq,) + arr.shape[1:], lambda b, t: (b,) + (0,) * (nd - 1))

    def layer_spec(arr):
        return pl.BlockSpec((None,) + arr.shape[1:], lambda b, t: (layer,) + (0,) * (arr.ndim - 1))

    def const_spec(arr):
        return pl.BlockSpec(arr.shape, lambda b, t: (0,) * arr.ndim)

    in_specs = ([row_spec(D_MODEL), state_spec(sg), state_spec(sr), state_spec(ss)]
                + [layer_spec(p) for p in params] + [const_spec(c) for c in consts])
    out_specs = [row_spec(D_MODEL), state_spec(sg), state_spec(sr), state_spec(ss)]
    out_shape = [jax.ShapeDtypeStruct(x2d.shape, F32), jax.ShapeDtypeStruct(sg.shape, F32),
                 jax.ShapeDtypeStruct(sr.shape, F32), jax.ShapeDtypeStruct(ss.shape, F32)]
    return pl.pallas_call(
        functools.partial(_layer_kernel, tile_rows=tile_rows, L=L, per_seq=per_seq),
        grid=(n_groups, tiles_per_group),
        in_specs=in_specs,
        out_specs=out_specs,
        out_shape=out_shape,
        scratch_shapes=[pltpu.VMEM((tile_rows, N_COLS), F32), pltpu.VMEM((tile_rows, D_MODEL), BF16)],
        compiler_params=pltpu.CompilerParams(
            dimension_semantics=("arbitrary", "arbitrary"), vmem_limit_bytes=VMEM_LIMIT_BYTES),
    )(x2d, sg, sr, ss, *params, *consts)


def _gla_state_to_kernel(s):
    n = s.shape[0]
    s = s.reshape(n, N_GLA_PAIRS, 2, GLA_DK, GLA_DV).transpose(0, 1, 4, 2, 3)
    return s.reshape(n, N_GLA_PAIRS, GLA_DV, LANES)


def _gla_state_from_kernel(s):
    n = s.shape[0]
    s = s.reshape(n, N_GLA_PAIRS, GLA_DV, 2, GLA_DK).transpose(0, 1, 3, 4, 2)
    return s.reshape(n, GLA_HEADS, GLA_DK, GLA_DV)


def _rwkv_state_to_kernel(s):
    n = s.shape[0]
    s = s.reshape(n, N_RWKV_PAIRS, 2, RWKV_HEAD, RWKV_HEAD)
    z = jnp.zeros_like(s[:, :, 0])
    top = jnp.concatenate([s[:, :, 0], z], axis=-1)
    bot = jnp.concatenate([z, s[:, :, 1]], axis=-1)
    return jnp.concatenate([top, bot], axis=-2)


def _rwkv_state_from_kernel(s):
    n = s.shape[0]
    H = RWKV_HEAD
    return jnp.stack([s[:, :, :H, :H], s[:, :, H:, H:]], axis=2).reshape(n, RWKV_HEADS, H, H)


def _row(p):
    return p.reshape(DEPTH, 1, -1)


def _constants(L):
    seg = np.kron(np.eye(2, dtype=np.float32), np.ones((RWKV_HEAD, RWKV_HEAD), np.float32))
    e_mat = _gla_exponent_matrix(L)
    tri = np.tril(np.ones((L, L), np.float32))
    return (jnp.asarray(np.tile(e_mat, (1, 3)), dtype=BF16),
            jnp.asarray(np.tile(tri, (1, 3)), dtype=BF16),
            jnp.asarray(np.concatenate([seg, seg], 0), dtype=BF16))


def kernel(x_prompt, x_sample, state_gla, state_rwkv, state_shift, norm_pre, norm_post, w_in,
           gla_a_up, gla_a_bias, gla_norm, rwkv_mu, rwkv_w_up, rwkv_w0, rwkv_a_up, rwkv_a0,
           rwkv_k_k, rwkv_k_a, rwkv_r_k, rwkv_ln_w, rwkv_ln_b, w_out):
    bp, seq, _ = x_prompt.shape
    bs, dec_seq, _ = x_sample.shape

    gd0 = 2 * GLA_KEY + 2 * GLA_WIDTH
    w_in_k = jnp.concatenate(
        [w_in[:, :, :gd0], w_in[:, :, gd0 + GLA_RANK:], w_in[:, :, gd0:gd0 + GLA_RANK],
         jnp.zeros((DEPTH, D_MODEL, LANES - GLA_RANK), w_in.dtype)], axis=-1).astype(BF16)
    aup = jnp.pad(gla_a_up, ((0, 0), (0, LANES - GLA_RANK), (0, 0))).astype(BF16)
    zpad = jnp.zeros_like(rwkv_w_up)
    waup = jnp.concatenate([jnp.concatenate([rwkv_w_up, zpad], 1),
                            jnp.concatenate([zpad, rwkv_a_up], 1)], 2).astype(BF16)
    params = (_row(norm_pre), _row(norm_post), w_in_k, aup, _row(gla_a_bias), _row(gla_norm),
              _row(rwkv_mu), waup, _row(rwkv_w0), _row(rwkv_a0), _row(rwkv_k_k), _row(rwkv_k_a),
              _row(rwkv_r_k), _row(rwkv_ln_w), _row(rwkv_ln_b), w_out.astype(BF16))

    prompt_tile = 256
    sample_seqs_per_tile = 8
    consts_p = _constants(CHUNK)
    consts_s = _constants(dec_seq)

    yp = x_prompt.reshape(bp * seq, D_MODEL)
    ys = x_sample.reshape(bs * dec_seq, D_MODEL)
    zg = jnp.zeros((bp, N_GLA_PAIRS, GLA_DV, LANES), F32)
    zr = jnp.zeros((bp, N_RWKV_PAIRS, LANES, LANES), F32)
    zs = jnp.zeros((bp, 1, SHIFT_W), F32)
    outs = [[] for _ in range(6)]
    for l in range(DEPTH):
        yp, g1, r1, s1 = _layer_call(
            l, yp, zg, zr, zs, params, consts_p, n_groups=bp, tiles_per_group=seq // prompt_tile,
            tile_rows=prompt_tile, L=CHUNK, per_seq=False)
        ys, g2, r2, s2 = _layer_call(
            l, ys, _gla_state_to_kernel(state_gla[l]), _rwkv_state_to_kernel(state_rwkv[l]),
            state_shift[l][:, None, :], params, consts_s, n_groups=bs // sample_seqs_per_tile,
            tiles_per_group=1, tile_rows=sample_seqs_per_tile * dec_seq, L=dec_seq, per_seq=True)
        for lst, val in zip(outs, (_gla_state_from_kernel(g1), _gla_state_from_kernel(g2),
                                   _rwkv_state_from_kernel(r1), _rwkv_state_from_kernel(r2),
                                   s1[:, 0], s2[:, 0])):
            lst.append(val)
    return (yp.reshape(bp, seq, D_MODEL), ys.reshape(bs, dec_seq, D_MODEL),
            jnp.stack(outs[0]), jnp.stack(outs[1]), jnp.stack(outs[2]), jnp.stack(outs[3]),
            jnp.stack(outs[4]), jnp.stack(outs[5]))
```

```python
import functools
import math

import numpy as np
import jax
import jax.numpy as jnp
from jax import lax
from jax.experimental import pallas as pl
from jax.experimental.pallas import tpu as pltpu

F32 = jnp.float32
BF16 = jnp.bfloat16

D_MODEL = 1024
DEPTH = 4
CHUNK = 64
RMS_EPS = 1e-6
GLA_WIDTH = 512
GLA_HEADS = 4
GLA_DV = 128
GLA_DK = 64
GLA_KEY = 256
GLA_RANK = 16
GLA_TEMP = 16.0
RWKV_WIDTH = 512
RWKV_HEAD = 64
RWKV_HEADS = 8
RWKV_RANK = 64
SHIFT_W = 3 * RWKV_WIDTH + 2 * RWKV_RANK
RWKV_GN_EPS = 64e-5
RWKV_DECAY_SCALE = 0.606531
LOG2E = math.log2(math.e)

LANES = 128
N_GLA_PAIRS = GLA_HEADS // 2
N_RWKV_PAIRS = RWKV_HEADS // 2

C_Q = 0
C_K = C_Q + GLA_KEY
C_V = C_K + GLA_KEY
C_ZG = C_V + GLA_WIDTH
C_SH = C_ZG + GLA_WIDTH
C_ZR = C_SH + SHIFT_W
C_GD = C_ZR + RWKV_WIDTH
N_COLS = C_GD + LANES

VMEM_LIMIT_BYTES = 56 * 1024 * 1024
ITEMS_PER_STEP = 4


def _nt(a, b):
    return lax.dot_general(a, b, (((1,), (1,)), ((), ())), preferred_element_type=F32)


def _tn(a, b):
    return lax.dot_general(a, b, (((0,), (0,)), ((), ())), preferred_element_type=F32)


def _mm(a, b):
    return jnp.dot(a, b, preferred_element_type=F32)


def _bf(v):
    return v.astype(BF16)


def _split2(v):
    hi = _bf(v)
    return hi, _bf(v - hi.astype(F32))


def _each(fn, *lists, group=ITEMS_PER_STEP):
    out = []
    for i, args in enumerate(zip(*lists)):
        out.append(fn(*args))
        if i % group == group - 1:
            yield None
    return out


def _alternate(ga, gb, stop_a, stop_b, ratio):
    done_a = done_b = False
    while not (done_a and done_b):
        for _ in range(ratio[0]):
            if not done_a:
                done_a = next(ga) == stop_a
        for _ in range(ratio[1]):
            if not done_b:
                done_b = next(gb) == stop_b


def _levels(L):
    out, h = [], L // 2
    while h >= 1:
        out.append(h)
        h //= 2
    return out


def _gla_exponent_matrix(L):
    t = np.arange(L)[:, None]
    i = np.arange(L)[None, :]
    blocks = [i <= t, i > t]
    for h in _levels(L):
        mid = (t // (2 * h)) * (2 * h) + h
        second = t >= mid
        blocks.append(np.where(second, (i >= mid) & (i <= t), (i > t) & (i < mid)))
    return np.concatenate(blocks, 0).astype(np.float32)


def _layer_kernel(x_ref, sg_in, sr_in, ss_in, npre, npost, win, aup, abias, gnorm, mu_ref,
                  waup, w0, a0, kk_w, ka_w, rk_w, lnw, lnb, wout, e_ref, tri_ref, seg_ref,
                  y_ref, sg_out, sr_out, ss_out, pbuf, mbuf, *, tile_rows, L, per_seq):
    TT = tile_rows
    nc = TT // L
    L2 = 2 * L
    W = RWKV_WIDTH
    levels = _levels(L)
    chunks = range(nc)

    @pl.when(pl.program_id(1) == 0)
    def _():
        sg_out[...] = sg_in[...]
        sr_out[...] = sr_in[...]
        ss_out[...] = ss_in[...]

    x = x_ref[...]
    hb = _bf(x * lax.rsqrt(jnp.mean(x * x, -1, keepdims=True) + RMS_EPS) * npre[...])

    lane = lax.broadcasted_iota(jnp.int32, (1, LANES), 1)
    head0 = lane < RWKV_HEAD
    tlane = lax.broadcasted_iota(jnp.int32, (1, L2), 1)
    tcol0 = tlane < L
    row = lax.broadcasted_iota(jnp.int32, (L, L2), 0)
    col = lax.broadcasted_iota(jnp.int32, (L, L2), 1) & (L - 1)
    eye = row == col
    strict_lower = row > col
    lower = row >= col
    crow = lax.broadcasted_iota(jnp.int32, (L, 1), 0)
    trow = lax.broadcasted_iota(jnp.int32, (TT, 1), 0)

    def level_mask(hsz, t_idx, s_idx):
        sh = hsz.bit_length()
        same = (t_idx >> sh) == (s_idx >> sh)
        return same & (((t_idx >> (sh - 1)) & 1) == 1) & (((s_idx >> (sh - 1)) & 1) == 0)

    lmask = [level_mask(hsz, row, col) for hsz in levels]
    src_t = lax.broadcasted_iota(jnp.int32, (L2, L), 0) & (L - 1)
    tok_t = lax.broadcasted_iota(jnp.int32, (L2, L), 1)
    eye_t = src_t == tok_t
    lmask_t = [level_mask(hsz, tok_t, src_t) for hsz in levels]
    second_half = [((crow >> (hsz.bit_length() - 1)) & 1) == 1 for hsz in levels]

    def rows(c):
        return slice(c * L, (c + 1) * L)

    def pick(v, c, j):
        return v[rows(c), j * LANES:(j + 1) * LANES]

    def project(c0, c1):
        pbuf[:, c0:c1] = _mm(hb, win[:, c0:c1])

    def stack(v, first=head0):
        return jnp.concatenate([jnp.where(first, v, 0.0), jnp.where(first, 0.0, v)], axis=0)

    def stack_t(v):
        return stack(v, tcol0)

    def segsum(v):
        vs = jnp.concatenate([v[:, i * LANES:(i + 1) * LANES] for i in range(4)], axis=0)
        s = _mm(jnp.concatenate(_split2(vs), axis=1), seg_ref[...])
        return jnp.concatenate([s[i * TT:(i + 1) * TT] for i in range(4)], axis=1)

    def chunk_cumsum(mat_ref, parts, c):
        return _mm(mat_ref[...], jnp.concatenate([p[rows(c)] for p in parts], axis=0))

    def last_row_bcast(v):
        return jnp.concatenate(
            [jnp.broadcast_to(v[(c + 1) * L - 1:(c + 1) * L], (L, v.shape[1])) for c in chunks], 0)

    def sigmoid(v):
        return 0.5 * jnp.tanh(0.5 * v) + 0.5

    def rwkv_stream():
        project(C_SH, C_GD)
        yield "proj"
        cur = pbuf[:, C_SH:C_SH + SHIFT_W]
        if per_seq:
            first = jnp.concatenate([jnp.broadcast_to(ss_out[c], (L, SHIFT_W)) for c in chunks], 0)
            prev = jnp.where((trow & (L - 1)) == 0, first, pltpu.roll(cur, 1, 0))
            for c in chunks:
                ss_out[c] = cur[(c + 1) * L - 1:(c + 1) * L]
        else:
            prev = jnp.where(trow == 0, ss_out[0], pltpu.roll(cur, 1, 0))
            ss_out[0] = cur[TT - 1:TT]
        yield None
        xm = cur + mu_ref[...] * (prev - cur)
        yield None
        r, kr, vr = xm[:, 0:W], xm[:, W:2 * W], xm[:, 2 * W:3 * W]
        wa = xm[:, 3 * W:3 * W + LANES]
        wa = jnp.where(head0, jnp.tanh(wa), wa)
        wa_pre = _mm(_bf(wa), waup[...])
        yield None
        log2_w = (-RWKV_DECAY_SCALE * LOG2E) * sigmoid(w0[...] + wa_pre[:, 0:W])
        yield None
        a = sigmoid(a0[...] + wa_pre[:, W:2 * W])
        yield None
        kk = kr * kk_w[...]
        kr = kr * (1.0 + (a - 1.0) * ka_w[...])
        yield None
        kk = kk * jnp.minimum(lax.rsqrt(segsum(kk * kk)), 1e12)
        yield None
        parts = _split2(log2_w)
        cum = jnp.concatenate([chunk_cumsum(tri_ref, parts, c) for c in chunks], axis=0)
        cum_last = last_row_bcast(cum)
        yield None
        g_inv = jnp.exp2(-cum)
        g_last = jnp.exp2(cum_last)
        beta = a * kk
        yield None
        alpha_t = -kk * jnp.exp2(cum - log2_w)
        yield None
        r_t = r * jnp.exp2(cum)
        yield None
        beta_h, k_h = beta * g_inv, kr * g_inv
        yield None
        beta_r, k_r = beta_h * g_last, k_h * g_last
        yield "elem"

        items = [(c, j) for c in chunks for j in range(N_RWKV_PAIRS)]

        def picked(v):
            return [_bf(pick(v, *cj)) for cj in items]

        def stacked(v):
            return _each(lambda cj: _bf(stack(pick(v, *cj))), items)

        def stacked_t(v):
            return _each(lambda cj: _bf(jnp.transpose(stack(pick(v, *cj)))), items)

        rt = picked(r_t)
        al_rt = [jnp.concatenate([p, q], axis=0) for p, q in zip(picked(alpha_t), rt)]
        bh_t = yield from stacked_t(beta_h)
        ab = yield from _each(_mm, al_rt, bh_t)
        a_ab = [jnp.where(strict_lower, v[0:L], 0.0) for v in ab]
        a_rb = [_bf(jnp.where(lower, v[L:L2], 0.0)) for v in ab]
        eyef = jnp.where(eye, 1.0, 0.0)
        d = yield from _each(lambda v: eyef + jnp.where(lmask[-1], v, 0.0), a_ab)
        kh_t = yield from stacked_t(k_h)
        ak = yield from _each(_mm, al_rt, kh_t)
        a_ak = [_bf(jnp.where(strict_lower, v[0:L], 0.0)) for v in ak]
        a_rk = [_bf(jnp.where(lower, v[L:L2], 0.0)) for v in ak]
        v_s = yield from stacked(vr)
        akv = yield from _each(_mm, a_ak, v_s)
        for li in range(len(levels) - 2, -1, -1):
            msk = lmask[li]
            dm = yield from _each(
                lambda p, v: _bf(_mm(_bf(p), _bf(stack_t(jnp.where(msk, v, 0.0))))), d, a_ab)
            d = yield from _each(lambda v, p: v + _mm(p, _bf(stack_t(v))), d, dm)
        al_s = yield from stacked(alpha_t)
        wu = yield from _each(
            lambda t, p, q: _mm(_bf(t), jnp.concatenate([p, _bf(stack(q))], axis=1)), d, al_s, akv)
        w_t = [_bf(v[:, 0:LANES]) for v in wu]
        u_p = [v[:, LANES:2 * LANES] for v in wu]
        y_kv = yield from _each(_mm, a_rk, v_s)
        kr_s = yield from stacked(k_r)
        s_kv = yield from _each(_tn, kr_s, v_s)
        br_s = yield from stacked(beta_r)
        g_col = yield from _each(
            lambda cj: jnp.transpose(jnp.broadcast_to(pick(g_last, *cj)[0:1], (LANES, LANES))), items)
        yield "mats"

        def rwkv_step(idx, states):
            stb = [_bf(s) for s in states]
            wr = [_mm(jnp.concatenate([w_t[i], rt[i]], axis=0), sb) for i, sb in zip(idx, stb)]
            yield None
            ub = [_bf(stack(v[0:L] + u_p[i])) for i, v in zip(idx, wr)]
            new = [s * g_col[i] + _tn(br_s[i], u) + s_kv[i] for i, s, u in zip(idx, states, ub)]
            yield None
            ys = [v[L:L2] + _mm(a_rb[i], u) + y_kv[i] for i, v, u in zip(idx, wr, ub)]
            yield None
            return ys, new

        y_items = [None] * len(items)
        if per_seq:
            for i0 in range(0, len(items), ITEMS_PER_STEP):
                idx = list(range(i0, i0 + ITEMS_PER_STEP))
                outs, new = yield from rwkv_step(idx, [sr_out[items[i]] for i in idx])
                for i, o, s in zip(idx, outs, new):
                    y_items[i] = o
                    sr_out[items[i]] = s
        else:
            states = [sr_out[0, j] for j in range(N_RWKV_PAIRS)]
            for c in chunks:
                idx = [c * N_RWKV_PAIRS + j for j in range(N_RWKV_PAIRS)]
                outs, states = yield from rwkv_step(idx, states)
                for i, o in zip(idx, outs):
                    y_items[i] = o
            for j in range(N_RWKV_PAIRS):
                sr_out[0, j] = states[j]
        y = jnp.concatenate(
            [jnp.concatenate([y_items[c * N_RWKV_PAIRS + j] for j in range(N_RWKV_PAIRS)], axis=1)
             for c in chunks], axis=0)
        yield None
        mean = segsum(y) * (1.0 / RWKV_HEAD)
        yc = y - mean
        yield None
        var = segsum(yc * yc) * (1.0 / RWKV_HEAD)
        yield None
        y = yc * lax.rsqrt(var + RWKV_GN_EPS) * lnw[...] + lnb[...]
        yield None
        bonus = segsum(r * kr * rk_w[...]) * vr
        yield None
        zr = pbuf[:, C_ZR:C_ZR + W]
        mbuf[:, GLA_WIDTH:GLA_WIDTH + W] = _bf((y + bonus) * (zr * sigmoid(zr)))
        while True:
            yield "end"

    def gla_stream():
        each1 = functools.partial(_each, group=1)
        project(C_GD, N_COLS)
        project(C_Q, C_V)
        yield None
        gl = _mm(_bf(pbuf[:, C_GD:C_GD + LANES]), aup[...]) + abias[...]
        log2_a = ((jnp.minimum(gl, 0.0) - jnp.log1p(jnp.exp(-jnp.abs(gl))))
                  * (LOG2E / GLA_TEMP))
        parts = _split2(log2_a)
        yield None
        project(C_V, C_ZG)
        yield None
        ex = []
        for c in chunks:
            ex.append(jnp.exp2(chunk_cumsum(e_ref, parts, c)))
            yield None
        project(C_ZG, C_SH)
        yield "decays"

        gitems = [(c, j) for c in chunks for j in range(N_GLA_PAIRS)]
        q = pbuf[:, C_Q:C_Q + GLA_KEY] * (GLA_DK ** -0.5)
        k = pbuf[:, C_K:C_K + GLA_KEY]
        qj = [pick(q, c, j) for c, j in gitems]
        kj = [pick(k, c, j) for c, j in gitems]

        def exl(blk, c, j):
            return ex[c][blk * L:(blk + 1) * L, j * LANES:(j + 1) * LANES]

        att = yield from each1(
            lambda p, t: jnp.where(eye_t, _nt(_bf(stack(t)), _bf(p)), 0.0), qj, kj)
        for li in range(len(levels)):
            sec, msk = second_half[li], lmask_t[li]

            def level_term(v, p, t, cj):
                xl = jnp.where(sec, p, t) * exl(2 + li, *cj)
                ql = _bf(jnp.where(sec, xl, 0.0))
                kl = _bf(stack(jnp.where(sec, 0.0, xl)))
                return v + jnp.where(msk, _nt(kl, ql), 0.0)

            att = yield from each1(level_term, att, qj, kj, gitems)
        yield "att"

        def value_rows(cj):
            c, j = cj
            v0 = C_V + 2 * j * GLA_DV
            return _bf(jnp.concatenate(
                [pbuf[rows(c), v0:v0 + GLA_DV], pbuf[rows(c), v0 + GLA_DV:v0 + 2 * GLA_DV]], 0))

        gv_s = yield from each1(value_rows, gitems)
        g_kv = yield from each1(
            lambda t, p, cj: _tn(_bf(stack(p * exl(1, *cj))), t), gv_s, kj, gitems)
        q_e = yield from each1(lambda p, cj: p * exl(0, *cj), qj, gitems)
        decay = yield from each1(
            lambda cj: jnp.transpose(jnp.broadcast_to(exl(0, *cj)[L - 1:L], (LANES, LANES))), gitems)

        g_start = [None] * len(gitems)
        if per_seq:
            for i, cj in enumerate(gitems):
                g_start[i] = sg_out[cj]
                sg_out[cj] = g_start[i] * decay[i] + g_kv[i]
        else:
            for j in range(N_GLA_PAIRS):
                st = sg_out[0, j]
                for c in chunks:
                    i = c * N_GLA_PAIRS + j
                    g_start[i] = st
                    st = st * decay[i] + g_kv[i]
                sg_out[0, j] = st
        yield None
        for i, (c, j) in enumerate(gitems):
            gsb = _bf(g_start[i])
            attb = _bf(att[i])
            for hh in range(2):
                mine_k = head0 if hh == 0 else jnp.logical_not(head0)
                o = (_tn(attb[hh * L:(hh + 1) * L], gv_s[i][hh * L:(hh + 1) * L])
                     + _mm(_bf(jnp.where(mine_k, q_e[i], 0.0)), gsb))
                o = o * lax.rsqrt(jnp.mean(o * o, -1, keepdims=True) + RMS_EPS) * gnorm[...]
                c0 = (2 * j + hh) * GLA_DV
                zg = pbuf[rows(c), C_ZG + c0:C_ZG + c0 + GLA_DV]
                mbuf[rows(c), c0:c0 + GLA_DV] = _bf(o * (zg * sigmoid(zg)))
            yield None
        while True:
            yield "end"

    rw, gl = rwkv_stream(), gla_stream()
    while next(rw) != "proj":
        pass
    _alternate(rw, gl, "elem", "decays", (2, 1))
    _alternate(rw, gl, "mats", "att", (2, 1))
    _alternate(rw, gl, "end", "end", (1, 2))

    o = _mm(mbuf[...], wout[...])
    o = o * lax.rsqrt(jnp.mean(o * o, -1, keepdims=True) + RMS_EPS) * npost[...]
    y_ref[...] = x_ref[...] + o


def _layer_call(layer, x2d, sg, sr, ss, params, consts, *, n_groups, tiles_per_group, tile_rows,
                L, per_seq):
    nseq = sg.shape[0] // n_groups

    def row_spec(width):
        return pl.BlockSpec((tile_rows, width), lambda b, t: (b * tiles_per_group + t, 0))

    def state_spec(arr):
        nd = arr.ndim
        return pl.BlockSpec((nseq,) + arr.shape[1:], lambda b, t: (b,) + (0,) * (nd - 1))

    def layer_spec(arr):
        return pl.BlockSpec((None,) + arr.shape[1:], lambda b, t: (layer,) + (0,) * (arr.ndim - 1))

    def const_spec(arr):
        return pl.BlockSpec(arr.shape, lambda b, t: (0,) * arr.ndim)

    in_specs = ([row_spec(D_MODEL), state_spec(sg), state_spec(sr), state_spec(ss)]
                + [layer_spec(p) for p in params] + [const_spec(c) for c in consts])
    out_specs = [row_spec(D_MODEL), state_spec(sg), state_spec(sr), state_spec(ss)]
    out_shape = [jax.ShapeDtypeStruct(x2d.shape, F32), jax.ShapeDtypeStruct(sg.shape, F32),
                 jax.ShapeDtypeStruct(sr.shape, F32), jax.ShapeDtypeStruct(ss.shape, F32)]
    return pl.pallas_call(
        functools.partial(_layer_kernel, tile_rows=tile_rows, L=L, per_seq=per_seq),
        grid=(n_groups, tiles_per_group),
        in_specs=in_specs,
        out_specs=out_specs,
        out_shape=out_shape,
        scratch_shapes=[pltpu.VMEM((tile_rows, N_COLS), F32), pltpu.VMEM((tile_rows, D_MODEL), BF16)],
        compiler_params=pltpu.CompilerParams(
            dimension_semantics=("arbitrary", "arbitrary"), vmem_limit_bytes=VMEM_LIMIT_BYTES),
    )(x2d, sg, sr, ss, *params, *consts)


def _gla_state_to_kernel(s):
    return s.reshape(s.shape[0], N_GLA_PAIRS, LANES, GLA_DV)


def _gla_state_from_kernel(s):
    return s.reshape(s.shape[0], GLA_HEADS, GLA_DK, GLA_DV)


def _rwkv_state_to_kernel(s):
    n = s.shape[0]
    s = jnp.swapaxes(s, -1, -2).reshape(n, N_RWKV_PAIRS, 2, RWKV_HEAD, RWKV_HEAD)
    z = jnp.zeros_like(s[:, :, 0])
    top = jnp.concatenate([s[:, :, 0], z], axis=-1)
    bot = jnp.concatenate([z, s[:, :, 1]], axis=-1)
    return jnp.concatenate([top, bot], axis=-2)


def _rwkv_state_from_kernel(s):
    n = s.shape[0]
    H = RWKV_HEAD
    s = jnp.stack([s[:, :, :H, :H], s[:, :, H:, H:]], axis=2).reshape(n, RWKV_HEADS, H, H)
    return jnp.swapaxes(s, -1, -2)


def _row(p):
    return p.reshape(DEPTH, 1, -1)


def _constants(L):
    seg = np.kron(np.eye(2, dtype=np.float32), np.ones((RWKV_HEAD, RWKV_HEAD), np.float32))
    e_mat = _gla_exponent_matrix(L)
    tri = np.tril(np.ones((L, L), np.float32))
    return (jnp.asarray(np.tile(e_mat, (1, 2)), dtype=BF16),
            jnp.asarray(np.tile(tri, (1, 2)), dtype=BF16),
            jnp.asarray(np.concatenate([seg, seg], 0), dtype=BF16))


def kernel(x_prompt, x_sample, state_gla, state_rwkv, state_shift, norm_pre, norm_post, w_in,
           gla_a_up, gla_a_bias, gla_norm, rwkv_mu, rwkv_w_up, rwkv_w0, rwkv_a_up, rwkv_a0,
           rwkv_k_k, rwkv_k_a, rwkv_r_k, rwkv_ln_w, rwkv_ln_b, w_out):
    bp, seq, _ = x_prompt.shape
    bs, dec_seq, _ = x_sample.shape

    gd0 = 2 * GLA_KEY + 2 * GLA_WIDTH
    w_in_k = jnp.concatenate(
        [w_in[:, :, :gd0], w_in[:, :, gd0 + GLA_RANK:], w_in[:, :, gd0:gd0 + GLA_RANK],
         jnp.zeros((DEPTH, D_MODEL, LANES - GLA_RANK), w_in.dtype)], axis=-1).astype(BF16)
    aup = jnp.pad(gla_a_up, ((0, 0), (0, LANES - GLA_RANK), (0, 0))).astype(BF16)
    zpad = jnp.zeros_like(rwkv_w_up)
    waup = jnp.concatenate([jnp.concatenate([rwkv_w_up, zpad], 1),
                            jnp.concatenate([zpad, rwkv_a_up], 1)], 2).astype(BF16)
    params = (_row(norm_pre), _row(norm_post), w_in_k, aup, _row(gla_a_bias), _row(gla_norm),
              _row(rwkv_mu), waup, _row(rwkv_w0), _row(rwkv_a0), _row(rwkv_k_k), _row(rwkv_k_a),
              _row(rwkv_r_k), _row(rwkv_ln_w), _row(rwkv_ln_b), w_out.astype(BF16))

    prompt_tile = 512
    sample_seqs_per_tile = 8
    consts_p = _constants(CHUNK)
    consts_s = _constants(dec_seq)

    yp = x_prompt.reshape(bp * seq, D_MODEL)
    ys = x_sample.reshape(bs * dec_seq, D_MODEL)
    zg = jnp.zeros((bp, N_GLA_PAIRS, LANES, GLA_DV), F32)
    zr = jnp.zeros((bp, N_RWKV_PAIRS, LANES, LANES), F32)
    zs = jnp.zeros((bp, 1, SHIFT_W), F32)
    outs = [[] for _ in range(6)]
    for l in range(DEPTH):
        yp, g1, r1, s1 = _layer_call(
            l, yp, zg, zr, zs, params, consts_p, n_groups=bp, tiles_per_group=seq // prompt_tile,
            tile_rows=prompt_tile, L=CHUNK, per_seq=False)
        ys, g2, r2, s2 = _layer_call(
            l, ys, _gla_state_to_kernel(state_gla[l]), _rwkv_state_to_kernel(state_rwkv[l]),
            state_shift[l][:, None, :], params, consts_s, n_groups=bs // sample_seqs_per_tile,
            tiles_per_group=1, tile_rows=sample_seqs_per_tile * dec_seq, L=dec_seq, per_seq=True)
        for lst, val in zip(outs, (_gla_state_from_kernel(g1), _gla_state_from_kernel(g2),
                                   _rwkv_state_from_kernel(r1), _rwkv_state_from_kernel(r2),
                                   s1[:, 0], s2[:, 0])):
            lst.append(val)
    return (yp.reshape(bp, seq, D_MODEL), ys.reshape(bs, dec_seq, D_MODEL),
            jnp.stack(outs[0]), jnp.stack(outs[1]), jnp.stack(outs[2]), jnp.stack(outs[3]),
            jnp.stack(outs[4]), jnp.stack(outs[5]))
```

```python
import functools
import math

import numpy as np
import jax
import jax.numpy as jnp
from jax import lax
from jax.experimental import pallas as pl
from jax.experimental.pallas import tpu as pltpu

F32 = jnp.float32
BF16 = jnp.bfloat16

D_MODEL = 1024
DEPTH = 4
CHUNK = 64
RMS_EPS = 1e-6
GLA_WIDTH = 512
GLA_HEADS = 4
GLA_DV = 128
GLA_DK = 64
GLA_KEY = 256
GLA_RANK = 16
GLA_TEMP = 16.0
RWKV_WIDTH = 512
RWKV_HEAD = 64
RWKV_HEADS = 8
RWKV_RANK = 64
SHIFT_W = 3 * RWKV_WIDTH + 2 * RWKV_RANK
RWKV_GN_EPS = 64e-5
RWKV_DECAY_SCALE = 0.606531
LOG2E = math.log2(math.e)

LANES = 128
N_GLA_PAIRS = GLA_HEADS // 2
N_RWKV_PAIRS = RWKV_HEADS // 2

C_Q = 0
C_K = C_Q + GLA_KEY
C_V = C_K + GLA_KEY
C_ZG = C_V + GLA_WIDTH
C_SH = C_ZG + GLA_WIDTH
C_ZR = C_SH + SHIFT_W
C_GD = C_ZR + RWKV_WIDTH
N_COLS = C_GD + LANES

VMEM_LIMIT_BYTES = 56 * 1024 * 1024
ITEMS_PER_STEP = 4


def _nt(a, b):
    return lax.dot_general(a, b, (((1,), (1,)), ((), ())), preferred_element_type=F32)


def _tn(a, b):
    return lax.dot_general(a, b, (((0,), (0,)), ((), ())), preferred_element_type=F32)


def _mm(a, b):
    return jnp.dot(a, b, preferred_element_type=F32)


def _bf(v):
    return v.astype(BF16)


def _split2(v):
    hi = _bf(v)
    return hi, _bf(v - hi.astype(F32))


def _each(fn, *lists, group=ITEMS_PER_STEP):
    out = []
    for i, args in enumerate(zip(*lists)):
        out.append(fn(*args))
        if i % group == group - 1:
            yield None
    return out


def _alternate(ga, gb, stop_a, stop_b, ratio):
    done_a = done_b = False
    while not (done_a and done_b):
        for _ in range(ratio[0]):
            if not done_a:
                done_a = next(ga) == stop_a
        for _ in range(ratio[1]):
            if not done_b:
                done_b = next(gb) == stop_b


def _levels(L):
    out, h = [], L // 2
    while h >= 1:
        out.append(h)
        h //= 2
    return out


def _gla_exponent_matrix(L):
    t = np.arange(L)[:, None]
    i = np.arange(L)[None, :]
    blocks = [i <= t, i > t]
    for h in _levels(L):
        mid = (t // (2 * h)) * (2 * h) + h
        second = t >= mid
        blocks.append(np.where(second, (i >= mid) & (i <= t), (i > t) & (i < mid)))
    return np.concatenate(blocks, 0).astype(np.float32)


def _layer_kernel(x_ref, sg_in, sr_in, ss_in, npre, npost, win_gla, win_rwkv, win_gd, aup, abias,
                  gnorm, mu_ref, waup, w0, a0, kk_w, ka_w, rk_w, lnw, lnb, wout, e_ref, tri_ref, seg_ref,
                  y_ref, sg_out, sr_out, ss_out, pbuf, mbuf, *, tile_rows, L, per_seq):
    TT = tile_rows
    nc = TT // L
    L2 = 2 * L
    W = RWKV_WIDTH
    levels = _levels(L)
    chunks = range(nc)

    @pl.when(pl.program_id(1) == 0)
    def _():
        sg_out[...] = sg_in[...]
        sr_out[...] = sr_in[...]
        ss_out[...] = ss_in[...]

    x = x_ref[...]
    hb = _bf(x * lax.rsqrt(jnp.mean(x * x, -1, keepdims=True) + RMS_EPS) * npre[...])

    lane = lax.broadcasted_iota(jnp.int32, (1, LANES), 1)
    head0 = lane < RWKV_HEAD
    tlane = lax.broadcasted_iota(jnp.int32, (1, L2), 1)
    tcol0 = tlane < L
    row = lax.broadcasted_iota(jnp.int32, (L, L2), 0)
    col = lax.broadcasted_iota(jnp.int32, (L, L2), 1) & (L - 1)
    eye = row == col
    strict_lower = row > col
    lower = row >= col
    crow = lax.broadcasted_iota(jnp.int32, (L, 1), 0)
    trow = lax.broadcasted_iota(jnp.int32, (TT, 1), 0)

    def level_mask(hsz, t_idx, s_idx):
        sh = hsz.bit_length()
        same = (t_idx >> sh) == (s_idx >> sh)
        return same & (((t_idx >> (sh - 1)) & 1) == 1) & (((s_idx >> (sh - 1)) & 1) == 0)

    lmask = [level_mask(hsz, row, col) for hsz in levels]
    src_t = lax.broadcasted_iota(jnp.int32, (L2, L), 0) & (L - 1)
    tok_t = lax.broadcasted_iota(jnp.int32, (L2, L), 1)
    eye_t = src_t == tok_t
    same_head_t = ((lax.broadcasted_iota(jnp.int32, (LANES, L2), 0) < RWKV_HEAD)
                   == (lax.broadcasted_iota(jnp.int32, (LANES, L2), 1) < L))
    lmask_t = [level_mask(hsz, tok_t, src_t) for hsz in levels]
    second_half = [((crow >> (hsz.bit_length() - 1)) & 1) == 1 for hsz in levels]

    def rows(c):
        return slice(c * L, (c + 1) * L)

    def pick(v, c, j):
        return v[rows(c), j * LANES:(j + 1) * LANES]

    def project(c0, c1):
        w_ref, base = ((win_gla, C_Q) if c1 <= C_SH else
                       (win_rwkv, C_SH) if c1 <= C_GD else (win_gd, C_GD))
        pbuf[:, c0:c1] = _mm(hb, w_ref[:, c0 - base:c1 - base])

    def stack(v, first=head0):
        zero = jnp.zeros((), v.dtype)
        return jnp.concatenate([jnp.where(first, v, zero), jnp.where(first, zero, v)], axis=0)

    def stack_t(v):
        return stack(v, tcol0)

    def segsum(v):
        vs = jnp.concatenate([v[:, i * LANES:(i + 1) * LANES] for i in range(4)], axis=0)
        s = _mm(jnp.concatenate(_split2(vs), axis=1), seg_ref[...])
        return jnp.concatenate([s[i * TT:(i + 1) * TT] for i in range(4)], axis=1)

    def chunk_cumsum(mat_ref, parts, c):
        return _mm(mat_ref[...], jnp.concatenate([p[rows(c)] for p in parts], axis=0))

    def last_row_bcast(v):
        return jnp.concatenate(
            [jnp.broadcast_to(v[(c + 1) * L - 1:(c + 1) * L], (L, v.shape[1])) for c in chunks], 0)

    def sigmoid(v):
        return 0.5 * jnp.tanh(0.5 * v) + 0.5

    def rwkv_stream():
        project(C_SH, C_GD)
        yield "proj"
        cur = pbuf[:, C_SH:C_SH + SHIFT_W]
        if per_seq:
            first = jnp.concatenate([jnp.broadcast_to(ss_out[c], (L, SHIFT_W)) for c in chunks], 0)
            prev = jnp.where((trow & (L - 1)) == 0, first, pltpu.roll(cur, 1, 0))
            for c in chunks:
                ss_out[c] = cur[(c + 1) * L - 1:(c + 1) * L]
        else:
            prev = jnp.where(trow == 0, ss_out[0], pltpu.roll(cur, 1, 0))
            ss_out[0] = cur[TT - 1:TT]
        yield None
        xm = cur + mu_ref[...] * (prev - cur)
        yield None
        r, kr, vr = xm[:, 0:W], xm[:, W:2 * W], xm[:, 2 * W:3 * W]
        wa = xm[:, 3 * W:3 * W + LANES]
        wa = jnp.where(head0, jnp.tanh(wa), wa)
        wa_pre = _mm(_bf(wa), waup[...])
        yield None
        log2_w = (-RWKV_DECAY_SCALE * LOG2E) * sigmoid(w0[...] + wa_pre[:, 0:W])
        yield None
        a = sigmoid(a0[...] + wa_pre[:, W:2 * W])
        yield None
        kk = kr * kk_w[...]
        kr = kr * (1.0 + (a - 1.0) * ka_w[...])
        yield None
        kk = kk * jnp.minimum(lax.rsqrt(segsum(kk * kk)), 1e12)
        yield None
        parts = _split2(log2_w)
        cum = jnp.concatenate([chunk_cumsum(tri_ref, parts, c) for c in chunks], axis=0)
        cum_last = last_row_bcast(cum)
        yield None
        g_inv = jnp.exp2(-cum)
        g_last = jnp.exp2(cum_last)
        beta = a * kk
        yield None
        alpha_t = -kk * jnp.exp2(cum - log2_w)
        yield None
        r_t = r * jnp.exp2(cum)
        yield None
        beta_h, k_h = beta * g_inv, kr * g_inv
        yield None
        beta_r, k_r = beta_h * g_last, k_h * g_last
        yield "elem"

        items = [(c, j) for c in chunks for j in range(N_RWKV_PAIRS)]

        def picked(v):
            return [_bf(pick(v, *cj)) for cj in items]

        def stacked(v):
            return _each(lambda cj: stack(_bf(pick(v, *cj))), items)

        def stacked_t(v):
            def one(cj):
                v2 = pick(v, *cj)
                vt = _bf(jnp.transpose(jnp.concatenate([v2, v2], axis=0)))
                return jnp.where(same_head_t, vt, jnp.zeros((), BF16))

            return _each(one, items)

        rt = picked(r_t)
        al_rt = [jnp.concatenate([p, q], axis=0) for p, q in zip(picked(alpha_t), rt)]
        bh_t = yield from stacked_t(beta_h)
        ab = yield from _each(_mm, al_rt, bh_t)
        a_ab = [v[0:L] for v in ab]
        a_rb = [_bf(jnp.where(lower, v[L:L2], 0.0)) for v in ab]
        eyef = jnp.where(eye, 1.0, 0.0)
        d = yield from _each(lambda v: eyef + jnp.where(lmask[-1], v, 0.0), a_ab)
        kh_t = yield from stacked_t(k_h)
        ak = yield from _each(_mm, al_rt, kh_t)
        a_ak = [_bf(jnp.where(strict_lower, v[0:L], 0.0)) for v in ak]
        a_rk = [_bf(jnp.where(lower, v[L:L2], 0.0)) for v in ak]
        v_s = yield from stacked(vr)
        akv = yield from _each(_mm, a_ak, v_s)
        for li in range(len(levels) - 2, -1, -1):
            msk = lmask[li]
            db = yield from _each(_bf, d)
            dm = yield from _each(
                lambda p, v: _bf(_mm(p, stack_t(_bf(jnp.where(msk, v, 0.0))))), db, a_ab)
            d = yield from _each(lambda v, p, q: v + _mm(p, stack_t(q)), d, dm, db)
        al_s = yield from stacked(alpha_t)
        wu = yield from _each(
            lambda t, p, q: _mm(_bf(t), jnp.concatenate([p, stack(_bf(q))], axis=1)), d, al_s, akv)
        w_t = [_bf(v[:, 0:LANES]) for v in wu]
        u_p = [v[:, LANES:2 * LANES] for v in wu]
        y_kv = yield from _each(_mm, a_rk, v_s)
        kr_s = yield from stacked(k_r)
        s_kv = yield from _each(_tn, kr_s, v_s)
        br_s = yield from stacked(beta_r)
        g_col = yield from _each(
            lambda cj: jnp.transpose(jnp.broadcast_to(pick(g_last, *cj)[0:1], (LANES, LANES))), items)
        yield "mats"

        def rwkv_step(idx, states):
            stb = [_bf(s) for s in states]
            wr = [_mm(jnp.concatenate([w_t[i], rt[i]], axis=0), sb) for i, sb in zip(idx, stb)]
            yield None
            ub = [stack(_bf(v[0:L] + u_p[i])) for i, v in zip(idx, wr)]
            new = [s * g_col[i] + _tn(br_s[i], u) + s_kv[i] for i, s, u in zip(idx, states, ub)]
            yield None
            ys = [v[L:L2] + _mm(a_rb[i], u) + y_kv[i] for i, v, u in zip(idx, wr, ub)]
            yield None
            return ys, new

        y_items = [None] * len(items)
        if per_seq:
            for i0 in range(0, len(items), ITEMS_PER_STEP):
                idx = list(range(i0, i0 + ITEMS_PER_STEP))
                outs, new = yield from rwkv_step(idx, [sr_out[items[i]] for i in idx])
                for i, o, s in zip(idx, outs, new):
                    y_items[i] = o
                    sr_out[items[i]] = s
        else:
            states = [sr_out[0, j] for j in range(N_RWKV_PAIRS)]
            for c in chunks:
                idx = [c * N_RWKV_PAIRS + j for j in range(N_RWKV_PAIRS)]
                outs, states = yield from rwkv_step(idx, states)
                for i, o in zip(idx, outs):
                    y_items[i] = o
            for j in range(N_RWKV_PAIRS):
                sr_out[0, j] = states[j]
        y = jnp.concatenate(
            [jnp.concatenate([y_items[c * N_RWKV_PAIRS + j] for j in range(N_RWKV_PAIRS)], axis=1)
             for c in chunks], axis=0)
        yield None
        mean = segsum(y) * (1.0 / RWKV_HEAD)
        yc = y - mean
        yield None
        var = segsum(yc * yc) * (1.0 / RWKV_HEAD)
        yield None
        y = yc * lax.rsqrt(var + RWKV_GN_EPS) * lnw[...] + lnb[...]
        yield None
        bonus = segsum(r * kr * rk_w[...]) * vr
        yield None
        zr = pbuf[:, C_ZR:C_ZR + W]
        mbuf[:, GLA_WIDTH:GLA_WIDTH + W] = _bf((y + bonus) * (zr * sigmoid(zr)))
        while True:
            yield "end"

    def gla_stream():
        each1 = functools.partial(_each, group=1)
        project(C_GD, N_COLS)
        project(C_Q, C_V)
        yield None
        gl = _mm(_bf(pbuf[:, C_GD:C_GD + LANES]), aup[...]) + abias[...]
        log2_a = ((jnp.minimum(gl, 0.0) - jnp.log1p(jnp.exp(-jnp.abs(gl))))
                  * (LOG2E / GLA_TEMP))
        parts = _split2(log2_a)
        yield None
        project(C_V, C_ZG)
        yield None
        ex = []
        for c in chunks:
            ex.append(jnp.exp2(chunk_cumsum(e_ref, parts, c)))
            yield None
        project(C_ZG, C_SH)
        yield "decays"

        gitems = [(c, j) for c in chunks for j in range(N_GLA_PAIRS)]
        q = pbuf[:, C_Q:C_Q + GLA_KEY] * (GLA_DK ** -0.5)
        k = pbuf[:, C_K:C_K + GLA_KEY]
        qj = [pick(q, c, j) for c, j in gitems]
        kj = [pick(k, c, j) for c, j in gitems]

        def exl(blk, c, j):
            return ex[c][blk * L:(blk + 1) * L, j * LANES:(j + 1) * LANES]

        att = yield from each1(
            lambda p, t: jnp.where(eye_t, _nt(stack(_bf(t)), _bf(p)), 0.0), qj, kj)
        for li in range(len(levels)):
            sec, msk = second_half[li], lmask_t[li]

            def level_term(v, p, t, cj):
                xl = jnp.where(sec, p, t) * exl(2 + li, *cj)
                ql = _bf(jnp.where(sec, xl, 0.0))
                kl = stack(_bf(jnp.where(sec, 0.0, xl)))
                return v + jnp.where(msk, _nt(kl, ql), 0.0)

            att = yield from each1(level_term, att, qj, kj, gitems)
        yield "att"

        def value_rows(cj):
            c, j = cj
            v0 = C_V + 2 * j * GLA_DV
            return _bf(jnp.concatenate(
                [pbuf[rows(c), v0:v0 + GLA_DV], pbuf[rows(c), v0 + GLA_DV:v0 + 2 * GLA_DV]], 0))

        gv_s = yield from each1(value_rows, gitems)
        g_kv = yield from each1(
            lambda t, p, cj: _tn(stack(_bf(p * exl(1, *cj))), t), gv_s, kj, gitems)
        q_e = yield from each1(lambda p, cj: p * exl(0, *cj), qj, gitems)
        decay = yield from each1(
            lambda cj: jnp.transpose(jnp.broadcast_to(exl(0, *cj)[L - 1:L], (LANES, LANES))), gitems)

        g_start = [None] * len(gitems)
        if per_seq:
            for i, cj in enumerate(gitems):
                g_start[i] = sg_out[cj]
                sg_out[cj] = g_start[i] * decay[i] + g_kv[i]
        else:
            for j in range(N_GLA_PAIRS):
                st = sg_out[0, j]
                for c in chunks:
                    i = c * N_GLA_PAIRS + j
                    g_start[i] = st
                    st = st * decay[i] + g_kv[i]
                sg_out[0, j] = st
        yield None
        for i, (c, j) in enumerate(gitems):
            gsb = _bf(g_start[i])
            attb = _bf(att[i])
            for hh in range(2):
                mine_k = head0 if hh == 0 else jnp.logical_not(head0)
                o = (_tn(attb[hh * L:(hh + 1) * L], gv_s[i][hh * L:(hh + 1) * L])
                     + _mm(_bf(jnp.where(mine_k, q_e[i], 0.0)), gsb))
                o = o * lax.rsqrt(jnp.mean(o * o, -1, keepdims=True) + RMS_EPS) * gnorm[...]
                c0 = (2 * j + hh) * GLA_DV
                zg = pbuf[rows(c), C_ZG + c0:C_ZG + c0 + GLA_DV]
                mbuf[rows(c), c0:c0 + GLA_DV] = _bf(o * (zg * sigmoid(zg)))
            yield None
        while True:
            yield "end"

    rw, gl = rwkv_stream(), gla_stream()
    while next(rw) != "proj":
        pass
    _alternate(rw, gl, "elem", "decays", (2, 1))
    _alternate(rw, gl, "mats", "att", (2, 1))
    _alternate(rw, gl, "end", "end", (1, 2))

    o = _mm(mbuf[...], wout[...])
    o = o * lax.rsqrt(jnp.mean(o * o, -1, keepdims=True) + RMS_EPS) * npost[...]
    y_ref[...] = x_ref[...] + o


def _layer_call(layer, state_layer, x2d, sg, sr, ss, params, consts, *, n_groups,
                tiles_per_group, tile_rows, L, per_seq):
    nseq = sg.shape[1] // n_groups

    def row_spec(width):
        return pl.BlockSpec((tile_rows, width), lambda b, t: (b * tiles_per_group + t, 0))

    def state_in_spec(arr):
        nd = arr.ndim
        return pl.BlockSpec((None, nseq) + arr.shape[2:],
                            lambda b, t: (state_layer, b) + (0,) * (nd - 2))

    def state_out_spec(arr):
        nd = arr.ndim - 1
        return pl.BlockSpec((nseq,) + arr.shape[2:], lambda b, t: (b,) + (0,) * (nd - 1))

    def layer_spec(arr):
        return pl.BlockSpec((None,) + arr.shape[1:], lambda b, t: (layer,) + (0,) * (arr.ndim - 1))

    def const_spec(arr):
        return pl.BlockSpec(arr.shape, lambda b, t: (0,) * arr.ndim)

    states = (sg, sr, ss)
    in_specs = ([row_spec(D_MODEL)] + [state_in_spec(s) for s in states]
                + [layer_spec(p) for p in params] + [const_spec(c) for c in consts])
    out_specs = [row_spec(D_MODEL)] + [state_out_spec(s) for s in states]
    out_shape = ([jax.ShapeDtypeStruct(x2d.shape, F32)]
                 + [jax.ShapeDtypeStruct(s.shape[1:], F32) for s in states])
    return pl.pallas_call(
        functools.partial(_layer_kernel, tile_rows=tile_rows, L=L, per_seq=per_seq),
        grid=(n_groups, tiles_per_group),
        in_specs=in_specs,
        out_specs=out_specs,
        out_shape=out_shape,
        scratch_shapes=[pltpu.VMEM((tile_rows, N_COLS), F32), pltpu.VMEM((tile_rows, D_MODEL), BF16)],
        compiler_params=pltpu.CompilerParams(
            dimension_semantics=("arbitrary", "arbitrary"), vmem_limit_bytes=VMEM_LIMIT_BYTES),
    )(x2d, sg, sr, ss, *params, *consts)


def _gla_state_to_kernel(s):
    return s.reshape(s.shape[:-3] + (N_GLA_PAIRS, LANES, GLA_DV))


def _gla_state_from_kernel(s):
    return s.reshape(s.shape[:-3] + (GLA_HEADS, GLA_DK, GLA_DV))


def _rwkv_state_to_kernel(s):
    H = RWKV_HEAD
    s = jnp.swapaxes(s, -1, -2).reshape(s.shape[:-3] + (N_RWKV_PAIRS, 2, H, H))
    z = jnp.zeros_like(s[..., 0, :, :])
    top = jnp.concatenate([s[..., 0, :, :], z], axis=-1)
    bot = jnp.concatenate([z, s[..., 1, :, :]], axis=-1)
    return jnp.concatenate([top, bot], axis=-2)


def _rwkv_state_from_kernel(s):
    H = RWKV_HEAD
    s = jnp.stack([s[..., :H, :H], s[..., H:, H:]], axis=-3)
    return jnp.swapaxes(s.reshape(s.shape[:-4] + (RWKV_HEADS, H, H)), -1, -2)


def _row(p):
    return p.reshape(DEPTH, 1, -1)


def _constants(L):
    seg = np.kron(np.eye(2, dtype=np.float32), np.ones((RWKV_HEAD, RWKV_HEAD), np.float32))
    e_mat = _gla_exponent_matrix(L)
    tri = np.tril(np.ones((L, L), np.float32))
    return (jnp.asarray(np.tile(e_mat, (1, 2)), dtype=BF16),
            jnp.asarray(np.tile(tri, (1, 2)), dtype=BF16),
            jnp.asarray(np.concatenate([seg, seg], 0), dtype=BF16))


def _tiling(seq, dec_seq):
    prompt_tile = 512 if seq % 512 == 0 else CHUNK
    sample_seqs_per_tile = 8
    return prompt_tile, sample_seqs_per_tile


def kernel(x_prompt, x_sample, state_gla, state_rwkv, state_shift, norm_pre, norm_post, w_in,
           gla_a_up, gla_a_bias, gla_norm, rwkv_mu, rwkv_w_up, rwkv_w0, rwkv_a_up, rwkv_a0,
           rwkv_k_k, rwkv_k_a, rwkv_r_k, rwkv_ln_w, rwkv_ln_b, w_out):
    bp, seq, _ = x_prompt.shape
    bs, dec_seq, _ = x_sample.shape

    gd0 = 2 * GLA_KEY + 2 * GLA_WIDTH
    w_gla = w_in[:, :, :gd0].astype(BF16)
    w_rwkv = w_in[:, :, gd0 + GLA_RANK:].astype(BF16)
    w_gd = jnp.pad(w_in[:, :, gd0:gd0 + GLA_RANK].astype(BF16),
                   ((0, 0), (0, 0), (0, LANES - GLA_RANK)))
    aup = jnp.pad(gla_a_up, ((0, 0), (0, LANES - GLA_RANK), (0, 0))).astype(BF16)
    zpad = jnp.zeros_like(rwkv_w_up)
    waup = jnp.concatenate([jnp.concatenate([rwkv_w_up, zpad], 1),
                            jnp.concatenate([zpad, rwkv_a_up], 1)], 2).astype(BF16)
    params = (_row(norm_pre), _row(norm_post), w_gla, w_rwkv, w_gd, aup, _row(gla_a_bias),
              _row(gla_norm), _row(rwkv_mu), waup, _row(rwkv_w0), _row(rwkv_a0), _row(rwkv_k_k),
              _row(rwkv_k_a), _row(rwkv_r_k), _row(rwkv_ln_w), _row(rwkv_ln_b), w_out.astype(BF16))

    prompt_tile, sample_seqs_per_tile = _tiling(seq, dec_seq)
    consts_p = _constants(CHUNK)
    consts_s = _constants(dec_seq)

    yp = x_prompt.reshape(bp * seq, D_MODEL)
    ys = x_sample.reshape(bs * dec_seq, D_MODEL)
    zero_states = (jnp.zeros((1, bp, N_GLA_PAIRS, LANES, GLA_DV), F32),
                   jnp.zeros((1, bp, N_RWKV_PAIRS, LANES, LANES), F32),
                   jnp.zeros((1, bp, 1, SHIFT_W), F32))
    sample_states = (_gla_state_to_kernel(state_gla), _rwkv_state_to_kernel(state_rwkv),
                     state_shift[:, :, None, :])
    outs = [[] for _ in range(6)]
    for l in range(DEPTH):
        yp, g1, r1, s1 = _layer_call(
            l, 0, yp, *zero_states, params, consts_p, n_groups=bp,
            tiles_per_group=seq // prompt_tile, tile_rows=prompt_tile, L=CHUNK, per_seq=False)
        ys, g2, r2, s2 = _layer_call(
            l, l, ys, *sample_states, params, consts_s, n_groups=bs // sample_seqs_per_tile,
            tiles_per_group=1, tile_rows=sample_seqs_per_tile * dec_seq, L=dec_seq, per_seq=True)
        for lst, val in zip(outs, (g1, g2, r1, r2, s1, s2)):
            lst.append(val)
    gla_p, gla_s, rwkv_p, rwkv_s, shift_p, shift_s = (jnp.stack(v) for v in outs)
    return (yp.reshape(bp, seq, D_MODEL), ys.reshape(bs, dec_seq, D_MODEL),
            _gla_state_from_kernel(gla_p), _gla_state_from_kernel(gla_s),
            _rwkv_state_from_kernel(rwkv_p), _rwkv_state_from_kernel(rwkv_s),
            shift_p[:, :, 0], shift_s[:, :, 0])
```

```python
import functools
import math

import numpy as np
import jax
import jax.numpy as jnp
from jax import lax
from jax.experimental import pallas as pl
from jax.experimental.pallas import tpu as pltpu

F32 = jnp.float32
BF16 = jnp.bfloat16

D_MODEL = 1024
DEPTH = 4
CHUNK = 64
RMS_EPS = 1e-6
GLA_WIDTH = 512
GLA_HEADS = 4
GLA_DV = 128
GLA_DK = 64
GLA_KEY = 256
GLA_RANK = 16
GLA_TEMP = 16.0
RWKV_WIDTH = 512
RWKV_HEAD = 64
RWKV_HEADS = 8
RWKV_RANK = 64
SHIFT_W = 3 * RWKV_WIDTH + 2 * RWKV_RANK
RWKV_GN_EPS = 64e-5
RWKV_DECAY_SCALE = 0.606531
LOG2E = math.log2(math.e)

LANES = 128
N_GLA_PAIRS = GLA_HEADS // 2
N_RWKV_PAIRS = RWKV_HEADS // 2

C_Q = 0
C_K = C_Q + GLA_KEY
C_V = C_K + GLA_KEY
C_ZG = C_V + GLA_WIDTH
C_SH = C_ZG + GLA_WIDTH
C_ZR = C_SH + SHIFT_W
C_GD = C_ZR + RWKV_WIDTH
N_COLS = C_GD + LANES

VMEM_LIMIT_BYTES = 56 * 1024 * 1024
ITEMS_PER_STEP = 4


def _nt(a, b):
    return lax.dot_general(a, b, (((1,), (1,)), ((), ())), preferred_element_type=F32)


def _tn(a, b):
    return lax.dot_general(a, b, (((0,), (0,)), ((), ())), preferred_element_type=F32)


def _mm(a, b):
    return jnp.dot(a, b, preferred_element_type=F32)


def _bf(v):
    return v.astype(BF16)


def _split2(v):
    hi = _bf(v)
    return hi, _bf(v - hi.astype(F32))


def _each(fn, *lists, group=ITEMS_PER_STEP):
    out = []
    for i, args in enumerate(zip(*lists)):
        out.append(fn(*args))
        if i % group == group - 1:
            yield None
    return out


def _alternate(ga, gb, stop_a, stop_b, ratio):
    done_a = done_b = False
    while not (done_a and done_b):
        for _ in range(ratio[0]):
            if not done_a:
                done_a = next(ga) == stop_a
        for _ in range(ratio[1]):
            if not done_b:
                done_b = next(gb) == stop_b


def _levels(L):
    out, h = [], L // 2
    while h >= 1:
        out.append(h)
        h //= 2
    return out


def _gla_exponent_matrix(L):
    t = np.arange(L)[:, None]
    i = np.arange(L)[None, :]
    blocks = [i <= t, i > t]
    for h in _levels(L):
        mid = (t // (2 * h)) * (2 * h) + h
        second = t >= mid
        blocks.append(np.where(second, (i >= mid) & (i <= t), (i > t) & (i < mid)))
    return np.concatenate(blocks, 0).astype(np.float32)


def _layer_kernel(x_ref, sg_in, sr_in, ss_in, npre, npost, win_gla, win_rwkv, win_gd, aup, abias,
                  gnorm, mu_ref, waup, w0, a0, kk_w, ka_w, rk_w, lnw, lnb, wout, e_ref, tri_ref, seg_ref,
                  y_ref, sg_out, sr_out, ss_out, pbuf, mbuf, *, tile_rows, L, per_seq):
    TT = tile_rows
    nc = TT // L
    L2 = 2 * L
    W = RWKV_WIDTH
    levels = _levels(L)
    chunks = range(nc)

    @pl.when(pl.program_id(1) == 0)
    def _():
        sg_out[...] = sg_in[...]
        sr_out[...] = sr_in[...]
        ss_out[...] = ss_in[...]

    x = x_ref[...]
    hb = _bf(x * lax.rsqrt(jnp.mean(x * x, -1, keepdims=True) + RMS_EPS) * npre[...])

    lane = lax.broadcasted_iota(jnp.int32, (1, LANES), 1)
    head0 = lane < RWKV_HEAD
    tlane = lax.broadcasted_iota(jnp.int32, (1, L2), 1)
    tcol0 = tlane < L
    row = lax.broadcasted_iota(jnp.int32, (L, L2), 0)
    col = lax.broadcasted_iota(jnp.int32, (L, L2), 1) & (L - 1)
    eye = row == col
    strict_lower = row > col
    lower = row >= col
    crow = lax.broadcasted_iota(jnp.int32, (L, 1), 0)
    trow = lax.broadcasted_iota(jnp.int32, (TT, 1), 0)

    def level_mask(hsz, t_idx, s_idx):
        sh = hsz.bit_length()
        same = (t_idx >> sh) == (s_idx >> sh)
        return same & (((t_idx >> (sh - 1)) & 1) == 1) & (((s_idx >> (sh - 1)) & 1) == 0)

    lmask = [level_mask(hsz, row, col) for hsz in levels]
    src_t = lax.broadcasted_iota(jnp.int32, (L2, L), 0) & (L - 1)
    tok_t = lax.broadcasted_iota(jnp.int32, (L2, L), 1)
    eye_t = src_t == tok_t
    same_head_t = ((lax.broadcasted_iota(jnp.int32, (LANES, L2), 0) < RWKV_HEAD)
                   == (lax.broadcasted_iota(jnp.int32, (LANES, L2), 1) < L))
    lmask_t = [level_mask(hsz, tok_t, src_t) for hsz in levels]
    second_half = [((crow >> (hsz.bit_length() - 1)) & 1) == 1 for hsz in levels]

    def rows(c):
        return slice(c * L, (c + 1) * L)

    def pick(v, c, j):
        return v[rows(c), j * LANES:(j + 1) * LANES]

    def project(c0, c1):
        w_ref, base = ((win_gla, C_Q) if c1 <= C_SH else
                       (win_rwkv, C_SH) if c1 <= C_GD else (win_gd, C_GD))
        pbuf[:, c0:c1] = _mm(hb, w_ref[:, c0 - base:c1 - base])

    def stack(v, first=head0):
        zero = jnp.zeros((), v.dtype)
        return jnp.concatenate([jnp.where(first, v, zero), jnp.where(first, zero, v)], axis=0)

    def stack_t(v):
        return stack(v, tcol0)

    def segsum(v):
        vs = jnp.concatenate([v[:, i * LANES:(i + 1) * LANES] for i in range(4)], axis=0)
        s = _mm(jnp.concatenate(_split2(vs), axis=1), seg_ref[...])
        return jnp.concatenate([s[i * TT:(i + 1) * TT] for i in range(4)], axis=1)

    def chunk_cumsum(mat_ref, parts, c):
        return _mm(mat_ref[...], jnp.concatenate([p[rows(c)] for p in parts], axis=0))

    def last_row_bcast(v):
        return jnp.concatenate(
            [jnp.broadcast_to(v[(c + 1) * L - 1:(c + 1) * L], (L, v.shape[1])) for c in chunks], 0)

    def sigmoid(v):
        return 0.5 * jnp.tanh(0.5 * v) + 0.5

    def rwkv_stream():
        project(C_SH, C_GD)
        yield "proj"
        cur = pbuf[:, C_SH:C_SH + SHIFT_W]
        if per_seq:
            first = jnp.concatenate([jnp.broadcast_to(ss_out[c], (L, SHIFT_W)) for c in chunks], 0)
            prev = jnp.where((trow & (L - 1)) == 0, first, pltpu.roll(cur, 1, 0))
            for c in chunks:
                ss_out[c] = cur[(c + 1) * L - 1:(c + 1) * L]
        else:
            prev = jnp.where(trow == 0, ss_out[0], pltpu.roll(cur, 1, 0))
            ss_out[0] = cur[TT - 1:TT]
        yield None
        xm = cur + mu_ref[...] * (prev - cur)
        yield None
        r, kr, vr = xm[:, 0:W], xm[:, W:2 * W], xm[:, 2 * W:3 * W]
        wa = xm[:, 3 * W:3 * W + LANES]
        wa = jnp.where(head0, jnp.tanh(wa), wa)
        wa_pre = _mm(_bf(wa), waup[...])
        yield None
        log2_w = (-RWKV_DECAY_SCALE * LOG2E) * sigmoid(w0[...] + wa_pre[:, 0:W])
        yield None
        a = sigmoid(a0[...] + wa_pre[:, W:2 * W])
        yield None
        kk = kr * kk_w[...]
        kr = kr * (1.0 + (a - 1.0) * ka_w[...])
        yield None
        kk = kk * jnp.minimum(lax.rsqrt(segsum(kk * kk)), 1e12)
        yield None
        parts = _split2(log2_w)
        cum = jnp.concatenate([chunk_cumsum(tri_ref, parts, c) for c in chunks], axis=0)
        cum_last = last_row_bcast(cum)
        yield None
        g_inv = jnp.exp2(-cum)
        g_last = jnp.exp2(cum_last)
        beta = a * kk
        yield None
        alpha_t = -kk * jnp.exp2(cum - log2_w)
        yield None
        r_t = r * jnp.exp2(cum)
        yield None
        beta_h, k_h = beta * g_inv, kr * g_inv
        yield None
        beta_r, k_r = beta_h * g_last, k_h * g_last
        yield "elem"

        items = [(c, j) for c in chunks for j in range(N_RWKV_PAIRS)]

        def picked(v):
            return [_bf(pick(v, *cj)) for cj in items]

        def stacked(v):
            return _each(lambda cj: stack(_bf(pick(v, *cj))), items)

        def stacked_t(v):
            def one(cj):
                v2 = pick(v, *cj)
                vt = _bf(jnp.transpose(jnp.concatenate([v2, v2], axis=0)))
                return jnp.where(same_head_t, vt, jnp.zeros((), BF16))

            return _each(one, items)

        rt = picked(r_t)
        al_rt = [jnp.concatenate([p, q], axis=0) for p, q in zip(picked(alpha_t), rt)]
        bh_t = yield from stacked_t(beta_h)
        ab = yield from _each(_mm, al_rt, bh_t)
        a_ab = [v[0:L] for v in ab]
        a_rb = [_bf(jnp.where(lower, v[L:L2], 0.0)) for v in ab]
        eyef = jnp.where(eye, 1.0, 0.0)
        d = yield from _each(lambda v: eyef + jnp.where(lmask[-1], v, 0.0), a_ab)
        kh_t = yield from stacked_t(k_h)
        ak = yield from _each(_mm, al_rt, kh_t)
        a_ak = [_bf(jnp.where(strict_lower, v[0:L], 0.0)) for v in ak]
        a_rk = [_bf(jnp.where(lower, v[L:L2], 0.0)) for v in ak]
        v_s = yield from stacked(vr)
        akv = yield from _each(_mm, a_ak, v_s)
        for li in range(len(levels) - 2, -1, -1):
            msk = lmask[li]
            db = yield from _each(_bf, d)
            dm = yield from _each(
                lambda p, v: _bf(_mm(p, stack_t(_bf(jnp.where(msk, v, 0.0))))), db, a_ab)
            d = yield from _each(lambda v, p, q: v + _mm(p, stack_t(q)), d, dm, db)
        al_s = yield from stacked(alpha_t)
        wu = yield from _each(
            lambda t, p, q: _mm(_bf(t), jnp.concatenate([p, stack(_bf(q))], axis=1)), d, al_s, akv)
        w_t = [_bf(v[:, 0:LANES]) for v in wu]
        u_p = [v[:, LANES:2 * LANES] for v in wu]
        y_kv = yield from _each(_mm, a_rk, v_s)
        kr_s = yield from stacked(k_r)
        s_kv = yield from _each(_tn, kr_s, v_s)
        br_s = yield from stacked(beta_r)
        g_col = yield from _each(
            lambda cj: jnp.transpose(jnp.broadcast_to(pick(g_last, *cj)[0:1], (LANES, LANES))), items)
        yield "mats"

        def rwkv_step(idx, states):
            stb = [_bf(s) for s in states]
            wr = [_mm(jnp.concatenate([w_t[i], rt[i]], axis=0), sb) for i, sb in zip(idx, stb)]
            yield None
            ub = [stack(_bf(v[0:L] + u_p[i])) for i, v in zip(idx, wr)]
            new = [s * g_col[i] + _tn(br_s[i], u) + s_kv[i] for i, s, u in zip(idx, states, ub)]
            yield None
            ys = [v[L:L2] + _mm(a_rb[i], u) + y_kv[i] for i, v, u in zip(idx, wr, ub)]
            yield None
            return ys, new

        y_items = [None] * len(items)
        if per_seq:
            for i0 in range(0, len(items), ITEMS_PER_STEP):
                idx = list(range(i0, i0 + ITEMS_PER_STEP))
                outs, new = yield from rwkv_step(idx, [sr_out[items[i]] for i in idx])
                for i, o, s in zip(idx, outs, new):
                    y_items[i] = o
                    sr_out[items[i]] = s
        else:
            states = [sr_out[0, j] for j in range(N_RWKV_PAIRS)]
            for c in chunks:
                idx = [c * N_RWKV_PAIRS + j for j in range(N_RWKV_PAIRS)]
                outs, states = yield from rwkv_step(idx, states)
                for i, o in zip(idx, outs):
                    y_items[i] = o
            for j in range(N_RWKV_PAIRS):
                sr_out[0, j] = states[j]
        y = jnp.concatenate(
            [jnp.concatenate([y_items[c * N_RWKV_PAIRS + j] for j in range(N_RWKV_PAIRS)], axis=1)
             for c in chunks], axis=0)
        yield None
        mean = segsum(y) * (1.0 / RWKV_HEAD)
        yc = y - mean
        yield None
        var = segsum(yc * yc) * (1.0 / RWKV_HEAD)
        yield None
        y = yc * lax.rsqrt(var + RWKV_GN_EPS) * lnw[...] + lnb[...]
        yield None
        bonus = segsum(r * kr * rk_w[...]) * vr
        yield None
        zr = pbuf[:, C_ZR:C_ZR + W]
        mbuf[:, GLA_WIDTH:GLA_WIDTH + W] = _bf((y + bonus) * (zr * sigmoid(zr)))
        while True:
            yield "end"

    def gla_stream():
        each1 = functools.partial(_each, group=1)
        project(C_GD, N_COLS)
        project(C_Q, C_V)
        yield None
        gl = _mm(_bf(pbuf[:, C_GD:C_GD + LANES]), aup[...]) + abias[...]
        log2_a = ((jnp.minimum(gl, 0.0) - jnp.log1p(jnp.exp(-jnp.abs(gl))))
                  * (LOG2E / GLA_TEMP))
        parts = _split2(log2_a)
        yield None
        project(C_V, C_ZG)
        yield None
        ex = []
        for c in chunks:
            ex.append(jnp.exp2(chunk_cumsum(e_ref, parts, c)))
            yield None
        project(C_ZG, C_SH)
        yield "decays"

        gitems = [(c, j) for c in chunks for j in range(N_GLA_PAIRS)]
        q = pbuf[:, C_Q:C_Q + GLA_KEY] * (GLA_DK ** -0.5)
        k = pbuf[:, C_K:C_K + GLA_KEY]
        qj = [pick(q, c, j) for c, j in gitems]
        kj = [pick(k, c, j) for c, j in gitems]

        def exl(blk, c, j):
            return ex[c][blk * L:(blk + 1) * L, j * LANES:(j + 1) * LANES]

        att = yield from each1(
            lambda p, t: jnp.where(eye_t, _nt(stack(_bf(t)), _bf(p)), 0.0), qj, kj)
        for li in range(len(levels)):
            sec, msk = second_half[li], lmask_t[li]

            def level_term(v, p, t, cj):
                xl = jnp.where(sec, p, t) * exl(2 + li, *cj)
                ql = _bf(jnp.where(sec, xl, 0.0))
                kl = stack(_bf(jnp.where(sec, 0.0, xl)))
                return v + jnp.where(msk, _nt(kl, ql), 0.0)

            att = yield from each1(level_term, att, qj, kj, gitems)
        yield "att"

        def value_rows(cj):
            c, j = cj
            v0 = C_V + 2 * j * GLA_DV
            return _bf(jnp.concatenate(
                [pbuf[rows(c), v0:v0 + GLA_DV], pbuf[rows(c), v0 + GLA_DV:v0 + 2 * GLA_DV]], 0))

        gv_s = yield from each1(value_rows, gitems)
        g_kv = yield from each1(
            lambda t, p, cj: _tn(stack(_bf(p * exl(1, *cj))), t), gv_s, kj, gitems)
        q_e = yield from each1(lambda p, cj: p * exl(0, *cj), qj, gitems)
        decay = yield from each1(
            lambda cj: jnp.transpose(jnp.broadcast_to(exl(0, *cj)[L - 1:L], (LANES, LANES))), gitems)

        g_start = [None] * len(gitems)
        if per_seq:
            for i, cj in enumerate(gitems):
                g_start[i] = sg_out[cj]
                sg_out[cj] = g_start[i] * decay[i] + g_kv[i]
        else:
            for j in range(N_GLA_PAIRS):
                st = sg_out[0, j]
                for c in chunks:
                    i = c * N_GLA_PAIRS + j
                    g_start[i] = st
                    st = st * decay[i] + g_kv[i]
                sg_out[0, j] = st
        yield None
        for i, (c, j) in enumerate(gitems):
            gsb = _bf(g_start[i])
            attb = _bf(att[i])
            for hh in range(2):
                mine_k = head0 if hh == 0 else jnp.logical_not(head0)
                o = (_tn(attb[hh * L:(hh + 1) * L], gv_s[i][hh * L:(hh + 1) * L])
                     + _mm(_bf(jnp.where(mine_k, q_e[i], 0.0)), gsb))
                o = o * lax.rsqrt(jnp.mean(o * o, -1, keepdims=True) + RMS_EPS) * gnorm[...]
                c0 = (2 * j + hh) * GLA_DV
                zg = pbuf[rows(c), C_ZG + c0:C_ZG + c0 + GLA_DV]
                mbuf[rows(c), c0:c0 + GLA_DV] = _bf(o * (zg * sigmoid(zg)))
            yield None
        while True:
            yield "end"

    rw, gl = rwkv_stream(), gla_stream()
    while next(rw) != "proj":
        pass
    _alternate(rw, gl, "elem", "decays", (2, 1))
    _alternate(rw, gl, "mats", "att", (2, 1))
    _alternate(rw, gl, "end", "end", (1, 2))

    o = _mm(mbuf[...], wout[...])
    o = o * lax.rsqrt(jnp.mean(o * o, -1, keepdims=True) + RMS_EPS) * npost[...]
    y_ref[...] = x_ref[...] + o


def _layer_call(layer, state_layer, x2d, sg, sr, ss, params, consts, *, n_groups,
                tiles_per_group, tile_rows, L, per_seq):
    nseq = sg.shape[1] // n_groups

    def row_spec(width):
        return pl.BlockSpec((tile_rows, width), lambda b, t: (b * tiles_per_group + t, 0))

    def state_in_spec(arr):
        nd = arr.ndim
        return pl.BlockSpec((None, nseq) + arr.shape[2:],
                            lambda b, t: (state_layer, b) + (0,) * (nd - 2))

    def state_out_spec(arr):
        nd = arr.ndim - 1
        return pl.BlockSpec((nseq,) + arr.shape[2:], lambda b, t: (b,) + (0,) * (nd - 1))

    def layer_spec(arr):
        return pl.BlockSpec((None,) + arr.shape[1:], lambda b, t: (layer,) + (0,) * (arr.ndim - 1))

    def const_spec(arr):
        return pl.BlockSpec(arr.shape, lambda b, t: (0,) * arr.ndim)

    states = (sg, sr, ss)
    in_specs = ([row_spec(D_MODEL)] + [state_in_spec(s) for s in states]
                + [layer_spec(p) for p in params] + [const_spec(c) for c in consts])
    out_specs = [row_spec(D_MODEL)] + [state_out_spec(s) for s in states]
    out_shape = ([jax.ShapeDtypeStruct(x2d.shape, F32)]
                 + [jax.ShapeDtypeStruct(s.shape[1:], F32) for s in states])
    return pl.pallas_call(
        functools.partial(_layer_kernel, tile_rows=tile_rows, L=L, per_seq=per_seq),
        grid=(n_groups, tiles_per_group),
        in_specs=in_specs,
        out_specs=out_specs,
        out_shape=out_shape,
        scratch_shapes=[pltpu.VMEM((tile_rows, N_COLS), F32), pltpu.VMEM((tile_rows, D_MODEL), BF16)],
        compiler_params=pltpu.CompilerParams(
            dimension_semantics=("arbitrary", "arbitrary"), vmem_limit_bytes=VMEM_LIMIT_BYTES),
    )(x2d, sg, sr, ss, *params, *consts)


def _gla_state_to_kernel(s):
    return s.reshape(s.shape[:-3] + (N_GLA_PAIRS, LANES, GLA_DV))


def _gla_state_from_kernel(s):
    return s.reshape(s.shape[:-3] + (GLA_HEADS, GLA_DK, GLA_DV))


def _rwkv_state_to_kernel(s):
    H = RWKV_HEAD
    lead = s.shape[:-3]
    s = jnp.swapaxes(s, -1, -2).reshape(lead + (N_RWKV_PAIRS, 2, H, 1, H))
    same_head = jnp.eye(2, dtype=s.dtype).reshape(2, 1, 2, 1)
    return (s * same_head).reshape(lead + (N_RWKV_PAIRS, LANES, LANES))


def _rwkv_state_from_kernel(s):
    H = RWKV_HEAD
    lead = s.shape[:-3]
    s = s.reshape(lead + (N_RWKV_PAIRS, 2, H, 2, H))
    same_head = jnp.eye(2, dtype=s.dtype).reshape(2, 1, 2, 1)
    s = jnp.swapaxes((s * same_head).sum(axis=-2), -1, -2)
    return s.reshape(lead + (RWKV_HEADS, H, H))


def _row(p):
    return p.reshape(DEPTH, 1, -1)


def _constants(L):
    seg = np.kron(np.eye(2, dtype=np.float32), np.ones((RWKV_HEAD, RWKV_HEAD), np.float32))
    e_mat = _gla_exponent_matrix(L)
    tri = np.tril(np.ones((L, L), np.float32))
    return (jnp.asarray(np.tile(e_mat, (1, 2)), dtype=BF16),
            jnp.asarray(np.tile(tri, (1, 2)), dtype=BF16),
            jnp.asarray(np.concatenate([seg, seg], 0), dtype=BF16))


def _tiling(seq, dec_seq):
    prompt_tile = 512 if seq % 512 == 0 else CHUNK
    sample_seqs_per_tile = 8
    return prompt_tile, sample_seqs_per_tile


def kernel(x_prompt, x_sample, state_gla, state_rwkv, state_shift, norm_pre, norm_post, w_in,
           gla_a_up, gla_a_bias, gla_norm, rwkv_mu, rwkv_w_up, rwkv_w0, rwkv_a_up, rwkv_a0,
           rwkv_k_k, rwkv_k_a, rwkv_r_k, rwkv_ln_w, rwkv_ln_b, w_out):
    bp, seq, _ = x_prompt.shape
    bs, dec_seq, _ = x_sample.shape

    gd0 = 2 * GLA_KEY + 2 * GLA_WIDTH
    w_in = w_in.astype(BF16)
    w_gla = w_in[:, :, :gd0]
    w_rwkv = w_in[:, :, gd0 + GLA_RANK:]
    w_gd = jnp.pad(w_in[:, :, gd0:gd0 + GLA_RANK], ((0, 0), (0, 0), (0, LANES - GLA_RANK)))
    aup = jnp.pad(gla_a_up, ((0, 0), (0, LANES - GLA_RANK), (0, 0))).astype(BF16)
    zpad = jnp.zeros_like(rwkv_w_up)
    waup = jnp.concatenate([jnp.concatenate([rwkv_w_up, zpad], 1),
                            jnp.concatenate([zpad, rwkv_a_up], 1)], 2).astype(BF16)
    params = (_row(norm_pre), _row(norm_post), w_gla, w_rwkv, w_gd, aup, _row(gla_a_bias),
              _row(gla_norm), _row(rwkv_mu), waup, _row(rwkv_w0), _row(rwkv_a0), _row(rwkv_k_k),
              _row(rwkv_k_a), _row(rwkv_r_k), _row(rwkv_ln_w), _row(rwkv_ln_b), w_out.astype(BF16))

    prompt_tile, sample_seqs_per_tile = _tiling(seq, dec_seq)
    consts_p = _constants(CHUNK)
    consts_s = _constants(dec_seq)

    yp = x_prompt.reshape(bp * seq, D_MODEL)
    ys = x_sample.reshape(bs * dec_seq, D_MODEL)
    zero_states = (jnp.zeros((1, bp, N_GLA_PAIRS, LANES, GLA_DV), F32),
                   jnp.zeros((1, bp, N_RWKV_PAIRS, LANES, LANES), F32),
                   jnp.zeros((1, bp, 1, SHIFT_W), F32))
    sample_states = (_gla_state_to_kernel(state_gla), _rwkv_state_to_kernel(state_rwkv),
                     state_shift[:, :, None, :])
    outs = [[] for _ in range(6)]
    for l in range(DEPTH):
        yp, g1, r1, s1 = _layer_call(
            l, 0, yp, *zero_states, params, consts_p, n_groups=bp,
            tiles_per_group=seq // prompt_tile, tile_rows=prompt_tile, L=CHUNK, per_seq=False)
        ys, g2, r2, s2 = _layer_call(
            l, l, ys, *sample_states, params, consts_s, n_groups=bs // sample_seqs_per_tile,
            tiles_per_group=1, tile_rows=sample_seqs_per_tile * dec_seq, L=dec_seq, per_seq=True)
        for lst, val in zip(outs, (g1, g2, r1, r2, s1, s2)):
            lst.append(val)
    gla_p, gla_s, rwkv_p, rwkv_s, shift_p, shift_s = (jnp.stack(v) for v in outs)
    return (yp.reshape(bp, seq, D_MODEL), ys.reshape(bs, dec_seq, D_MODEL),
            _gla_state_from_kernel(gla_p), _gla_state_from_kernel(gla_s),
            _rwkv_state_from_kernel(rwkv_p), _rwkv_state_from_kernel(rwkv_s),
            shift_p[:, :, 0], shift_s[:, :, 0])
```

```python
import functools
import math

import numpy as np
import jax
import jax.numpy as jnp
from jax import lax
from jax.experimental import pallas as pl
from jax.experimental.pallas import tpu as pltpu

F32 = jnp.float32
BF16 = jnp.bfloat16

D_MODEL = 1024
DEPTH = 4
CHUNK = 64
RMS_EPS = 1e-6
GLA_WIDTH = 512
GLA_HEADS = 4
GLA_DV = 128
GLA_DK = 64
GLA_KEY = 256
GLA_RANK = 16
GLA_TEMP = 16.0
RWKV_WIDTH = 512
RWKV_HEAD = 64
RWKV_HEADS = 8
RWKV_RANK = 64
SHIFT_W = 3 * RWKV_WIDTH + 2 * RWKV_RANK
RWKV_GN_EPS = 64e-5
RWKV_DECAY_SCALE = 0.606531
LOG2E = math.log2(math.e)

LANES = 128
N_GLA_PAIRS = GLA_HEADS // 2
N_RWKV_PAIRS = RWKV_HEADS // 2
RWKV_PAIR_STATE = (N_RWKV_PAIRS, 2 * RWKV_HEAD, RWKV_HEAD)

C_Q = 0
C_K = C_Q + GLA_KEY
C_V = C_K + GLA_KEY
C_ZG = C_V + GLA_WIDTH
C_SH = C_ZG + GLA_WIDTH
C_ZR = C_SH + SHIFT_W
C_GD = C_ZR + RWKV_WIDTH
N_COLS = C_GD + LANES

VMEM_LIMIT_BYTES = 56 * 1024 * 1024
ITEMS_PER_STEP = 4


def _nt(a, b):
    return lax.dot_general(a, b, (((1,), (1,)), ((), ())), preferred_element_type=F32)


def _tn(a, b):
    return lax.dot_general(a, b, (((0,), (0,)), ((), ())), preferred_element_type=F32)


def _mm(a, b):
    return jnp.dot(a, b, preferred_element_type=F32)


def _bf(v):
    return v.astype(BF16)


def _split3(v):
    hi = _bf(v)
    r1 = v - hi.astype(F32)
    mid = _bf(r1)
    return hi, mid, _bf(r1 - mid.astype(F32))


def _split2(v):
    hi = _bf(v)
    return hi, _bf(v - hi.astype(F32))


def _each(fn, *lists, group=ITEMS_PER_STEP):
    out = []
    for i, args in enumerate(zip(*lists)):
        out.append(fn(*args))
        if i % group == group - 1:
            yield None
    return out


def _alternate(ga, gb, stop_a, stop_b, ratio):
    done_a = done_b = False
    while not (done_a and done_b):
        for _ in range(ratio[0]):
            if not done_a:
                done_a = next(ga) == stop_a
        for _ in range(ratio[1]):
            if not done_b:
                done_b = next(gb) == stop_b


def _pair_rows(v, first=None):
    if first is None:
        first = lax.broadcasted_iota(jnp.int32, (1, LANES), 1) < RWKV_HEAD
    zero = jnp.zeros((), v.dtype)
    return jnp.concatenate([jnp.where(first, v, zero), jnp.where(first, zero, v)], axis=0)


def _levels(L):
    out, h = [], L // 2
    while h >= 1:
        out.append(h)
        h //= 2
    return out


def _gla_exponent_matrix(L):
    t = np.arange(L)[:, None]
    i = np.arange(L)[None, :]
    blocks = [i <= t, i > t]
    for h in _levels(L):
        mid = (t // (2 * h)) * (2 * h) + h
        second = t >= mid
        blocks.append(np.where(second, (i >= mid) & (i <= t), (i > t) & (i < mid)))
    return np.concatenate(blocks, 0).astype(np.float32)


def _layer_kernel(x_ref, sg_in, sr_in, ss_in, npre, npost, win_gla, win_rwkv, win_gd, aup, abias,
                  gnorm, mu_ref, waup, w0, a0, kk_w, ka_w, rk_w, lnw, lnb, wout, e_ref, tri_ref, seg_ref,
                  y_ref, sg_out, sr_out, ss_out, pbuf, mbuf, rst, *, tile_rows, L, per_seq,
                  last_tile):
    TT = tile_rows
    nc = TT // L
    L2 = 2 * L
    W = RWKV_WIDTH
    levels = _levels(L)
    chunks = range(nc)

    @pl.when(pl.program_id(1) == 0)
    def _():
        sg_out[...] = sg_in[...]
        ss_out[...] = ss_in[...]
        dup = _bf(jnp.where(
            lax.broadcasted_iota(jnp.int32, (RWKV_HEAD, LANES), 0)
            == (lax.broadcasted_iota(jnp.int32, (RWKV_HEAD, LANES), 1) & (RWKV_HEAD - 1)), 1.0, 0.0))
        same_head = ((lax.broadcasted_iota(jnp.int32, (LANES, LANES), 0) < RWKV_HEAD)
                     == (lax.broadcasted_iota(jnp.int32, (LANES, LANES), 1) < RWKV_HEAD))
        for s in range(rst.shape[0]):
            for j in range(N_RWKV_PAIRS):
                hi, mid, lo = _split3(sr_in[s, j])
                both = _mm(hi, dup) + _mm(mid, dup) + _mm(lo, dup)
                rst[s, j] = jnp.transpose(jnp.where(same_head, both, 0.0))

    x = x_ref[...]
    hb = _bf(x * lax.rsqrt(jnp.mean(x * x, -1, keepdims=True) + RMS_EPS) * npre[...])

    lane = lax.broadcasted_iota(jnp.int32, (1, LANES), 1)
    head0 = lane < RWKV_HEAD
    tlane = lax.broadcasted_iota(jnp.int32, (1, L2), 1)
    tcol0 = tlane < L
    row = lax.broadcasted_iota(jnp.int32, (L, L2), 0)
    col = lax.broadcasted_iota(jnp.int32, (L, L2), 1) & (L - 1)
    eye = row == col
    strict_lower = row > col
    lower = row >= col
    crow = lax.broadcasted_iota(jnp.int32, (L, 1), 0)
    trow = lax.broadcasted_iota(jnp.int32, (TT, 1), 0)

    def level_mask(hsz, t_idx, s_idx):
        sh = hsz.bit_length()
        same = (t_idx >> sh) == (s_idx >> sh)
        return same & (((t_idx >> (sh - 1)) & 1) == 1) & (((s_idx >> (sh - 1)) & 1) == 0)

    lmask = [level_mask(hsz, row, col) for hsz in levels]
    src_t = lax.broadcasted_iota(jnp.int32, (L2, L), 0) & (L - 1)
    tok_t = lax.broadcasted_iota(jnp.int32, (L2, L), 1)
    eye_t = src_t == tok_t
    same_head_t = ((lax.broadcasted_iota(jnp.int32, (LANES, L2), 0) < RWKV_HEAD)
                   == (lax.broadcasted_iota(jnp.int32, (LANES, L2), 1) < L))
    lmask_t = [level_mask(hsz, tok_t, src_t) for hsz in levels]
    second_half = [((crow >> (hsz.bit_length() - 1)) & 1) == 1 for hsz in levels]

    def rows(c):
        return slice(c * L, (c + 1) * L)

    def pick(v, c, j):
        return v[rows(c), j * LANES:(j + 1) * LANES]

    def project(c0, c1):
        w_ref, base = ((win_gla, C_Q) if c1 <= C_SH else
                       (win_rwkv, C_SH) if c1 <= C_GD else (win_gd, C_GD))
        pbuf[:, c0:c1] = _mm(hb, w_ref[:, c0 - base:c1 - base])

    def stack(v):
        return _pair_rows(v, head0)

    def stack_t(v):
        return _pair_rows(v, tcol0)

    def segsum(v):
        vs = jnp.concatenate([v[:, i * LANES:(i + 1) * LANES] for i in range(4)], axis=0)
        s = _mm(jnp.concatenate(_split2(vs), axis=1), seg_ref[...])
        return jnp.concatenate([s[i * TT:(i + 1) * TT] for i in range(4)], axis=1)

    def chunk_cumsum(mat_ref, parts, c):
        return _mm(mat_ref[...], jnp.concatenate([p[rows(c)] for p in parts], axis=0))

    def last_row_bcast(v):
        return jnp.concatenate(
            [jnp.broadcast_to(v[(c + 1) * L - 1:(c + 1) * L], (L, v.shape[1])) for c in chunks], 0)

    def sigmoid(v):
        return 0.5 * jnp.tanh(0.5 * v) + 0.5

    def rwkv_stream():
        project(C_SH, C_GD)
        yield "proj"
        cur = pbuf[:, C_SH:C_SH + SHIFT_W]
        if per_seq:
            first = jnp.concatenate([jnp.broadcast_to(ss_out[c], (L, SHIFT_W)) for c in chunks], 0)
            prev = jnp.where((trow & (L - 1)) == 0, first, pltpu.roll(cur, 1, 0))
            for c in chunks:
                ss_out[c] = cur[(c + 1) * L - 1:(c + 1) * L]
        else:
            prev = jnp.where(trow == 0, ss_out[0], pltpu.roll(cur, 1, 0))
            ss_out[0] = cur[TT - 1:TT]
        yield None
        xm = cur + mu_ref[...] * (prev - cur)
        yield None
        r, kr, vr = xm[:, 0:W], xm[:, W:2 * W], xm[:, 2 * W:3 * W]
        wa = xm[:, 3 * W:3 * W + LANES]
        wa = jnp.where(head0, jnp.tanh(wa), wa)
        wa_pre = _mm(_bf(wa), waup[...])
        yield None
        log2_w = (-RWKV_DECAY_SCALE * LOG2E) * sigmoid(w0[...] + wa_pre[:, 0:W])
        yield None
        a = sigmoid(a0[...] + wa_pre[:, W:2 * W])
        yield None
        kk = kr * kk_w[...]
        kr = kr * (1.0 + (a - 1.0) * ka_w[...])
        yield None
        kk = kk * jnp.minimum(lax.rsqrt(segsum(kk * kk)), 1e12)
        yield None
        parts = _split2(log2_w)
        cum = jnp.concatenate([chunk_cumsum(tri_ref, parts, c) for c in chunks], axis=0)
        cum_last = last_row_bcast(cum)
        yield None
        g_inv = jnp.exp2(-cum)
        g_last = jnp.exp2(cum_last)
        beta = a * kk
        yield None
        alpha_t = -kk * jnp.exp2(cum - log2_w)
        yield None
        r_t = r * jnp.exp2(cum)
        yield None
        beta_h, k_h = beta * g_inv, kr * g_inv
        yield None
        beta_r, k_r = beta_h * g_last, k_h * g_last
        yield "elem"

        items = [(c, j) for c in chunks for j in range(N_RWKV_PAIRS)]

        def picked(v):
            return [_bf(pick(v, *cj)) for cj in items]

        def stacked(v):
            return _each(lambda cj: stack(_bf(pick(v, *cj))), items)

        def stacked_t(v):
            def one(cj):
                v2 = pick(v, *cj)
                vt = _bf(jnp.transpose(jnp.concatenate([v2, v2], axis=0)))
                return jnp.where(same_head_t, vt, jnp.zeros((), BF16))

            return _each(one, items)

        rt = picked(r_t)
        al_rt = [jnp.concatenate([p, q], axis=0) for p, q in zip(picked(alpha_t), rt)]
        bh_t = yield from stacked_t(beta_h)
        ab = yield from _each(_mm, al_rt, bh_t)
        a_ab = [v[0:L] for v in ab]
        a_rb = [_bf(jnp.where(lower, v[L:L2], 0.0)) for v in ab]
        eyef = jnp.where(eye, 1.0, 0.0)
        d = yield from _each(lambda v: eyef + jnp.where(lmask[-1], v, 0.0), a_ab)
        kh_t = yield from stacked_t(k_h)
        ak = yield from _each(_mm, al_rt, kh_t)
        a_ak = [_bf(jnp.where(strict_lower, v[0:L], 0.0)) for v in ak]
        a_rk = [_bf(jnp.where(lower, v[L:L2], 0.0)) for v in ak]
        v_s = yield from stacked(vr)
        akv = yield from _each(_mm, a_ak, v_s)
        for li in range(len(levels) - 2, -1, -1):
            msk = lmask[li]
            db = yield from _each(_bf, d)
            dm = yield from _each(
                lambda p, v: _bf(_mm(p, stack_t(_bf(jnp.where(msk, v, 0.0))))), db, a_ab)
            d = yield from _each(lambda v, p, q: v + _mm(p, stack_t(q)), d, dm, db)
        al_s = yield from stacked(alpha_t)
        wu = yield from _each(
            lambda t, p, q: _mm(_bf(t), jnp.concatenate([p, stack(_bf(q))], axis=1)), d, al_s, akv)
        w_t = [_bf(v[:, 0:LANES]) for v in wu]
        u_p = [v[:, LANES:2 * LANES] for v in wu]
        y_kv = yield from _each(_mm, a_rk, v_s)
        kr_s = yield from stacked(k_r)
        s_kv = yield from _each(_tn, kr_s, v_s)
        br_s = yield from stacked(beta_r)
        g_col = yield from _each(
            lambda cj: jnp.transpose(jnp.broadcast_to(pick(g_last, *cj)[0:1], (LANES, LANES))), items)
        yield "mats"

        def rwkv_step(idx, states):
            stb = [_bf(s) for s in states]
            wr = [_mm(jnp.concatenate([w_t[i], rt[i]], axis=0), sb) for i, sb in zip(idx, stb)]
            yield None
            ub = [stack(_bf(v[0:L] + u_p[i])) for i, v in zip(idx, wr)]
            new = [s * g_col[i] + _tn(br_s[i], u) + s_kv[i] for i, s, u in zip(idx, states, ub)]
            yield None
            ys = [v[L:L2] + _mm(a_rb[i], u) + y_kv[i] for i, v, u in zip(idx, wr, ub)]
            yield None
            return ys, new

        y_items = [None] * len(items)
        if per_seq:
            for i0 in range(0, len(items), ITEMS_PER_STEP):
                idx = list(range(i0, i0 + ITEMS_PER_STEP))
                outs, new = yield from rwkv_step(idx, [rst[items[i]] for i in idx])
                for i, o, s in zip(idx, outs, new):
                    y_items[i] = o
                    rst[items[i]] = s
        else:
            states = [rst[0, j] for j in range(N_RWKV_PAIRS)]
            for c in chunks:
                idx = [c * N_RWKV_PAIRS + j for j in range(N_RWKV_PAIRS)]
                outs, states = yield from rwkv_step(idx, states)
                for i, o in zip(idx, outs):
                    y_items[i] = o
            for j in range(N_RWKV_PAIRS):
                rst[0, j] = states[j]
        y = jnp.concatenate(
            [jnp.concatenate([y_items[c * N_RWKV_PAIRS + j] for j in range(N_RWKV_PAIRS)], axis=1)
             for c in chunks], axis=0)
        yield None
        mean = segsum(y) * (1.0 / RWKV_HEAD)
        yc = y - mean
        yield None
        var = segsum(yc * yc) * (1.0 / RWKV_HEAD)
        yield None
        y = yc * lax.rsqrt(var + RWKV_GN_EPS) * lnw[...] + lnb[...]
        yield None
        bonus = segsum(r * kr * rk_w[...]) * vr
        yield None
        zr = pbuf[:, C_ZR:C_ZR + W]
        mbuf[:, GLA_WIDTH:GLA_WIDTH + W] = _bf((y + bonus) * (zr * sigmoid(zr)))
        while True:
            yield "end"

    def gla_stream():
        each1 = functools.partial(_each, group=1)
        project(C_GD, N_COLS)
        project(C_Q, C_V)
        yield None
        gl = _mm(_bf(pbuf[:, C_GD:C_GD + LANES]), aup[...]) + abias[...]
        log2_a = ((jnp.minimum(gl, 0.0) - jnp.log1p(jnp.exp(-jnp.abs(gl))))
                  * (LOG2E / GLA_TEMP))
        parts = _split2(log2_a)
        yield None
        project(C_V, C_ZG)
        yield None
        ex = []
        for c in chunks:
            ex.append(jnp.exp2(chunk_cumsum(e_ref, parts, c)))
            yield None
        project(C_ZG, C_SH)
        yield "decays"

        gitems = [(c, j) for c in chunks for j in range(N_GLA_PAIRS)]
        q = pbuf[:, C_Q:C_Q + GLA_KEY] * (GLA_DK ** -0.5)
        k = pbuf[:, C_K:C_K + GLA_KEY]
        qj = [pick(q, c, j) for c, j in gitems]
        kj = [pick(k, c, j) for c, j in gitems]

        def exl(blk, c, j):
            return ex[c][blk * L:(blk + 1) * L, j * LANES:(j + 1) * LANES]

        att = yield from each1(
            lambda p, t: jnp.where(eye_t, _nt(stack(_bf(t)), _bf(p)), 0.0), qj, kj)
        for li in range(len(levels)):
            sec, msk = second_half[li], lmask_t[li]

            def level_term(v, p, t, cj):
                xl = jnp.where(sec, p, t) * exl(2 + li, *cj)
                ql = _bf(jnp.where(sec, xl, 0.0))
                kl = stack(_bf(jnp.where(sec, 0.0, xl)))
                return v + jnp.where(msk, _nt(kl, ql), 0.0)

            att = yield from each1(level_term, att, qj, kj, gitems)
        yield "att"

        def value_rows(cj):
            c, j = cj
            v0 = C_V + 2 * j * GLA_DV
            return _bf(jnp.concatenate(
                [pbuf[rows(c), v0:v0 + GLA_DV], pbuf[rows(c), v0 + GLA_DV:v0 + 2 * GLA_DV]], 0))

        gv_s = yield from each1(value_rows, gitems)
        g_kv = yield from each1(
            lambda t, p, cj: _tn(stack(_bf(p * exl(1, *cj))), t), gv_s, kj, gitems)
        q_e = yield from each1(lambda p, cj: p * exl(0, *cj), qj, gitems)
        decay = yield from each1(
            lambda cj: jnp.transpose(jnp.broadcast_to(exl(0, *cj)[L - 1:L], (LANES, LANES))), gitems)

        g_start = [None] * len(gitems)
        if per_seq:
            for i, cj in enumerate(gitems):
                g_start[i] = sg_out[cj]
                sg_out[cj] = g_start[i] * decay[i] + g_kv[i]
        else:
            for j in range(N_GLA_PAIRS):
                st = sg_out[0, j]
                for c in chunks:
                    i = c * N_GLA_PAIRS + j
                    g_start[i] = st
                    st = st * decay[i] + g_kv[i]
                sg_out[0, j] = st
        yield None
        for i, (c, j) in enumerate(gitems):
            gsb = _bf(g_start[i])
            attb = _bf(att[i])
            for hh in range(2):
                mine_k = head0 if hh == 0 else jnp.logical_not(head0)
                o = (_tn(attb[hh * L:(hh + 1) * L], gv_s[i][hh * L:(hh + 1) * L])
                     + _mm(_bf(jnp.where(mine_k, q_e[i], 0.0)), gsb))
                o = o * lax.rsqrt(jnp.mean(o * o, -1, keepdims=True) + RMS_EPS) * gnorm[...]
                c0 = (2 * j + hh) * GLA_DV
                zg = pbuf[rows(c), C_ZG + c0:C_ZG + c0 + GLA_DV]
                mbuf[rows(c), c0:c0 + GLA_DV] = _bf(o * (zg * sigmoid(zg)))
            yield None
        while True:
            yield "end"

    rw, gl = rwkv_stream(), gla_stream()
    while next(rw) != "proj":
        pass
    _alternate(rw, gl, "elem", "decays", (2, 1))
    _alternate(rw, gl, "mats", "att", (2, 1))
    _alternate(rw, gl, "end", "end", (1, 2))

    o = _mm(mbuf[...], wout[...])
    o = o * lax.rsqrt(jnp.mean(o * o, -1, keepdims=True) + RMS_EPS) * npost[...]
    y_ref[...] = x_ref[...] + o

    @pl.when(pl.program_id(1) == last_tile)
    def _():
        r_i = lax.broadcasted_iota(jnp.int32, (LANES, RWKV_HEAD), 0)
        c_i = lax.broadcasted_iota(jnp.int32, (LANES, RWKV_HEAD), 1)
        fold = _bf(jnp.where((r_i & (RWKV_HEAD - 1)) == c_i, 1.0, 0.0))
        for s in range(rst.shape[0]):
            for j in range(N_RWKV_PAIRS):
                hi, mid, lo = _split3(jnp.transpose(rst[s, j]))
                sr_out[s, j] = _mm(hi, fold) + _mm(mid, fold) + _mm(lo, fold)


def _layer_call(layer, state_layer, x2d, sg, sr, ss, params, consts, *, n_groups,
                tiles_per_group, tile_rows, L, per_seq):
    nseq = sg.shape[1] // n_groups

    def row_spec(width):
        return pl.BlockSpec((tile_rows, width), lambda b, t: (b * tiles_per_group + t, 0))

    def state_in_spec(arr):
        nd = arr.ndim
        return pl.BlockSpec((None, nseq) + arr.shape[2:],
                            lambda b, t: (state_layer, b) + (0,) * (nd - 2))

    def state_out_spec(arr):
        nd = arr.ndim - 1
        return pl.BlockSpec((nseq,) + arr.shape[2:], lambda b, t: (b,) + (0,) * (nd - 1))

    def layer_spec(arr):
        return pl.BlockSpec((None,) + arr.shape[1:], lambda b, t: (layer,) + (0,) * (arr.ndim - 1))

    def const_spec(arr):
        return pl.BlockSpec(arr.shape, lambda b, t: (0,) * arr.ndim)

    states = (sg, sr, ss)
    in_specs = ([row_spec(D_MODEL)] + [state_in_spec(s) for s in states]
                + [layer_spec(p) for p in params] + [const_spec(c) for c in consts])
    out_specs = [row_spec(D_MODEL)] + [state_out_spec(s) for s in states]
    out_shape = ([jax.ShapeDtypeStruct(x2d.shape, F32)]
                 + [jax.ShapeDtypeStruct(s.shape[1:], F32) for s in states])
    return pl.pallas_call(
        functools.partial(_layer_kernel, tile_rows=tile_rows, L=L, per_seq=per_seq,
                          last_tile=tiles_per_group - 1),
        grid=(n_groups, tiles_per_group),
        in_specs=in_specs,
        out_specs=out_specs,
        out_shape=out_shape,
        scratch_shapes=[pltpu.VMEM((tile_rows, N_COLS), F32), pltpu.VMEM((tile_rows, D_MODEL), BF16),
                        pltpu.VMEM((nseq, N_RWKV_PAIRS, LANES, LANES), F32)],
        compiler_params=pltpu.CompilerParams(
            dimension_semantics=("arbitrary", "arbitrary"), vmem_limit_bytes=VMEM_LIMIT_BYTES),
    )(x2d, sg, sr, ss, *params, *consts)


def _gla_state_to_kernel(s):
    return s.reshape(s.shape[:-3] + (N_GLA_PAIRS, LANES, GLA_DV))


def _gla_state_from_kernel(s):
    return s.reshape(s.shape[:-3] + (GLA_HEADS, GLA_DK, GLA_DV))


def _row(p):
    return p.reshape(DEPTH, 1, -1)


def _constants(L):
    seg = np.kron(np.eye(2, dtype=np.float32), np.ones((RWKV_HEAD, RWKV_HEAD), np.float32))
    e_mat = _gla_exponent_matrix(L)
    tri = np.tril(np.ones((L, L), np.float32))
    return (jnp.asarray(np.tile(e_mat, (1, 2)), dtype=BF16),
            jnp.asarray(np.tile(tri, (1, 2)), dtype=BF16),
            jnp.asarray(np.concatenate([seg, seg], 0), dtype=BF16))


def _tiling(seq, dec_seq):
    prompt_tile = 512 if seq % 512 == 0 else CHUNK
    sample_seqs_per_tile = 8
    return prompt_tile, sample_seqs_per_tile


def kernel(x_prompt, x_sample, state_gla, state_rwkv, state_shift, norm_pre, norm_post, w_in,
           gla_a_up, gla_a_bias, gla_norm, rwkv_mu, rwkv_w_up, rwkv_w0, rwkv_a_up, rwkv_a0,
           rwkv_k_k, rwkv_k_a, rwkv_r_k, rwkv_ln_w, rwkv_ln_b, w_out):
    bp, seq, _ = x_prompt.shape
    bs, dec_seq, _ = x_sample.shape

    gd0 = 2 * GLA_KEY + 2 * GLA_WIDTH
    w_in = w_in.astype(BF16)
    w_gla = w_in[:, :, :gd0]
    w_rwkv = w_in[:, :, gd0 + GLA_RANK:]
    w_gd = jnp.pad(w_in[:, :, gd0:gd0 + GLA_RANK], ((0, 0), (0, 0), (0, LANES - GLA_RANK)))
    aup = jnp.pad(gla_a_up, ((0, 0), (0, LANES - GLA_RANK), (0, 0))).astype(BF16)
    zpad = jnp.zeros_like(rwkv_w_up)
    waup = jnp.concatenate([jnp.concatenate([rwkv_w_up, zpad], 1),
                            jnp.concatenate([zpad, rwkv_a_up], 1)], 2).astype(BF16)
    params = (_row(norm_pre), _row(norm_post), w_gla, w_rwkv, w_gd, aup, _row(gla_a_bias),
              _row(gla_norm), _row(rwkv_mu), waup, _row(rwkv_w0), _row(rwkv_a0), _row(rwkv_k_k),
              _row(rwkv_k_a), _row(rwkv_r_k), _row(rwkv_ln_w), _row(rwkv_ln_b), w_out.astype(BF16))

    prompt_tile, sample_seqs_per_tile = _tiling(seq, dec_seq)
    consts_p = _constants(CHUNK)
    consts_s = _constants(dec_seq)

    yp = x_prompt.reshape(bp * seq, D_MODEL)
    ys = x_sample.reshape(bs * dec_seq, D_MODEL)
    zero_states = (jnp.zeros((1, bp, N_GLA_PAIRS, LANES, GLA_DV), F32),
                   jnp.zeros((1, bp) + RWKV_PAIR_STATE, F32),
                   jnp.zeros((1, bp, 1, SHIFT_W), F32))
    sample_states = (_gla_state_to_kernel(state_gla),
                     state_rwkv.reshape(state_rwkv.shape[:2] + RWKV_PAIR_STATE),
                     state_shift[:, :, None, :])
    outs = [[] for _ in range(6)]
    for l in range(DEPTH):
        yp, g1, r1, s1 = _layer_call(
            l, 0, yp, *zero_states, params, consts_p, n_groups=bp,
            tiles_per_group=seq // prompt_tile, tile_rows=prompt_tile, L=CHUNK, per_seq=False)
        ys, g2, r2, s2 = _layer_call(
            l, l, ys, *sample_states, params, consts_s, n_groups=bs // sample_seqs_per_tile,
            tiles_per_group=1, tile_rows=sample_seqs_per_tile * dec_seq, L=dec_seq, per_seq=True)
        for lst, val in zip(outs, (g1, g2, r1, r2, s1, s2)):
            lst.append(val)
    gla_p, gla_s, rwkv_p, rwkv_s, shift_p, shift_s = (jnp.stack(v) for v in outs)
    return (yp.reshape(bp, seq, D_MODEL), ys.reshape(bs, dec_seq, D_MODEL),
            _gla_state_from_kernel(gla_p), _gla_state_from_kernel(gla_s),
            rwkv_p.reshape((DEPTH, bp, RWKV_HEADS, RWKV_HEAD, RWKV_HEAD)),
            rwkv_s.reshape((DEPTH, bs, RWKV_HEADS, RWKV_HEAD, RWKV_HEAD)),
            shift_p[:, :, 0], shift_s[:, :, 0])
```

```python
import functools
import math

import numpy as np
import jax
import jax.numpy as jnp
from jax import lax
from jax.experimental import pallas as pl
from jax.experimental.pallas import tpu as pltpu

F32 = jnp.float32
BF16 = jnp.bfloat16

D_MODEL = 1024
DEPTH = 4
CHUNK = 64
RMS_EPS = 1e-6
GLA_WIDTH = 512
GLA_HEADS = 4
GLA_DV = 128
GLA_DK = 64
GLA_KEY = 256
GLA_RANK = 16
GLA_TEMP = 16.0
RWKV_WIDTH = 512
RWKV_HEAD = 64
RWKV_HEADS = 8
RWKV_RANK = 64
SHIFT_W = 3 * RWKV_WIDTH + 2 * RWKV_RANK
RWKV_GN_EPS = 64e-5
RWKV_DECAY_SCALE = 0.606531
LOG2E = math.log2(math.e)

LANES = 128
N_GLA_PAIRS = GLA_HEADS // 2
N_RWKV_PAIRS = RWKV_HEADS // 2
RWKV_PAIR_STATE = (N_RWKV_PAIRS, 2 * RWKV_HEAD, RWKV_HEAD)

C_Q = 0
C_K = C_Q + GLA_KEY
C_V = C_K + GLA_KEY
C_ZG = C_V + GLA_WIDTH
C_SH = C_ZG + GLA_WIDTH
C_ZR = C_SH + SHIFT_W
C_GD = C_ZR + RWKV_WIDTH
N_COLS = C_GD + LANES

VMEM_LIMIT_BYTES = 56 * 1024 * 1024
ITEMS_PER_STEP = 4


def _nt(a, b):
    return lax.dot_general(a, b, (((1,), (1,)), ((), ())), preferred_element_type=F32)


def _tn(a, b):
    return lax.dot_general(a, b, (((0,), (0,)), ((), ())), preferred_element_type=F32)


def _mm(a, b):
    return jnp.dot(a, b, preferred_element_type=F32)


def _bf(v):
    return v.astype(BF16)


def _split3(v):
    hi = _bf(v)
    r1 = v - hi.astype(F32)
    mid = _bf(r1)
    return hi, mid, _bf(r1 - mid.astype(F32))


def _split2(v):
    hi = _bf(v)
    return hi, _bf(v - hi.astype(F32))


def _each(fn, *lists, group=ITEMS_PER_STEP):
    out = []
    for i, args in enumerate(zip(*lists)):
        out.append(fn(*args))
        if i % group == group - 1:
            yield None
    return out


def _alternate(ga, gb, stop_a, stop_b, ratio):
    done_a = done_b = False
    while not (done_a and done_b):
        for _ in range(ratio[0]):
            if not done_a:
                done_a = next(ga) == stop_a
        for _ in range(ratio[1]):
            if not done_b:
                done_b = next(gb) == stop_b


def _pair_rows(v, first=None):
    if first is None:
        first = lax.broadcasted_iota(jnp.int32, (1, LANES), 1) < RWKV_HEAD
    zero = jnp.zeros((), v.dtype)
    return jnp.concatenate([jnp.where(first, v, zero), jnp.where(first, zero, v)], axis=0)


def _levels(L):
    out, h = [], L // 2
    while h >= 1:
        out.append(h)
        h //= 2
    return out


def _gla_exponent_matrix(L):
    t = np.arange(L)[:, None]
    i = np.arange(L)[None, :]
    blocks = [i <= t, i > t]
    for h in _levels(L):
        mid = (t // (2 * h)) * (2 * h) + h
        second = t >= mid
        blocks.append(np.where(second, (i >= mid) & (i <= t), (i > t) & (i < mid)))
    return np.concatenate(blocks, 0).astype(np.float32)


def _layer_kernel(x_ref, sg_in, sr_in, ss_in, npre, npost, win_gla, win_rwkv, win_gd, aup, abias,
                  gnorm, mu_ref, waup, w0, a0, kk_w, ka_w, rk_w, lnw, lnb, wout, e_ref, tri_ref, seg_ref,
                  y_ref, sg_out, sr_out, ss_out, pbuf, mbuf, rst, *, L, last_tile):
    nb, seq_rows, _ = x_ref.shape
    nc = seq_rows // L
    TT = nb * seq_rows
    L2 = 2 * L
    W = RWKV_WIDTH
    levels = _levels(L)
    seqs = range(nb)
    chunks = range(nb * nc)

    @pl.when(pl.program_id(1) == 0)
    def _():
        sg_out[...] = sg_in[...]
        ss_out[...] = ss_in[...]
        dup = _bf(jnp.where(
            lax.broadcasted_iota(jnp.int32, (RWKV_HEAD, LANES), 0)
            == (lax.broadcasted_iota(jnp.int32, (RWKV_HEAD, LANES), 1) & (RWKV_HEAD - 1)), 1.0, 0.0))
        same_head = ((lax.broadcasted_iota(jnp.int32, (LANES, LANES), 0) < RWKV_HEAD)
                     == (lax.broadcasted_iota(jnp.int32, (LANES, LANES), 1) < RWKV_HEAD))
        for s in seqs:
            for j in range(N_RWKV_PAIRS):
                hi, mid, lo = _split3(sr_in[s, j])
                both = _mm(hi, dup) + _mm(mid, dup) + _mm(lo, dup)
                rst[s, j] = jnp.transpose(jnp.where(same_head, both, 0.0))

    x = jnp.concatenate([x_ref[s] for s in seqs], axis=0)
    hb = _bf(x * lax.rsqrt(jnp.mean(x * x, -1, keepdims=True) + RMS_EPS) * npre[...])

    lane = lax.broadcasted_iota(jnp.int32, (1, LANES), 1)
    head0 = lane < RWKV_HEAD
    tlane = lax.broadcasted_iota(jnp.int32, (1, L2), 1)
    tcol0 = tlane < L
    row = lax.broadcasted_iota(jnp.int32, (L, L2), 0)
    col = lax.broadcasted_iota(jnp.int32, (L, L2), 1) & (L - 1)
    eye = row == col
    strict_lower = row > col
    lower = row >= col
    crow = lax.broadcasted_iota(jnp.int32, (L, 1), 0)
    trow = lax.broadcasted_iota(jnp.int32, (TT, 1), 0)

    def level_mask(hsz, t_idx, s_idx):
        sh = hsz.bit_length()
        same = (t_idx >> sh) == (s_idx >> sh)
        return same & (((t_idx >> (sh - 1)) & 1) == 1) & (((s_idx >> (sh - 1)) & 1) == 0)

    lmask = [level_mask(hsz, row, col) for hsz in levels]
    src_t = lax.broadcasted_iota(jnp.int32, (L2, L), 0) & (L - 1)
    tok_t = lax.broadcasted_iota(jnp.int32, (L2, L), 1)
    eye_t = src_t == tok_t
    same_head_t = ((lax.broadcasted_iota(jnp.int32, (LANES, L2), 0) < RWKV_HEAD)
                   == (lax.broadcasted_iota(jnp.int32, (LANES, L2), 1) < L))
    lmask_t = [level_mask(hsz, tok_t, src_t) for hsz in levels]
    second_half = [((crow >> (hsz.bit_length() - 1)) & 1) == 1 for hsz in levels]

    def rows(c):
        return slice(c * L, (c + 1) * L)

    def pick(v, c, j):
        return v[rows(c), j * LANES:(j + 1) * LANES]

    def project(c0, c1):
        w_ref, base = ((win_gla, C_Q) if c1 <= C_SH else
                       (win_rwkv, C_SH) if c1 <= C_GD else (win_gd, C_GD))
        pbuf[:, c0:c1] = _mm(hb, w_ref[:, c0 - base:c1 - base])

    def stack(v):
        return _pair_rows(v, head0)

    def stack_t(v):
        return _pair_rows(v, tcol0)

    def segsum(v):
        vs = jnp.concatenate([v[:, i * LANES:(i + 1) * LANES] for i in range(4)], axis=0)
        s = _mm(jnp.concatenate(_split2(vs), axis=1), seg_ref[...])
        return jnp.concatenate([s[i * TT:(i + 1) * TT] for i in range(4)], axis=1)

    def chunk_cumsum(mat_ref, parts, c):
        return _mm(mat_ref[...], jnp.concatenate([p[rows(c)] for p in parts], axis=0))

    def last_row_bcast(v):
        return jnp.concatenate(
            [jnp.broadcast_to(v[(c + 1) * L - 1:(c + 1) * L], (L, v.shape[1])) for c in chunks], 0)

    def sigmoid(v):
        return 0.5 * jnp.tanh(0.5 * v) + 0.5

    def rwkv_stream():
        project(C_SH, C_GD)
        yield "proj"
        cur = pbuf[:, C_SH:C_SH + SHIFT_W]
        first = jnp.concatenate(
            [jnp.broadcast_to(ss_out[s], (seq_rows, SHIFT_W)) for s in seqs], axis=0)
        prev = jnp.where((trow & (seq_rows - 1)) == 0, first, pltpu.roll(cur, 1, 0))
        for s in seqs:
            ss_out[s] = cur[(s + 1) * seq_rows - 1:(s + 1) * seq_rows]
        yield None
        xm = cur + mu_ref[...] * (prev - cur)
        yield None
        r, kr, vr = xm[:, 0:W], xm[:, W:2 * W], xm[:, 2 * W:3 * W]
        wa = xm[:, 3 * W:3 * W + LANES]
        wa = jnp.where(head0, jnp.tanh(wa), wa)
        wa_pre = _mm(_bf(wa), waup[...])
        yield None
        log2_w = (-RWKV_DECAY_SCALE * LOG2E) * sigmoid(w0[...] + wa_pre[:, 0:W])
        yield None
        a = sigmoid(a0[...] + wa_pre[:, W:2 * W])
        yield None
        kk = kr * kk_w[...]
        kr = kr * (1.0 + (a - 1.0) * ka_w[...])
        yield None
        kk = kk * jnp.minimum(lax.rsqrt(segsum(kk * kk)), 1e12)
        yield None
        parts = _split2(log2_w)
        cum = jnp.concatenate([chunk_cumsum(tri_ref, parts, c) for c in chunks], axis=0)
        cum_last = last_row_bcast(cum)
        yield None
        g_inv = jnp.exp2(-cum)
        g_last = jnp.exp2(cum_last)
        beta = a * kk
        yield None
        alpha_t = -kk * jnp.exp2(cum - log2_w)
        yield None
        r_t = r * jnp.exp2(cum)
        yield None
        beta_h, k_h = beta * g_inv, kr * g_inv
        yield None
        beta_r, k_r = beta_h * g_last, k_h * g_last
        yield "elem"

        items = [(c, j) for c in chunks for j in range(N_RWKV_PAIRS)]

        def picked(v):
            return [_bf(pick(v, *cj)) for cj in items]

        def stacked(v):
            return _each(lambda cj: stack(_bf(pick(v, *cj))), items)

        def stacked_t(v):
            def one(cj):
                v2 = pick(v, *cj)
                vt = _bf(jnp.transpose(jnp.concatenate([v2, v2], axis=0)))
                return jnp.where(same_head_t, vt, jnp.zeros((), BF16))

            return _each(one, items)

        rt = picked(r_t)
        al_rt = [jnp.concatenate([p, q], axis=0) for p, q in zip(picked(alpha_t), rt)]
        bh_t = yield from stacked_t(beta_h)
        ab = yield from _each(_mm, al_rt, bh_t)
        a_ab = [v[0:L] for v in ab]
        a_rb = [_bf(jnp.where(lower, v[L:L2], 0.0)) for v in ab]
        eyef = jnp.where(eye, 1.0, 0.0)
        d = yield from _each(lambda v: eyef + jnp.where(lmask[-1], v, 0.0), a_ab)
        kh_t = yield from stacked_t(k_h)
        ak = yield from _each(_mm, al_rt, kh_t)
        a_ak = [_bf(jnp.where(strict_lower, v[0:L], 0.0)) for v in ak]
        a_rk = [_bf(jnp.where(lower, v[L:L2], 0.0)) for v in ak]
        v_s = yield from stacked(vr)
        akv = yield from _each(_mm, a_ak, v_s)
        for li in range(len(levels) - 2, -1, -1):
            msk = lmask[li]
            db = yield from _each(_bf, d)
            dm = yield from _each(
                lambda p, v: _bf(_mm(p, stack_t(_bf(jnp.where(msk, v, 0.0))))), db, a_ab)
            d = yield from _each(lambda v, p, q: v + _mm(p, stack_t(q)), d, dm, db)
        al_s = yield from stacked(alpha_t)
        wu = yield from _each(
            lambda t, p, q: _mm(_bf(t), jnp.concatenate([p, stack(_bf(q))], axis=1)), d, al_s, akv)
        w_t = [_bf(v[:, 0:LANES]) for v in wu]
        u_p = [v[:, LANES:2 * LANES] for v in wu]
        y_kv = yield from _each(_mm, a_rk, v_s)
        kr_s = yield from stacked(k_r)
        s_kv = yield from _each(_tn, kr_s, v_s)
        br_s = yield from stacked(beta_r)
        g_col = yield from _each(
            lambda cj: jnp.transpose(jnp.broadcast_to(pick(g_last, *cj)[0:1], (LANES, LANES))), items)
        yield "mats"

        def rwkv_step(idx, states):
            stb = [_bf(s) for s in states]
            wr = [_mm(jnp.concatenate([w_t[i], rt[i]], axis=0), sb) for i, sb in zip(idx, stb)]
            yield None
            ub = [stack(_bf(v[0:L] + u_p[i])) for i, v in zip(idx, wr)]
            new = [s * g_col[i] + _tn(br_s[i], u) + s_kv[i] for i, s, u in zip(idx, states, ub)]
            yield None
            ys = [v[L:L2] + _mm(a_rb[i], u) + y_kv[i] for i, v, u in zip(idx, wr, ub)]
            yield None
            return ys, new

        pairs = [(s, j) for s in seqs for j in range(N_RWKV_PAIRS)]
        y_items = [None] * len(items)
        states = [rst[sj] for sj in pairs]
        for c in range(nc):
            idx = [(s * nc + c) * N_RWKV_PAIRS + j for s, j in pairs]
            outs, states = yield from rwkv_step(idx, states)
            for i, o in zip(idx, outs):
                y_items[i] = o
        for sj, st in zip(pairs, states):
            rst[sj] = st
        y = jnp.concatenate(
            [jnp.concatenate([y_items[c * N_RWKV_PAIRS + j] for j in range(N_RWKV_PAIRS)], axis=1)
             for c in chunks], axis=0)
        yield None
        mean = segsum(y) * (1.0 / RWKV_HEAD)
        yc = y - mean
        yield None
        var = segsum(yc * yc) * (1.0 / RWKV_HEAD)
        yield None
        y = yc * lax.rsqrt(var + RWKV_GN_EPS) * lnw[...] + lnb[...]
        yield None
        bonus = segsum(r * kr * rk_w[...]) * vr
        yield None
        zr = pbuf[:, C_ZR:C_ZR + W]
        mbuf[:, GLA_WIDTH:GLA_WIDTH + W] = _bf((y + bonus) * (zr * sigmoid(zr)))
        while True:
            yield "end"

    def gla_stream():
        each1 = functools.partial(_each, group=1)
        project(C_GD, N_COLS)
        project(C_Q, C_V)
        yield None
        gl = _mm(_bf(pbuf[:, C_GD:C_GD + LANES]), aup[...]) + abias[...]
        log2_a = ((jnp.minimum(gl, 0.0) - jnp.log1p(jnp.exp(-jnp.abs(gl))))
                  * (LOG2E / GLA_TEMP))
        parts = _split2(log2_a)
        yield None
        project(C_V, C_ZG)
        yield None
        ex = []
        for c in chunks:
            ex.append(jnp.exp2(chunk_cumsum(e_ref, parts, c)))
            yield None
        project(C_ZG, C_SH)
        yield "decays"

        gitems = [(c, j) for c in chunks for j in range(N_GLA_PAIRS)]
        q = pbuf[:, C_Q:C_Q + GLA_KEY] * (GLA_DK ** -0.5)
        k = pbuf[:, C_K:C_K + GLA_KEY]
        qj = [pick(q, c, j) for c, j in gitems]
        kj = [pick(k, c, j) for c, j in gitems]

        def exl(blk, c, j):
            return ex[c][blk * L:(blk + 1) * L, j * LANES:(j + 1) * LANES]

        att = yield from each1(
            lambda p, t: jnp.where(eye_t, _nt(stack(_bf(t)), _bf(p)), 0.0), qj, kj)
        for li in range(len(levels)):
            sec, msk = second_half[li], lmask_t[li]

            def level_term(v, p, t, cj):
                xl = jnp.where(sec, p, t) * exl(2 + li, *cj)
                ql = _bf(jnp.where(sec, xl, 0.0))
                kl = stack(_bf(jnp.where(sec, 0.0, xl)))
                return v + jnp.where(msk, _nt(kl, ql), 0.0)

            att = yield from each1(level_term, att, qj, kj, gitems)
        yield "att"

        def value_rows(cj):
            c, j = cj
            v0 = C_V + 2 * j * GLA_DV
            return _bf(jnp.concatenate(
                [pbuf[rows(c), v0:v0 + GLA_DV], pbuf[rows(c), v0 + GLA_DV:v0 + 2 * GLA_DV]], 0))

        gv_s = yield from each1(value_rows, gitems)
        g_kv = yield from each1(
            lambda t, p, cj: _tn(stack(_bf(p * exl(1, *cj))), t), gv_s, kj, gitems)
        q_e = yield from each1(lambda p, cj: p * exl(0, *cj), qj, gitems)
        decay = yield from each1(
            lambda cj: jnp.transpose(jnp.broadcast_to(exl(0, *cj)[L - 1:L], (LANES, LANES))), gitems)

        g_start = [None] * len(gitems)
        for s in seqs:
            for j in range(N_GLA_PAIRS):
                st = sg_out[s, j]
                for c in range(nc):
                    i = (s * nc + c) * N_GLA_PAIRS + j
                    g_start[i] = st
                    st = st * decay[i] + g_kv[i]
                sg_out[s, j] = st
        yield None
        for i, (c, j) in enumerate(gitems):
            gsb = _bf(g_start[i])
            attb = _bf(att[i])
            for hh in range(2):
                mine_k = head0 if hh == 0 else jnp.logical_not(head0)
                o = (_tn(attb[hh * L:(hh + 1) * L], gv_s[i][hh * L:(hh + 1) * L])
                     + _mm(_bf(jnp.where(mine_k, q_e[i], 0.0)), gsb))
                o = o * lax.rsqrt(jnp.mean(o * o, -1, keepdims=True) + RMS_EPS) * gnorm[...]
                c0 = (2 * j + hh) * GLA_DV
                zg = pbuf[rows(c), C_ZG + c0:C_ZG + c0 + GLA_DV]
                mbuf[rows(c), c0:c0 + GLA_DV] = _bf(o * (zg * sigmoid(zg)))
            yield None
        while True:
            yield "end"

    rw, gl = rwkv_stream(), gla_stream()
    while next(rw) != "proj":
        pass
    _alternate(rw, gl, "elem", "decays", (2, 1))
    _alternate(rw, gl, "mats", "att", (2, 1))
    _alternate(rw, gl, "end", "end", (1, 2))

    o = _mm(mbuf[...], wout[...])
    o = o * lax.rsqrt(jnp.mean(o * o, -1, keepdims=True) + RMS_EPS) * npost[...]
    y = x + o
    for s in seqs:
        y_ref[s] = y[s * seq_rows:(s + 1) * seq_rows]

    @pl.when(pl.program_id(1) == last_tile)
    def _():
        r_i = lax.broadcasted_iota(jnp.int32, (LANES, RWKV_HEAD), 0)
        c_i = lax.broadcasted_iota(jnp.int32, (LANES, RWKV_HEAD), 1)
        fold = _bf(jnp.where((r_i & (RWKV_HEAD - 1)) == c_i, 1.0, 0.0))
        for s in seqs:
            for j in range(N_RWKV_PAIRS):
                hi, mid, lo = _split3(jnp.transpose(rst[s, j]))
                sr_out[s, j] = _mm(hi, fold) + _mm(mid, fold) + _mm(lo, fold)


def _layer_call(layer, state_layer, x3d, sg, sr, ss, params, consts, *, seqs_per_tile, seq_rows, L):
    n_seq, n_tok, _ = x3d.shape
    tiles_per_group = n_tok // seq_rows
    tile_rows = seqs_per_tile * seq_rows

    x_spec = pl.BlockSpec((seqs_per_tile, seq_rows, D_MODEL), lambda g, t: (g, t, 0))

    def state_in_spec(arr):
        nd = arr.ndim
        return pl.BlockSpec((None, seqs_per_tile) + arr.shape[2:],
                            lambda g, t: (state_layer, g) + (0,) * (nd - 2))

    def state_out_spec(arr):
        nd = arr.ndim - 1
        return pl.BlockSpec((seqs_per_tile,) + arr.shape[2:], lambda g, t: (g,) + (0,) * (nd - 1))

    def layer_spec(arr):
        return pl.BlockSpec((None,) + arr.shape[1:], lambda g, t: (layer,) + (0,) * (arr.ndim - 1))

    def const_spec(arr):
        return pl.BlockSpec(arr.shape, lambda g, t: (0,) * arr.ndim)

    states = (sg, sr, ss)
    in_specs = ([x_spec] + [state_in_spec(s) for s in states]
                + [layer_spec(p) for p in params] + [const_spec(c) for c in consts])
    out_specs = [x_spec] + [state_out_spec(s) for s in states]
    out_shape = ([jax.ShapeDtypeStruct(x3d.shape, F32)]
                 + [jax.ShapeDtypeStruct(s.shape[1:], F32) for s in states])
    return pl.pallas_call(
        functools.partial(_layer_kernel, L=L, last_tile=tiles_per_group - 1),
        grid=(n_seq // seqs_per_tile, tiles_per_group),
        in_specs=in_specs,
        out_specs=out_specs,
        out_shape=out_shape,
        scratch_shapes=[pltpu.VMEM((tile_rows, N_COLS), F32), pltpu.VMEM((tile_rows, D_MODEL), BF16),
                        pltpu.VMEM((seqs_per_tile, N_RWKV_PAIRS, LANES, LANES), F32)],
        compiler_params=pltpu.CompilerParams(
            dimension_semantics=("arbitrary", "arbitrary"), vmem_limit_bytes=VMEM_LIMIT_BYTES),
    )(x3d, sg, sr, ss, *params, *consts)


def _gla_state_to_kernel(s):
    return s.reshape(s.shape[:-3] + (N_GLA_PAIRS, LANES, GLA_DV))


def _gla_state_from_kernel(s):
    return s.reshape(s.shape[:-3] + (GLA_HEADS, GLA_DK, GLA_DV))


def _row(p):
    return p.reshape(DEPTH, 1, -1)


def _constants(L):
    seg = np.kron(np.eye(2, dtype=np.float32), np.ones((RWKV_HEAD, RWKV_HEAD), np.float32))
    e_mat = _gla_exponent_matrix(L)
    tri = np.tril(np.ones((L, L), np.float32))
    return (jnp.asarray(np.tile(e_mat, (1, 2)), dtype=BF16),
            jnp.asarray(np.tile(tri, (1, 2)), dtype=BF16),
            jnp.asarray(np.concatenate([seg, seg], 0), dtype=BF16))


def _tiling(bp, seq, bs):
    prompt_seqs = 2 if bp % 2 == 0 else 1
    prompt_rows = 512 // prompt_seqs if seq % 512 == 0 else CHUNK
    sample_seqs = 8 if bs % 8 == 0 else 1
    return prompt_seqs, prompt_rows, sample_seqs


def kernel(x_prompt, x_sample, state_gla, state_rwkv, state_shift, norm_pre, norm_post, w_in,
           gla_a_up, gla_a_bias, gla_norm, rwkv_mu, rwkv_w_up, rwkv_w0, rwkv_a_up, rwkv_a0,
           rwkv_k_k, rwkv_k_a, rwkv_r_k, rwkv_ln_w, rwkv_ln_b, w_out):
    bp, seq, _ = x_prompt.shape
    bs, dec_seq, _ = x_sample.shape

    gd0 = 2 * GLA_KEY + 2 * GLA_WIDTH
    w_in = w_in.astype(BF16)
    w_gla = w_in[:, :, :gd0]
    w_rwkv = w_in[:, :, gd0 + GLA_RANK:]
    w_gd = jnp.pad(w_in[:, :, gd0:gd0 + GLA_RANK], ((0, 0), (0, 0), (0, LANES - GLA_RANK)))
    aup = jnp.pad(gla_a_up, ((0, 0), (0, LANES - GLA_RANK), (0, 0))).astype(BF16)
    zpad = jnp.zeros_like(rwkv_w_up)
    waup = jnp.concatenate([jnp.concatenate([rwkv_w_up, zpad], 1),
                            jnp.concatenate([zpad, rwkv_a_up], 1)], 2).astype(BF16)
    params = (_row(norm_pre), _row(norm_post), w_gla, w_rwkv, w_gd, aup, _row(gla_a_bias),
              _row(gla_norm), _row(rwkv_mu), waup, _row(rwkv_w0), _row(rwkv_a0), _row(rwkv_k_k),
              _row(rwkv_k_a), _row(rwkv_r_k), _row(rwkv_ln_w), _row(rwkv_ln_b), w_out.astype(BF16))

    prompt_seqs, prompt_rows, sample_seqs = _tiling(bp, seq, bs)
    consts_p = _constants(CHUNK)
    consts_s = _constants(dec_seq)

    yp, ys = x_prompt, x_sample
    zero_states = (jnp.zeros((1, bp, N_GLA_PAIRS, LANES, GLA_DV), F32),
                   jnp.zeros((1, bp) + RWKV_PAIR_STATE, F32),
                   jnp.zeros((1, bp, 1, SHIFT_W), F32))
    sample_states = (_gla_state_to_kernel(state_gla),
                     state_rwkv.reshape(state_rwkv.shape[:2] + RWKV_PAIR_STATE),
                     state_shift[:, :, None, :])
    outs = [[] for _ in range(6)]
    for l in range(DEPTH):
        yp, g1, r1, s1 = _layer_call(
            l, 0, yp, *zero_states, params, consts_p, seqs_per_tile=prompt_seqs,
            seq_rows=prompt_rows, L=CHUNK)
        ys, g2, r2, s2 = _layer_call(
            l, l, ys, *sample_states, params, consts_s, seqs_per_tile=sample_seqs,
            seq_rows=dec_seq, L=dec_seq)
        for lst, val in zip(outs, (g1, g2, r1, r2, s1, s2)):
            lst.append(val)
    gla_p, gla_s, rwkv_p, rwkv_s, shift_p, shift_s = (jnp.stack(v) for v in outs)
    return (yp, ys,
            _gla_state_from_kernel(gla_p), _gla_state_from_kernel(gla_s),
            rwkv_p.reshape((DEPTH, bp, RWKV_HEADS, RWKV_HEAD, RWKV_HEAD)),
            rwkv_s.reshape((DEPTH, bs, RWKV_HEADS, RWKV_HEAD, RWKV_HEAD)),
            shift_p[:, :, 0], shift_s[:, :, 0])
```

```python
import functools
import math

import numpy as np
import jax
import jax.numpy as jnp
from jax import lax
from jax.experimental import pallas as pl
from jax.experimental.pallas import tpu as pltpu

F32 = jnp.float32
BF16 = jnp.bfloat16

D_MODEL = 1024
DEPTH = 4
CHUNK = 64
RMS_EPS = 1e-6
GLA_WIDTH = 512
GLA_HEADS = 4
GLA_DV = 128
GLA_DK = 64
GLA_KEY = 256
GLA_RANK = 16
GLA_TEMP = 16.0
RWKV_WIDTH = 512
RWKV_HEAD = 64
RWKV_HEADS = 8
RWKV_RANK = 64
SHIFT_W = 3 * RWKV_WIDTH + 2 * RWKV_RANK
RWKV_GN_EPS = 64e-5
RWKV_DECAY_SCALE = 0.606531
LOG2E = math.log2(math.e)

LANES = 128
N_GLA_PAIRS = GLA_HEADS // 2
N_RWKV_PAIRS = RWKV_HEADS // 2
RWKV_PAIR_STATE = (N_RWKV_PAIRS, 2 * RWKV_HEAD, RWKV_HEAD)

C_Q = 0
C_K = C_Q + GLA_KEY
C_V = C_K + GLA_KEY
C_ZG = C_V + GLA_WIDTH
C_SH = C_ZG + GLA_WIDTH
C_ZR = C_SH + SHIFT_W
C_GD = C_ZR + RWKV_WIDTH
N_COLS = C_GD + LANES

VMEM_LIMIT_BYTES = 56 * 1024 * 1024
ITEMS_PER_STEP = 4


def _nt(a, b):
    return lax.dot_general(a, b, (((1,), (1,)), ((), ())), preferred_element_type=F32)


def _tn(a, b):
    return lax.dot_general(a, b, (((0,), (0,)), ((), ())), preferred_element_type=F32)


def _mm(a, b):
    return jnp.dot(a, b, preferred_element_type=F32)


def _bf(v):
    return v.astype(BF16)


def _split3(v):
    hi = _bf(v)
    r1 = v - hi.astype(F32)
    mid = _bf(r1)
    return hi, mid, _bf(r1 - mid.astype(F32))


def _split2(v):
    hi = _bf(v)
    return hi, _bf(v - hi.astype(F32))


def _each(fn, *lists, group=ITEMS_PER_STEP):
    out = []
    for i, args in enumerate(zip(*lists)):
        out.append(fn(*args))
        if i % group == group - 1:
            yield None
    return out


def _alternate(ga, gb, stop_a, stop_b, ratio):
    done_a = done_b = False
    while not (done_a and done_b):
        for _ in range(ratio[0]):
            if not done_a:
                done_a = next(ga) == stop_a
        for _ in range(ratio[1]):
            if not done_b:
                done_b = next(gb) == stop_b


def _pair_rows(v, first=None):
    if first is None:
        first = lax.broadcasted_iota(jnp.int32, (1, LANES), 1) < RWKV_HEAD
    zero = jnp.zeros((), v.dtype)
    return jnp.concatenate([jnp.where(first, v, zero), jnp.where(first, zero, v)], axis=0)


def _levels(L):
    out, h = [], L // 2
    while h >= 1:
        out.append(h)
        h //= 2
    return out


def _gla_exponent_matrix(L):
    t = np.arange(L)[:, None]
    i = np.arange(L)[None, :]
    blocks = [i <= t, i > t]
    for h in _levels(L):
        mid = (t // (2 * h)) * (2 * h) + h
        second = t >= mid
        blocks.append(np.where(second, (i >= mid) & (i <= t), (i > t) & (i < mid)))
    return np.concatenate(blocks, 0).astype(np.float32)


def _layer_kernel(x_ref, sg_in, sr_in, ss_in, npre, npost, win_gla, win_rwkv, win_gd, aup, abias,
                  gnorm, mu_ref, waup, w0, a0, kk_w, ka_w, rk_w, lnw, lnb, wout, e_ref, tri_ref, seg_ref,
                  y_ref, sg_out, sr_out, ss_out, pbuf, mbuf, rst, *, L, last_tile):
    nb, seq_rows, _ = x_ref.shape
    nc = seq_rows // L
    TT = nb * seq_rows
    L2 = 2 * L
    W = RWKV_WIDTH
    levels = _levels(L)
    seqs = range(nb)
    chunks = range(nb * nc)

    @pl.when(pl.program_id(1) == 0)
    def _():
        sg_out[...] = sg_in[...]
        ss_out[...] = ss_in[...]
        dup = _bf(jnp.where(
            lax.broadcasted_iota(jnp.int32, (RWKV_HEAD, LANES), 0)
            == (lax.broadcasted_iota(jnp.int32, (RWKV_HEAD, LANES), 1) & (RWKV_HEAD - 1)), 1.0, 0.0))
        same_head = ((lax.broadcasted_iota(jnp.int32, (LANES, LANES), 0) < RWKV_HEAD)
                     == (lax.broadcasted_iota(jnp.int32, (LANES, LANES), 1) < RWKV_HEAD))
        for s in seqs:
            for j in range(N_RWKV_PAIRS):
                hi, mid, lo = _split3(sr_in[s, j])
                both = _mm(hi, dup) + _mm(mid, dup) + _mm(lo, dup)
                rst[s, j] = jnp.transpose(jnp.where(same_head, both, 0.0))

    x = jnp.concatenate([x_ref[s] for s in seqs], axis=0)
    hb = _bf(x * lax.rsqrt(jnp.mean(x * x, -1, keepdims=True) + RMS_EPS) * npre[...])

    lane = lax.broadcasted_iota(jnp.int32, (1, LANES), 1)
    head0 = lane < RWKV_HEAD
    tlane = lax.broadcasted_iota(jnp.int32, (1, L2), 1)
    tcol0 = tlane < L
    row = lax.broadcasted_iota(jnp.int32, (L, L2), 0)
    col = lax.broadcasted_iota(jnp.int32, (L, L2), 1) & (L - 1)
    eye = row == col
    strict_lower = row > col
    lower = row >= col
    crow = lax.broadcasted_iota(jnp.int32, (L, 1), 0)
    trow = lax.broadcasted_iota(jnp.int32, (TT, 1), 0)

    def level_mask(hsz, t_idx, s_idx):
        sh = hsz.bit_length()
        same = (t_idx >> sh) == (s_idx >> sh)
        return same & (((t_idx >> (sh - 1)) & 1) == 1) & (((s_idx >> (sh - 1)) & 1) == 0)

    lmask = [level_mask(hsz, row, col) for hsz in levels]
    src_t = lax.broadcasted_iota(jnp.int32, (L2, L), 0) & (L - 1)
    tok_t = lax.broadcasted_iota(jnp.int32, (L2, L), 1)
    eye_t = src_t == tok_t
    same_head_t = ((lax.broadcasted_iota(jnp.int32, (LANES, L2), 0) < RWKV_HEAD)
                   == (lax.broadcasted_iota(jnp.int32, (LANES, L2), 1) < L))
    lmask_t = [level_mask(hsz, tok_t, src_t) for hsz in levels]
    second_half = [((crow >> (hsz.bit_length() - 1)) & 1) == 1 for hsz in levels]

    def rows(c):
        return slice(c * L, (c + 1) * L)

    def pick(v, c, j):
        return v[rows(c), j * LANES:(j + 1) * LANES]

    def project(c0, c1):
        w_ref, base = ((win_gla, C_Q) if c1 <= C_SH else
                       (win_rwkv, C_SH) if c1 <= C_GD else (win_gd, C_GD))
        pbuf[:, c0:c1] = _mm(hb, w_ref[:, c0 - base:c1 - base])

    def stack(v):
        return _pair_rows(v, head0)

    def stack_t(v):
        return _pair_rows(v, tcol0)

    def segsum(v):
        vs = jnp.concatenate([v[:, i * LANES:(i + 1) * LANES] for i in range(4)], axis=0)
        s = _mm(jnp.concatenate(_split2(vs), axis=1), seg_ref[...])
        return jnp.concatenate([s[i * TT:(i + 1) * TT] for i in range(4)], axis=1)

    def chunk_cumsum(mat_ref, parts, c):
        return _mm(mat_ref[...], jnp.concatenate([p[rows(c)] for p in parts], axis=0))

    def last_row_bcast(v):
        return jnp.concatenate(
            [jnp.broadcast_to(v[(c + 1) * L - 1:(c + 1) * L], (L, v.shape[1])) for c in chunks], 0)

    def sigmoid(v):
        return 0.5 * jnp.tanh(0.5 * v) + 0.5

    def rwkv_stream():
        project(C_SH, C_GD)
        yield "proj"
        cur = pbuf[:, C_SH:C_SH + SHIFT_W]
        first = jnp.concatenate(
            [jnp.broadcast_to(ss_out[s], (seq_rows, SHIFT_W)) for s in seqs], axis=0)
        prev = jnp.where((trow & (seq_rows - 1)) == 0, first, pltpu.roll(cur, 1, 0))
        for s in seqs:
            ss_out[s] = cur[(s + 1) * seq_rows - 1:(s + 1) * seq_rows]
        yield None
        xm = cur + mu_ref[...] * (prev - cur)
        yield None
        r, kr, vr = xm[:, 0:W], xm[:, W:2 * W], xm[:, 2 * W:3 * W]
        wa = xm[:, 3 * W:3 * W + LANES]
        wa = jnp.where(head0, jnp.tanh(wa), wa)
        wa_pre = _mm(_bf(wa), waup[...])
        yield None
        log2_w = (-RWKV_DECAY_SCALE * LOG2E) * sigmoid(w0[...] + wa_pre[:, 0:W])
        yield None
        a = sigmoid(a0[...] + wa_pre[:, W:2 * W])
        yield None
        kk = kr * kk_w[...]
        kr = kr * (1.0 + (a - 1.0) * ka_w[...])
        yield None
        kk = kk * jnp.minimum(lax.rsqrt(segsum(kk * kk)), 1e12)
        yield None
        parts = _split2(log2_w)
        cum = jnp.concatenate([chunk_cumsum(tri_ref, parts, c) for c in chunks], axis=0)
        cum_last = last_row_bcast(cum)
        yield None
        g_inv = jnp.exp2(-cum)
        g_last = jnp.exp2(cum_last)
        beta = a * kk
        yield None
        alpha_t = -kk * jnp.exp2(cum - log2_w)
        yield None
        r_t = r * jnp.exp2(cum)
        yield None
        beta_h, k_h = beta * g_inv, kr * g_inv
        yield None
        beta_r, k_r = beta_h * g_last, k_h * g_last
        yield "elem"

        items = [(c, j) for c in chunks for j in range(N_RWKV_PAIRS)]

        def picked(v):
            return [_bf(pick(v, *cj)) for cj in items]

        def stacked(v):
            return _each(lambda cj: stack(_bf(pick(v, *cj))), items)

        def stacked_t(v):
            def one(cj):
                v2 = pick(v, *cj)
                vt = _bf(jnp.transpose(jnp.concatenate([v2, v2], axis=0)))
                return jnp.where(same_head_t, vt, jnp.zeros((), BF16))

            return _each(one, items)

        rt = picked(r_t)
        al_rt = [jnp.concatenate([p, q], axis=0) for p, q in zip(picked(alpha_t), rt)]
        bh_t = yield from stacked_t(beta_h)
        ab = yield from _each(_mm, al_rt, bh_t)
        a_ab = [v[0:L] for v in ab]
        a_rb = [_bf(jnp.where(lower, v[L:L2], 0.0)) for v in ab]
        eyef = jnp.where(eye, 1.0, 0.0)
        d = yield from _each(lambda v: eyef + jnp.where(lmask[-1], v, 0.0), a_ab)
        kh_t = yield from stacked_t(k_h)
        ak = yield from _each(_mm, al_rt, kh_t)
        a_ak = [_bf(jnp.where(strict_lower, v[0:L], 0.0)) for v in ak]
        a_rk = [_bf(jnp.where(lower, v[L:L2], 0.0)) for v in ak]
        v_s = yield from stacked(vr)
        kr_t = yield from stacked_t(k_r)
        by_v = yield from _each(
            lambda p, q, t, v: _mm(jnp.concatenate([p, q, t], axis=0), v), a_ak, a_rk, kr_t, v_s)
        akv = [v[0:L] for v in by_v]
        y_kv = [v[L:L2] for v in by_v]
        s_kv = [v[L2:L2 + LANES] for v in by_v]
        for li in range(len(levels) - 2, -1, -1):
            msk = lmask[li]
            db = yield from _each(_bf, d)
            dm = yield from _each(
                lambda p, v: _bf(_mm(p, stack_t(_bf(jnp.where(msk, v, 0.0))))), db, a_ab)
            d = yield from _each(lambda v, p, q: v + _mm(p, stack_t(q)), d, dm, db)
        al_s = yield from stacked(alpha_t)
        wu = yield from _each(
            lambda t, p, q: _mm(_bf(t), jnp.concatenate([p, stack(_bf(q))], axis=1)), d, al_s, akv)
        w_t = [_bf(v[:, 0:LANES]) for v in wu]
        u_p = [v[:, LANES:2 * LANES] for v in wu]
        br_t = yield from stacked_t(beta_r)
        by_u = [jnp.concatenate([p, q], axis=0) for p, q in zip(br_t, a_rb)]
        g_col = yield from _each(
            lambda cj: jnp.transpose(jnp.broadcast_to(pick(g_last, *cj)[0:1], (LANES, LANES))), items)
        yield "mats"

        def rwkv_step(idx, states):
            stb = [_bf(s) for s in states]
            wr = [_mm(jnp.concatenate([w_t[i], rt[i]], axis=0), sb) for i, sb in zip(idx, stb)]
            yield None
            ub = [stack(_bf(v[0:L] + u_p[i])) for i, v in zip(idx, wr)]
            bu = [_mm(by_u[i], u) for i, u in zip(idx, ub)]
            new = [s * g_col[i] + v[0:LANES] + s_kv[i] for i, s, v in zip(idx, states, bu)]
            yield None
            ys = [v[L:L2] + t[LANES:LANES + L] + y_kv[i] for i, v, t in zip(idx, wr, bu)]
            yield None
            return ys, new

        pairs = [(s, j) for s in seqs for j in range(N_RWKV_PAIRS)]
        y_items = [None] * len(items)
        states = [rst[sj] for sj in pairs]
        for c in range(nc):
            idx = [(s * nc + c) * N_RWKV_PAIRS + j for s, j in pairs]
            outs, states = yield from rwkv_step(idx, states)
            for i, o in zip(idx, outs):
                y_items[i] = o
        for sj, st in zip(pairs, states):
            rst[sj] = st
        y = jnp.concatenate(
            [jnp.concatenate([y_items[c * N_RWKV_PAIRS + j] for j in range(N_RWKV_PAIRS)], axis=1)
             for c in chunks], axis=0)
        yield None
        mean = segsum(y) * (1.0 / RWKV_HEAD)
        yc = y - mean
        yield None
        var = segsum(yc * yc) * (1.0 / RWKV_HEAD)
        yield None
        y = yc * lax.rsqrt(var + RWKV_GN_EPS) * lnw[...] + lnb[...]
        yield None
        bonus = segsum(r * kr * rk_w[...]) * vr
        yield None
        zr = pbuf[:, C_ZR:C_ZR + W]
        mbuf[:, GLA_WIDTH:GLA_WIDTH + W] = _bf((y + bonus) * (zr * sigmoid(zr)))
        while True:
            yield "end"

    def gla_stream():
        each1 = functools.partial(_each, group=1)
        project(C_GD, N_COLS)
        project(C_Q, C_V)
        yield None
        gl = _mm(_bf(pbuf[:, C_GD:C_GD + LANES]), aup[...]) + abias[...]
        log2_a = ((jnp.minimum(gl, 0.0) - jnp.log1p(jnp.exp(-jnp.abs(gl))))
                  * (LOG2E / GLA_TEMP))
        parts = _split2(log2_a)
        yield None
        project(C_V, C_ZG)
        yield None
        ex = []
        for c in chunks:
            ex.append(jnp.exp2(chunk_cumsum(e_ref, parts, c)))
            yield None
        project(C_ZG, C_SH)
        yield "decays"

        gitems = [(c, j) for c in chunks for j in range(N_GLA_PAIRS)]
        q = pbuf[:, C_Q:C_Q + GLA_KEY] * (GLA_DK ** -0.5)
        k = pbuf[:, C_K:C_K + GLA_KEY]
        qj = [pick(q, c, j) for c, j in gitems]
        kj = [pick(k, c, j) for c, j in gitems]

        def exl(blk, c, j):
            return ex[c][blk * L:(blk + 1) * L, j * LANES:(j + 1) * LANES]

        att = yield from each1(
            lambda p, t: jnp.where(eye_t, _nt(stack(_bf(t)), _bf(p)), 0.0), qj, kj)
        for li in range(len(levels)):
            sec, msk = second_half[li], lmask_t[li]

            def level_term(v, p, t, cj):
                xl = jnp.where(sec, p, t) * exl(2 + li, *cj)
                ql = _bf(jnp.where(sec, xl, 0.0))
                kl = stack(_bf(jnp.where(sec, 0.0, xl)))
                return v + jnp.where(msk, _nt(kl, ql), 0.0)

            att = yield from each1(level_term, att, qj, kj, gitems)
        yield "att"

        def value_rows(cj):
            c, j = cj
            v0 = C_V + 2 * j * GLA_DV
            return _bf(jnp.concatenate(
                [pbuf[rows(c), v0:v0 + GLA_DV], pbuf[rows(c), v0 + GLA_DV:v0 + 2 * GLA_DV]], 0))

        gv_s = yield from each1(value_rows, gitems)
        g_kv = yield from each1(
            lambda t, p, cj: _tn(stack(_bf(p * exl(1, *cj))), t), gv_s, kj, gitems)
        q_e = yield from each1(lambda p, cj: p * exl(0, *cj), qj, gitems)
        decay = yield from each1(
            lambda cj: jnp.transpose(jnp.broadcast_to(exl(0, *cj)[L - 1:L], (LANES, LANES))), gitems)

        g_start = [None] * len(gitems)
        for s in seqs:
            for j in range(N_GLA_PAIRS):
                st = sg_out[s, j]
                for c in range(nc):
                    i = (s * nc + c) * N_GLA_PAIRS + j
                    g_start[i] = st
                    st = st * decay[i] + g_kv[i]
                sg_out[s, j] = st
        yield None
        for i, (c, j) in enumerate(gitems):
            gsb = _bf(g_start[i])
            attb = _bf(att[i])
            for hh in range(2):
                mine_k = head0 if hh == 0 else jnp.logical_not(head0)
                o = (_tn(attb[hh * L:(hh + 1) * L], gv_s[i][hh * L:(hh + 1) * L])
                     + _mm(_bf(jnp.where(mine_k, q_e[i], 0.0)), gsb))
                o = o * lax.rsqrt(jnp.mean(o * o, -1, keepdims=True) + RMS_EPS) * gnorm[...]
                c0 = (2 * j + hh) * GLA_DV
                zg = pbuf[rows(c), C_ZG + c0:C_ZG + c0 + GLA_DV]
                mbuf[rows(c), c0:c0 + GLA_DV] = _bf(o * (zg * sigmoid(zg)))
            yield None
        while True:
            yield "end"

    rw, gl = rwkv_stream(), gla_stream()
    while next(rw) != "proj":
        pass
    _alternate(rw, gl, "elem", "decays", (2, 1))
    _alternate(rw, gl, "mats", "att", (2, 1))
    _alternate(rw, gl, "end", "end", (1, 2))

    o = _mm(mbuf[...], wout[...])
    o = o * lax.rsqrt(jnp.mean(o * o, -1, keepdims=True) + RMS_EPS) * npost[...]
    y = x + o
    for s in seqs:
        y_ref[s] = y[s * seq_rows:(s + 1) * seq_rows]

    @pl.when(pl.program_id(1) == last_tile)
    def _():
        r_i = lax.broadcasted_iota(jnp.int32, (LANES, RWKV_HEAD), 0)
        c_i = lax.broadcasted_iota(jnp.int32, (LANES, RWKV_HEAD), 1)
        fold = _bf(jnp.where((r_i & (RWKV_HEAD - 1)) == c_i, 1.0, 0.0))
        for s in seqs:
            for j in range(N_RWKV_PAIRS):
                hi, mid, lo = _split3(jnp.transpose(rst[s, j]))
                sr_out[s, j] = _mm(hi, fold) + _mm(mid, fold) + _mm(lo, fold)


def _layer_call(layer, state_layer, x3d, sg, sr, ss, params, consts, *, seqs_per_tile, seq_rows, L):
    n_seq, n_tok, _ = x3d.shape
    tiles_per_group = n_tok // seq_rows
    tile_rows = seqs_per_tile * seq_rows

    x_spec = pl.BlockSpec((seqs_per_tile, seq_rows, D_MODEL), lambda g, t: (g, t, 0))

    def state_in_spec(arr):
        nd = arr.ndim
        return pl.BlockSpec((None, seqs_per_tile) + arr.shape[2:],
                            lambda g, t: (state_layer, g) + (0,) * (nd - 2))

    def state_out_spec(arr):
        nd = arr.ndim - 1
        return pl.BlockSpec((seqs_per_tile,) + arr.shape[2:], lambda g, t: (g,) + (0,) * (nd - 1))

    def layer_spec(arr):
        return pl.BlockSpec((None,) + arr.shape[1:], lambda g, t: (layer,) + (0,) * (arr.ndim - 1))

    def const_spec(arr):
        return pl.BlockSpec(arr.shape, lambda g, t: (0,) * arr.ndim)

    states = (sg, sr, ss)
    in_specs = ([x_spec] + [state_in_spec(s) for s in states]
                + [layer_spec(p) for p in params] + [const_spec(c) for c in consts])
    out_specs = [x_spec] + [state_out_spec(s) for s in states]
    out_shape = ([jax.ShapeDtypeStruct(x3d.shape, F32)]
                 + [jax.ShapeDtypeStruct(s.shape[1:], F32) for s in states])
    return pl.pallas_call(
        functools.partial(_layer_kernel, L=L, last_tile=tiles_per_group - 1),
        grid=(n_seq // seqs_per_tile, tiles_per_group),
        in_specs=in_specs,
        out_specs=out_specs,
        out_shape=out_shape,
        scratch_shapes=[pltpu.VMEM((tile_rows, N_COLS), F32), pltpu.VMEM((tile_rows, D_MODEL), BF16),
                        pltpu.VMEM((seqs_per_tile, N_RWKV_PAIRS, LANES, LANES), F32)],
        compiler_params=pltpu.CompilerParams(
            dimension_semantics=("arbitrary", "arbitrary"), vmem_limit_bytes=VMEM_LIMIT_BYTES),
    )(x3d, sg, sr, ss, *params, *consts)


def _gla_state_to_kernel(s):
    return s.reshape(s.shape[:-3] + (N_GLA_PAIRS, LANES, GLA_DV))


def _gla_state_from_kernel(s):
    return s.reshape(s.shape[:-3] + (GLA_HEADS, GLA_DK, GLA_DV))


def _row(p):
    return p.reshape(DEPTH, 1, -1)


def _constants(L):
    seg = np.kron(np.eye(2, dtype=np.float32), np.ones((RWKV_HEAD, RWKV_HEAD), np.float32))
    e_mat = _gla_exponent_matrix(L)
    tri = np.tril(np.ones((L, L), np.float32))
    return (jnp.asarray(np.tile(e_mat, (1, 2)), dtype=BF16),
            jnp.asarray(np.tile(tri, (1, 2)), dtype=BF16),
            jnp.asarray(np.concatenate([seg, seg], 0), dtype=BF16))


def _tiling(bp, seq, bs):
    prompt_seqs = 2 if bp % 2 == 0 else 1
    prompt_rows = 512 // prompt_seqs if seq % 512 == 0 else CHUNK
    sample_seqs = 8 if bs % 8 == 0 else 1
    return prompt_seqs, prompt_rows, sample_seqs


def kernel(x_prompt, x_sample, state_gla, state_rwkv, state_shift, norm_pre, norm_post, w_in,
           gla_a_up, gla_a_bias, gla_norm, rwkv_mu, rwkv_w_up, rwkv_w0, rwkv_a_up, rwkv_a0,
           rwkv_k_k, rwkv_k_a, rwkv_r_k, rwkv_ln_w, rwkv_ln_b, w_out):
    bp, seq, _ = x_prompt.shape
    bs, dec_seq, _ = x_sample.shape

    gd0 = 2 * GLA_KEY + 2 * GLA_WIDTH
    w_in = w_in.astype(BF16)
    w_gla = w_in[:, :, :gd0]
    w_rwkv = w_in[:, :, gd0 + GLA_RANK:]
    w_gd = jnp.pad(w_in[:, :, gd0:gd0 + GLA_RANK], ((0, 0), (0, 0), (0, LANES - GLA_RANK)))
    aup = jnp.pad(gla_a_up, ((0, 0), (0, LANES - GLA_RANK), (0, 0))).astype(BF16)
    zpad = jnp.zeros_like(rwkv_w_up)
    waup = jnp.concatenate([jnp.concatenate([rwkv_w_up, zpad], 1),
                            jnp.concatenate([zpad, rwkv_a_up], 1)], 2).astype(BF16)
    params = (_row(norm_pre), _row(norm_post), w_gla, w_rwkv, w_gd, aup, _row(gla_a_bias),
              _row(gla_norm), _row(rwkv_mu), waup, _row(rwkv_w0), _row(rwkv_a0), _row(rwkv_k_k),
              _row(rwkv_k_a), _row(rwkv_r_k), _row(rwkv_ln_w), _row(rwkv_ln_b), w_out.astype(BF16))

    prompt_seqs, prompt_rows, sample_seqs = _tiling(bp, seq, bs)
    consts_p = _constants(CHUNK)
    consts_s = _constants(dec_seq)

    yp, ys = x_prompt, x_sample
    zero_states = (jnp.zeros((1, bp, N_GLA_PAIRS, LANES, GLA_DV), F32),
                   jnp.zeros((1, bp) + RWKV_PAIR_STATE, F32),
                   jnp.zeros((1, bp, 1, SHIFT_W), F32))
    sample_states = (_gla_state_to_kernel(state_gla),
                     state_rwkv.reshape(state_rwkv.shape[:2] + RWKV_PAIR_STATE),
                     state_shift[:, :, None, :])
    outs = [[] for _ in range(6)]
    for l in range(DEPTH):
        yp, g1, r1, s1 = _layer_call(
            l, 0, yp, *zero_states, params, consts_p, seqs_per_tile=prompt_seqs,
            seq_rows=prompt_rows, L=CHUNK)
        ys, g2, r2, s2 = _layer_call(
            l, l, ys, *sample_states, params, consts_s, seqs_per_tile=sample_seqs,
            seq_rows=dec_seq, L=dec_seq)
        for lst, val in zip(outs, (g1, g2, r1, r2, s1, s2)):
            lst.append(val)
    gla_p, gla_s, rwkv_p, rwkv_s, shift_p, shift_s = (jnp.stack(v) for v in outs)
    return (yp, ys,
            _gla_state_from_kernel(gla_p), _gla_state_from_kernel(gla_s),
            rwkv_p.reshape((DEPTH, bp, RWKV_HEADS, RWKV_HEAD, RWKV_HEAD)),
            rwkv_s.reshape((DEPTH, bs, RWKV_HEADS, RWKV_HEAD, RWKV_HEAD)),
            shift_p[:, :, 0], shift_s[:, :, 0])
```

```python
import functools
import math

import numpy as np
import jax
import jax.numpy as jnp
from jax import lax
from jax.experimental import pallas as pl
from jax.experimental.pallas import tpu as pltpu

F32 = jnp.float32
BF16 = jnp.bfloat16

D_MODEL = 1024
DEPTH = 4
CHUNK = 64
RMS_EPS = 1e-6
GLA_WIDTH = 512
GLA_HEADS = 4
GLA_DV = 128
GLA_DK = 64
GLA_KEY = 256
GLA_RANK = 16
GLA_TEMP = 16.0
RWKV_WIDTH = 512
RWKV_HEAD = 64
RWKV_HEADS = 8
RWKV_RANK = 64
SHIFT_W = 3 * RWKV_WIDTH + 2 * RWKV_RANK
RWKV_GN_EPS = 64e-5
RWKV_DECAY_SCALE = 0.606531
LOG2E = math.log2(math.e)

LANES = 128
N_GLA_PAIRS = GLA_HEADS // 2
N_RWKV_PAIRS = RWKV_HEADS // 2
RWKV_PAIR_STATE = (N_RWKV_PAIRS, 2 * RWKV_HEAD, RWKV_HEAD)

C_Q = 0
C_K = C_Q + GLA_KEY
C_V = C_K + GLA_KEY
C_ZG = C_V + GLA_WIDTH
C_SH = C_ZG + GLA_WIDTH
C_ZR = C_SH + SHIFT_W
C_GD = C_ZR + RWKV_WIDTH
N_COLS = C_GD + LANES

VMEM_LIMIT_BYTES = 56 * 1024 * 1024
ITEMS_PER_STEP = 4


def _nt(a, b):
    return lax.dot_general(a, b, (((1,), (1,)), ((), ())), preferred_element_type=F32)


def _tn(a, b):
    return lax.dot_general(a, b, (((0,), (0,)), ((), ())), preferred_element_type=F32)


def _mm(a, b):
    return jnp.dot(a, b, preferred_element_type=F32)


def _bf(v):
    return v.astype(BF16)


def _split3(v):
    hi = _bf(v)
    r1 = v - hi.astype(F32)
    mid = _bf(r1)
    return hi, mid, _bf(r1 - mid.astype(F32))


def _split2(v):
    hi = _bf(v)
    return hi, _bf(v - hi.astype(F32))


def _each(fn, *lists, group=ITEMS_PER_STEP):
    out = []
    for i, args in enumerate(zip(*lists)):
        out.append(fn(*args))
        if i % group == group - 1:
            yield None
    return out


def _alternate(ga, gb, stop_a, stop_b, ratio):
    done_a = done_b = False
    while not (done_a and done_b):
        for _ in range(ratio[0]):
            if not done_a:
                done_a = next(ga) == stop_a
        for _ in range(ratio[1]):
            if not done_b:
                done_b = next(gb) == stop_b


def _pair_rows(v, first=None):
    if first is None:
        first = lax.broadcasted_iota(jnp.int32, (1, LANES), 1) < RWKV_HEAD
    zero = jnp.zeros((), v.dtype)
    return jnp.concatenate([jnp.where(first, v, zero), jnp.where(first, zero, v)], axis=0)


def _levels(L):
    out, h = [], L // 2
    while h >= 1:
        out.append(h)
        h //= 2
    return out


def _gla_exponent_matrix(L):
    t = np.arange(L)[:, None]
    i = np.arange(L)[None, :]
    blocks = [i <= t, i > t]
    for h in _levels(L):
        mid = (t // (2 * h)) * (2 * h) + h
        second = t >= mid
        blocks.append(np.where(second, (i >= mid) & (i <= t), (i > t) & (i < mid)))
    return np.concatenate(blocks, 0).astype(np.float32)


def _layer_kernel(x_ref, sg_in, sr_in, ss_in, npre, npost, win_gla, win_rwkv, win_gd, aup, abias,
                  gnorm, mu_ref, waup, w0, a0, kk_w, ka_w, rk_w, lnw, lnb, wout, e_ref, tri_ref, seg_ref,
                  y_ref, sg_out, sr_out, ss_out, pbuf, mbuf, rst, *, L, last_tile):
    nb, seq_rows, _ = x_ref.shape
    nc = seq_rows // L
    TT = nb * seq_rows
    L2 = 2 * L
    W = RWKV_WIDTH
    levels = _levels(L)
    seqs = range(nb)
    chunks = range(nb * nc)

    @pl.when(pl.program_id(1) == 0)
    def _():
        sg_out[...] = sg_in[...]
        ss_out[...] = ss_in[...]
        dup = _bf(jnp.where(
            lax.broadcasted_iota(jnp.int32, (RWKV_HEAD, LANES), 0)
            == (lax.broadcasted_iota(jnp.int32, (RWKV_HEAD, LANES), 1) & (RWKV_HEAD - 1)), 1.0, 0.0))
        same_head = ((lax.broadcasted_iota(jnp.int32, (LANES, LANES), 0) < RWKV_HEAD)
                     == (lax.broadcasted_iota(jnp.int32, (LANES, LANES), 1) < RWKV_HEAD))
        for s in seqs:
            for j in range(N_RWKV_PAIRS):
                hi, mid, lo = _split3(sr_in[s, j])
                both = _mm(hi, dup) + _mm(mid, dup) + _mm(lo, dup)
                rst[s, j] = jnp.transpose(jnp.where(same_head, both, 0.0))

    x = jnp.concatenate([x_ref[s] for s in seqs], axis=0)
    hb = _bf(x * lax.rsqrt(jnp.mean(x * x, -1, keepdims=True) + RMS_EPS) * npre[...])

    lane = lax.broadcasted_iota(jnp.int32, (1, LANES), 1)
    head0 = lane < RWKV_HEAD
    tlane = lax.broadcasted_iota(jnp.int32, (1, L2), 1)
    tcol0 = tlane < L
    row = lax.broadcasted_iota(jnp.int32, (L, L2), 0)
    col = lax.broadcasted_iota(jnp.int32, (L, L2), 1) & (L - 1)
    eye = row == col
    strict_lower = row > col
    lower = row >= col
    crow = lax.broadcasted_iota(jnp.int32, (L, 1), 0)
    trow = lax.broadcasted_iota(jnp.int32, (TT, 1), 0)

    def level_mask(hsz, t_idx, s_idx):
        sh = hsz.bit_length()
        same = (t_idx >> sh) == (s_idx >> sh)
        return same & (((t_idx >> (sh - 1)) & 1) == 1) & (((s_idx >> (sh - 1)) & 1) == 0)

    lmask = [level_mask(hsz, row, col) for hsz in levels]
    src_t = lax.broadcasted_iota(jnp.int32, (L2, L), 0) & (L - 1)
    tok_t = lax.broadcasted_iota(jnp.int32, (L2, L), 1)
    eye_t = src_t == tok_t
    same_head_t = ((lax.broadcasted_iota(jnp.int32, (LANES, L2), 0) < RWKV_HEAD)
                   == (lax.broadcasted_iota(jnp.int32, (LANES, L2), 1) < L))
    lmask_t = [level_mask(hsz, tok_t, src_t) for hsz in levels]
    second_half = [((crow >> (hsz.bit_length() - 1)) & 1) == 1 for hsz in levels]

    def rows(c):
        return slice(c * L, (c + 1) * L)

    def pick(v, c, j):
        return v[rows(c), j * LANES:(j + 1) * LANES]

    def project(c0, c1):
        w_ref, base = ((win_gla, C_Q) if c1 <= C_SH else
                       (win_rwkv, C_SH) if c1 <= C_GD else (win_gd, C_GD))
        pbuf[:, c0:c1] = _mm(hb, w_ref[:, c0 - base:c1 - base])

    def stack(v):
        return _pair_rows(v, head0)

    def stack_t(v):
        return _pair_rows(v, tcol0)

    def segsum(v):
        vs = jnp.concatenate([v[:, i * LANES:(i + 1) * LANES] for i in range(4)], axis=0)
        s = _mm(jnp.concatenate(_split2(vs), axis=1), seg_ref[...])
        return jnp.concatenate([s[i * TT:(i + 1) * TT] for i in range(4)], axis=1)

    def chunk_cumsum(mat_ref, parts, c):
        return _mm(mat_ref[...], jnp.concatenate([p[rows(c)] for p in parts], axis=0))

    def last_row_bcast(v):
        return jnp.concatenate(
            [jnp.broadcast_to(v[(c + 1) * L - 1:(c + 1) * L], (L, v.shape[1])) for c in chunks], 0)

    def sigmoid(v):
        return 0.5 * jnp.tanh(0.5 * v) + 0.5

    def rwkv_stream():
        project(C_SH, C_GD)
        yield "proj"
        cur = pbuf[:, C_SH:C_SH + SHIFT_W]
        first = jnp.concatenate(
            [jnp.broadcast_to(ss_out[s], (seq_rows, SHIFT_W)) for s in seqs], axis=0)
        prev = jnp.where((trow & (seq_rows - 1)) == 0, first, pltpu.roll(cur, 1, 0))
        for s in seqs:
            ss_out[s] = cur[(s + 1) * seq_rows - 1:(s + 1) * seq_rows]
        yield None
        xm = cur + mu_ref[...] * (prev - cur)
        yield None
        r, kr, vr = xm[:, 0:W], xm[:, W:2 * W], xm[:, 2 * W:3 * W]
        wa = xm[:, 3 * W:3 * W + LANES]
        wa = jnp.where(head0, jnp.tanh(wa), wa)
        wa_pre = _mm(_bf(wa), waup[...])
        yield None
        log2_w = (-RWKV_DECAY_SCALE * LOG2E) * sigmoid(w0[...] + wa_pre[:, 0:W])
        yield None
        a = sigmoid(a0[...] + wa_pre[:, W:2 * W])
        yield None
        kk = kr * kk_w[...]
        kr = kr * (1.0 + (a - 1.0) * ka_w[...])
        yield None
        kk = kk * jnp.minimum(lax.rsqrt(segsum(kk * kk)), 1e12)
        yield None
        parts = _split2(log2_w)
        cum = jnp.concatenate([chunk_cumsum(tri_ref, parts, c) for c in chunks], axis=0)
        cum_last = last_row_bcast(cum)
        yield None
        g_inv = jnp.exp2(-cum)
        g_last = jnp.exp2(cum_last)
        beta = a * kk
        yield None
        alpha_t = -kk * jnp.exp2(cum - log2_w)
        yield None
        r_t = r * jnp.exp2(cum)
        yield None
        beta_h, k_h = beta * g_inv, kr * g_inv
        yield None
        beta_r, k_r = beta_h * g_last, k_h * g_last
        yield "elem"

        items = [(c, j) for c in chunks for j in range(N_RWKV_PAIRS)]

        def picked(v):
            return [_bf(pick(v, *cj)) for cj in items]

        def stacked(v):
            return _each(lambda cj: stack(_bf(pick(v, *cj))), items)

        def stacked_t(v):
            def one(cj):
                v2 = pick(v, *cj)
                vt = _bf(jnp.transpose(jnp.concatenate([v2, v2], axis=0)))
                return jnp.where(same_head_t, vt, jnp.zeros((), BF16))

            return _each(one, items)

        rt = picked(r_t)
        al_rt = [jnp.concatenate([p, q], axis=0) for p, q in zip(picked(alpha_t), rt)]
        bh_t = yield from stacked_t(beta_h)
        ab = yield from _each(_mm, al_rt, bh_t)
        a_ab = [v[0:L] for v in ab]
        a_rb = [_bf(jnp.where(lower, v[L:L2], 0.0)) for v in ab]
        eyef = jnp.where(eye, 1.0, 0.0)
        d = yield from _each(lambda v: eyef + jnp.where(lmask[-1], v, 0.0), a_ab)
        kh_t = yield from stacked_t(k_h)
        ak = yield from _each(_mm, al_rt, kh_t)
        a_ak = [_bf(jnp.where(strict_lower, v[0:L], 0.0)) for v in ak]
        a_rk = [_bf(jnp.where(lower, v[L:L2], 0.0)) for v in ak]
        v_s = yield from stacked(vr)
        kr_t = yield from stacked_t(k_r)
        by_v = yield from _each(
            lambda p, q, t, v: _mm(jnp.concatenate([p, q, t], axis=0), v), a_ak, a_rk, kr_t, v_s)
        akv = [v[0:L] for v in by_v]
        y_kv = [v[L:L2] for v in by_v]
        s_kv = [v[L2:L2 + LANES] for v in by_v]
        for li in range(len(levels) - 2, -1, -1):
            msk = lmask[li]
            db = yield from _each(_bf, d)
            dm = yield from _each(
                lambda p, v: _bf(_mm(p, stack_t(_bf(jnp.where(msk, v, 0.0))))), db, a_ab)
            d = yield from _each(lambda v, p, q: v + _mm(p, stack_t(q)), d, dm, db)
        al_s = yield from stacked(alpha_t)
        wu = yield from _each(
            lambda t, p, q: _mm(_bf(t), jnp.concatenate([p, stack(_bf(q))], axis=1)), d, al_s, akv)
        w_t = [_bf(v[:, 0:LANES]) for v in wu]
        u_p = [v[:, LANES:2 * LANES] for v in wu]
        br_t = yield from stacked_t(beta_r)
        by_u = [jnp.concatenate([p, q], axis=0) for p, q in zip(br_t, a_rb)]
        g_col = yield from _each(
            lambda cj: jnp.transpose(jnp.broadcast_to(pick(g_last, *cj)[0:1], (LANES, LANES))), items)
        yield "mats"

        def rwkv_step(idx, states):
            stb = [_bf(s) for s in states]
            wr = [_mm(jnp.concatenate([w_t[i], rt[i]], axis=0), sb) for i, sb in zip(idx, stb)]
            yield None
            ub = [stack(_bf(v[0:L] + u_p[i])) for i, v in zip(idx, wr)]
            bu = [_mm(by_u[i], u) for i, u in zip(idx, ub)]
            new = [s * g_col[i] + v[0:LANES] + s_kv[i] for i, s, v in zip(idx, states, bu)]
            yield None
            ys = [v[L:L2] + t[LANES:LANES + L] + y_kv[i] for i, v, t in zip(idx, wr, bu)]
            yield None
            return ys, new

        pairs = [(s, j) for s in seqs for j in range(N_RWKV_PAIRS)]
        y_items = [None] * len(items)
        states = [rst[sj] for sj in pairs]
        for c in range(nc):
            idx = [(s * nc + c) * N_RWKV_PAIRS + j for s, j in pairs]
            outs, states = yield from rwkv_step(idx, states)
            for i, o in zip(idx, outs):
                y_items[i] = o
        for sj, st in zip(pairs, states):
            rst[sj] = st
        y = jnp.concatenate(
            [jnp.concatenate([y_items[c * N_RWKV_PAIRS + j] for j in range(N_RWKV_PAIRS)], axis=1)
             for c in chunks], axis=0)
        yield None
        mean = segsum(y) * (1.0 / RWKV_HEAD)
        yc = y - mean
        yield None
        var = segsum(yc * yc) * (1.0 / RWKV_HEAD)
        yield None
        y = yc * lax.rsqrt(var + RWKV_GN_EPS) * lnw[...] + lnb[...]
        yield None
        bonus = segsum(r * kr * rk_w[...]) * vr
        yield None
        zr = pbuf[:, C_ZR:C_ZR + W]
        mbuf[:, GLA_WIDTH:GLA_WIDTH + W] = _bf((y + bonus) * (zr * sigmoid(zr)))
        while True:
            yield "end"

    def gla_stream():
        each1 = functools.partial(_each, group=1)
        project(C_GD, N_COLS)
        project(C_Q, C_V)
        yield None
        gl = _mm(_bf(pbuf[:, C_GD:C_GD + LANES]), aup[...]) + abias[...]
        log2_a = ((jnp.minimum(gl, 0.0) - jnp.log1p(jnp.exp(-jnp.abs(gl))))
                  * (LOG2E / GLA_TEMP))
        parts = _split2(log2_a)
        yield None
        project(C_V, C_ZG)
        yield None
        ex = []
        for c in chunks:
            ex.append(jnp.exp2(chunk_cumsum(e_ref, parts, c)))
            yield None
        project(C_ZG, C_SH)
        yield "decays"

        gitems = [(c, j) for c in chunks for j in range(N_GLA_PAIRS)]
        q = pbuf[:, C_Q:C_Q + GLA_KEY] * (GLA_DK ** -0.5)
        k = pbuf[:, C_K:C_K + GLA_KEY]
        qj = [pick(q, c, j) for c, j in gitems]
        kj = [pick(k, c, j) for c, j in gitems]

        def exl(blk, c, j):
            return ex[c][blk * L:(blk + 1) * L, j * LANES:(j + 1) * LANES]

        att = yield from each1(
            lambda p, t: jnp.where(eye_t, _nt(stack(_bf(t)), _bf(p)), 0.0), qj, kj)
        for li in range(len(levels)):
            sec, msk = second_half[li], lmask_t[li]

            def level_term(v, p, t, cj):
                xl = jnp.where(sec, p, t) * exl(2 + li, *cj)
                ql = _bf(jnp.where(sec, xl, 0.0))
                kl = stack(_bf(jnp.where(sec, 0.0, xl)))
                return v + jnp.where(msk, _nt(kl, ql), 0.0)

            att = yield from each1(level_term, att, qj, kj, gitems)
        yield "att"

        def value_rows(cj):
            c, j = cj
            v0 = C_V + 2 * j * GLA_DV
            return _bf(jnp.concatenate(
                [pbuf[rows(c), v0:v0 + GLA_DV], pbuf[rows(c), v0 + GLA_DV:v0 + 2 * GLA_DV]], 0))

        gv_s = yield from each1(value_rows, gitems)
        g_kv = yield from each1(
            lambda t, p, cj: _tn(stack(_bf(p * exl(1, *cj))), t), gv_s, kj, gitems)
        q_e = yield from each1(lambda p, cj: p * exl(0, *cj), qj, gitems)
        decay = yield from each1(
            lambda cj: jnp.transpose(jnp.broadcast_to(exl(0, *cj)[L - 1:L], (LANES, LANES))), gitems)

        g_start = [None] * len(gitems)
        for s in seqs:
            for j in range(N_GLA_PAIRS):
                st = sg_out[s, j]
                for c in range(nc):
                    i = (s * nc + c) * N_GLA_PAIRS + j
                    g_start[i] = st
                    st = st * decay[i] + g_kv[i]
                sg_out[s, j] = st
        yield None
        for i, (c, j) in enumerate(gitems):
            gsb = _bf(g_start[i])
            attb = _bf(att[i])
            for hh in range(2):
                mine_k = head0 if hh == 0 else jnp.logical_not(head0)
                o = (_tn(attb[hh * L:(hh + 1) * L], gv_s[i][hh * L:(hh + 1) * L])
                     + _mm(_bf(jnp.where(mine_k, q_e[i], 0.0)), gsb))
                o = o * lax.rsqrt(jnp.mean(o * o, -1, keepdims=True) + RMS_EPS) * gnorm[...]
                c0 = (2 * j + hh) * GLA_DV
                zg = pbuf[rows(c), C_ZG + c0:C_ZG + c0 + GLA_DV]
                mbuf[rows(c), c0:c0 + GLA_DV] = _bf(o * (zg * sigmoid(zg)))
            yield None
        while True:
            yield "end"

    rw, gl = rwkv_stream(), gla_stream()
    while next(rw) != "proj":
        pass
    _alternate(rw, gl, "elem", "decays", (2, 1))
    _alternate(rw, gl, "mats", "att", (2, 1))
    _alternate(rw, gl, "end", "end", (1, 2))

    o = _mm(mbuf[...], wout[...])
    o = o * lax.rsqrt(jnp.mean(o * o, -1, keepdims=True) + RMS_EPS) * npost[...]
    y = x + o
    for s in seqs:
        y_ref[s] = y[s * seq_rows:(s + 1) * seq_rows]

    @pl.when(pl.program_id(1) == last_tile)
    def _():
        r_i = lax.broadcasted_iota(jnp.int32, (LANES, RWKV_HEAD), 0)
        c_i = lax.broadcasted_iota(jnp.int32, (LANES, RWKV_HEAD), 1)
        fold = _bf(jnp.where((r_i & (RWKV_HEAD - 1)) == c_i, 1.0, 0.0))
        for s in seqs:
            for j in range(N_RWKV_PAIRS):
                hi, mid, lo = _split3(jnp.transpose(rst[s, j]))
                sr_out[s, j] = _mm(hi, fold) + _mm(mid, fold) + _mm(lo, fold)


def _layer_call(layer, state_layer, x3d, sg, sr, ss, params, consts, *, seqs_per_tile, seq_rows, L):
    n_seq, n_tok, _ = x3d.shape
    tiles_per_group = n_tok // seq_rows
    tile_rows = seqs_per_tile * seq_rows

    x_spec = pl.BlockSpec((seqs_per_tile, seq_rows, D_MODEL), lambda g, t: (g, t, 0))

    def state_in_spec(arr):
        nd = arr.ndim
        return pl.BlockSpec((None, seqs_per_tile) + arr.shape[2:],
                            lambda g, t: (state_layer, g) + (0,) * (nd - 2))

    def state_out_spec(arr):
        nd = arr.ndim - 1
        return pl.BlockSpec((seqs_per_tile,) + arr.shape[2:], lambda g, t: (g,) + (0,) * (nd - 1))

    def layer_spec(arr):
        return pl.BlockSpec((None,) + arr.shape[1:], lambda g, t: (layer,) + (0,) * (arr.ndim - 1))

    def const_spec(arr):
        return pl.BlockSpec(arr.shape, lambda g, t: (0,) * arr.ndim)

    states = (sg, sr, ss)
    in_specs = ([x_spec] + [state_in_spec(s) for s in states]
                + [layer_spec(p) for p in params] + [const_spec(c) for c in consts])
    out_specs = [x_spec] + [state_out_spec(s) for s in states]
    out_shape = ([jax.ShapeDtypeStruct(x3d.shape, F32)]
                 + [jax.ShapeDtypeStruct(s.shape[1:], F32) for s in states])
    return pl.pallas_call(
        functools.partial(_layer_kernel, L=L, last_tile=tiles_per_group - 1),
        grid=(n_seq // seqs_per_tile, tiles_per_group),
        in_specs=in_specs,
        out_specs=out_specs,
        out_shape=out_shape,
        scratch_shapes=[pltpu.VMEM((tile_rows, N_COLS), F32), pltpu.VMEM((tile_rows, D_MODEL), BF16),
                        pltpu.VMEM((seqs_per_tile, N_RWKV_PAIRS, LANES, LANES), F32)],
        compiler_params=pltpu.CompilerParams(
            dimension_semantics=("arbitrary", "arbitrary"), vmem_limit_bytes=VMEM_LIMIT_BYTES),
    )(x3d, sg, sr, ss, *params, *consts)


W_PREP_ROWS = 128


def _weight_prep_kernel(w_ref, gla_ref, rwkv_ref, gd_ref):
    gd0 = 2 * GLA_KEY + 2 * GLA_WIDTH
    w = w_ref[...]
    gla_ref[...] = _bf(w[:, :gd0])
    rwkv_ref[...] = _bf(w[:, gd0 + GLA_RANK:])
    lane = lax.broadcasted_iota(jnp.int32, (1, LANES), 1)
    gd_ref[...] = _bf(jnp.where(lane < GLA_RANK, w[:, gd0:gd0 + LANES], 0.0))


def _prepare_w_in(w_in):
    depth, d_in, n_cols = w_in.shape
    gd0 = 2 * GLA_KEY + 2 * GLA_WIDTH
    widths = (gd0, n_cols - gd0 - GLA_RANK, LANES)

    def spec(width):
        return pl.BlockSpec((None, W_PREP_ROWS, width), lambda l, i: (l, i, 0))

    return pl.pallas_call(
        _weight_prep_kernel,
        grid=(depth, d_in // W_PREP_ROWS),
        in_specs=[spec(n_cols)],
        out_specs=[spec(w) for w in widths],
        out_shape=[jax.ShapeDtypeStruct((depth, d_in, w), BF16) for w in widths],
    )(w_in)


def _gla_state_to_kernel(s):
    return s.reshape(s.shape[:-3] + (N_GLA_PAIRS, LANES, GLA_DV))


def _gla_state_from_kernel(s):
    return s.reshape(s.shape[:-3] + (GLA_HEADS, GLA_DK, GLA_DV))


def _row(p):
    return p.reshape(DEPTH, 1, -1)


def _constants(L):
    seg = np.kron(np.eye(2, dtype=np.float32), np.ones((RWKV_HEAD, RWKV_HEAD), np.float32))
    e_mat = _gla_exponent_matrix(L)
    tri = np.tril(np.ones((L, L), np.float32))
    return (jnp.asarray(np.tile(e_mat, (1, 2)), dtype=BF16),
            jnp.asarray(np.tile(tri, (1, 2)), dtype=BF16),
            jnp.asarray(np.concatenate([seg, seg], 0), dtype=BF16))


def _tiling(bp, seq, bs):
    prompt_seqs = 2 if bp % 2 == 0 else 1
    prompt_rows = 512 // prompt_seqs if seq % 512 == 0 else CHUNK
    sample_seqs = 8 if bs % 8 == 0 else 1
    return prompt_seqs, prompt_rows, sample_seqs


def kernel(x_prompt, x_sample, state_gla, state_rwkv, state_shift, norm_pre, norm_post, w_in,
           gla_a_up, gla_a_bias, gla_norm, rwkv_mu, rwkv_w_up, rwkv_w0, rwkv_a_up, rwkv_a0,
           rwkv_k_k, rwkv_k_a, rwkv_r_k, rwkv_ln_w, rwkv_ln_b, w_out):
    bp, seq, _ = x_prompt.shape
    bs, dec_seq, _ = x_sample.shape

    w_gla, w_rwkv, w_gd = _prepare_w_in(w_in)
    aup = jnp.pad(gla_a_up, ((0, 0), (0, LANES - GLA_RANK), (0, 0))).astype(BF16)
    zpad = jnp.zeros_like(rwkv_w_up)
    waup = jnp.concatenate([jnp.concatenate([rwkv_w_up, zpad], 1),
                            jnp.concatenate([zpad, rwkv_a_up], 1)], 2).astype(BF16)
    params = (_row(norm_pre), _row(norm_post), w_gla, w_rwkv, w_gd, aup, _row(gla_a_bias),
              _row(gla_norm), _row(rwkv_mu), waup, _row(rwkv_w0), _row(rwkv_a0), _row(rwkv_k_k),
              _row(rwkv_k_a), _row(rwkv_r_k), _row(rwkv_ln_w), _row(rwkv_ln_b), w_out.astype(BF16))

    prompt_seqs, prompt_rows, sample_seqs = _tiling(bp, seq, bs)
    consts_p = _constants(CHUNK)
    consts_s = _constants(dec_seq)

    yp, ys = x_prompt, x_sample
    zero_states = (jnp.zeros((1, bp, N_GLA_PAIRS, LANES, GLA_DV), F32),
                   jnp.zeros((1, bp) + RWKV_PAIR_STATE, F32),
                   jnp.zeros((1, bp, 1, SHIFT_W), F32))
    sample_states = (_gla_state_to_kernel(state_gla),
                     state_rwkv.reshape(state_rwkv.shape[:2] + RWKV_PAIR_STATE),
                     state_shift[:, :, None, :])
    outs = [[] for _ in range(6)]
    for l in range(DEPTH):
        yp, g1, r1, s1 = _layer_call(
            l, 0, yp, *zero_states, params, consts_p, seqs_per_tile=prompt_seqs,
            seq_rows=prompt_rows, L=CHUNK)
        ys, g2, r2, s2 = _layer_call(
            l, l, ys, *sample_states, params, consts_s, seqs_per_tile=sample_seqs,
            seq_rows=dec_seq, L=dec_seq)
        for lst, val in zip(outs, (g1, g2, r1, r2, s1, s2)):
            lst.append(val)
    gla_p, gla_s, rwkv_p, rwkv_s, shift_p, shift_s = (jnp.stack(v) for v in outs)
    return (yp, ys,
            _gla_state_from_kernel(gla_p), _gla_state_from_kernel(gla_s),
            rwkv_p.reshape((DEPTH, bp, RWKV_HEADS, RWKV_HEAD, RWKV_HEAD)),
            rwkv_s.reshape((DEPTH, bs, RWKV_HEADS, RWKV_HEAD, RWKV_HEAD)),
            shift_p[:, :, 0], shift_s[:, :, 0])
```

```python
import functools
import math

import numpy as np
import jax
import jax.numpy as jnp
from jax import lax
from jax.experimental import pallas as pl
from jax.experimental.pallas import tpu as pltpu

F32 = jnp.float32
BF16 = jnp.bfloat16

D_MODEL = 1024
DEPTH = 4
CHUNK = 64
RMS_EPS = 1e-6
GLA_WIDTH = 512
GLA_HEADS = 4
GLA_DV = 128
GLA_DK = 64
GLA_KEY = 256
GLA_RANK = 16
GLA_TEMP = 16.0
RWKV_WIDTH = 512
RWKV_HEAD = 64
RWKV_HEADS = 8
RWKV_RANK = 64
SHIFT_W = 3 * RWKV_WIDTH + 2 * RWKV_RANK
RWKV_GN_EPS = 64e-5
RWKV_DECAY_SCALE = 0.606531
LOG2E = math.log2(math.e)

LANES = 128
N_GLA_PAIRS = GLA_HEADS // 2
N_RWKV_PAIRS = RWKV_HEADS // 2
RWKV_PAIR_STATE = (N_RWKV_PAIRS, 2 * RWKV_HEAD, RWKV_HEAD)

C_Q = 0
C_K = C_Q + GLA_KEY
C_V = C_K + GLA_KEY
C_ZG = C_V + GLA_WIDTH
C_SH = C_ZG + GLA_WIDTH
C_ZR = C_SH + SHIFT_W
C_GD = C_ZR + RWKV_WIDTH
N_COLS = C_GD + LANES

VMEM_LIMIT_BYTES = 56 * 1024 * 1024
PROMPT_STEP_ROWS = 512
SAMPLE_SEQS_PER_STEP = 8
ITEMS_PER_STEP = 4
STAGE_STEPS = ((2, 1), (2, 1), (1, 2))
KK_NORM_FLOOR = 1e-12
HALF_LOG2_DECAY = -0.5 * RWKV_DECAY_SCALE * LOG2E


def _nt(a, b):
    return lax.dot_general(a, b, (((1,), (1,)), ((), ())), preferred_element_type=F32)


def _tn(a, b):
    return lax.dot_general(a, b, (((0,), (0,)), ((), ())), preferred_element_type=F32)


def _mm(a, b):
    return jnp.dot(a, b, preferred_element_type=F32)


def _bf(v):
    return v.astype(BF16)


def _split3(v):
    hi = _bf(v)
    r1 = v - hi.astype(F32)
    mid = _bf(r1)
    return hi, mid, _bf(r1 - mid.astype(F32))


def _split2(v):
    hi = _bf(v)
    return hi, _bf(v - hi.astype(F32))


def _each(fn, *lists, group=ITEMS_PER_STEP):
    out = []
    for i, args in enumerate(zip(*lists)):
        out.append(fn(*args))
        if i % group == group - 1:
            yield None
    return out


def _alternate(ga, gb, stop_a, stop_b, ratio):
    done_a = done_b = False
    while not (done_a and done_b):
        for _ in range(ratio[0]):
            if not done_a:
                done_a = next(ga) == stop_a
        for _ in range(ratio[1]):
            if not done_b:
                done_b = next(gb) == stop_b


def _pair_rows(v, first=None):
    if first is None:
        first = lax.broadcasted_iota(jnp.int32, (1, LANES), 1) < RWKV_HEAD
    zero = jnp.zeros((), v.dtype)
    return jnp.concatenate([jnp.where(first, v, zero), jnp.where(first, zero, v)], axis=0)


def _levels(L):
    out, h = [], L // 2
    while h >= 1:
        out.append(h)
        h //= 2
    return out


def _gla_exponent_matrix(L):
    t = np.arange(L)[:, None]
    i = np.arange(L)[None, :]
    blocks = [i <= t, i > t]
    for h in _levels(L):
        mid = (t // (2 * h)) * (2 * h) + h
        second = t >= mid
        blocks.append(np.where(second, (i >= mid) & (i <= t), (i > t) & (i < mid)))
    return np.concatenate(blocks, 0).astype(np.float32)


def _layer_kernel(x_ref, sg_in, sr_in, ss_in, npre, npost, win_gla, win_rwkv, win_gd, aup, abias,
                  gnorm, mu_ref, waup, w0, a0, kk_w, ka_w, rk_w, lnw, lnb, wout, e_ref, tri_ref, seg_ref,
                  y_ref, sg_out, sr_out, ss_out, pbuf, mbuf, rst, *, L, last_tile):
    nb, seq_rows, _ = x_ref.shape
    nc = seq_rows // L
    TT = nb * seq_rows
    L2 = 2 * L
    W = RWKV_WIDTH
    levels = _levels(L)
    seqs = range(nb)
    chunks = range(nb * nc)

    @pl.when(pl.program_id(1) == 0)
    def _():
        sg_out[...] = sg_in[...]
        ss_out[...] = ss_in[...]
        dup = _bf(jnp.where(
            lax.broadcasted_iota(jnp.int32, (RWKV_HEAD, LANES), 0)
            == (lax.broadcasted_iota(jnp.int32, (RWKV_HEAD, LANES), 1) & (RWKV_HEAD - 1)), 1.0, 0.0))
        same_head = ((lax.broadcasted_iota(jnp.int32, (LANES, LANES), 0) < RWKV_HEAD)
                     == (lax.broadcasted_iota(jnp.int32, (LANES, LANES), 1) < RWKV_HEAD))
        for s in seqs:
            for j in range(N_RWKV_PAIRS):
                hi, mid, lo = _split3(sr_in[s, j])
                both = _mm(hi, dup) + _mm(mid, dup) + _mm(lo, dup)
                rst[s, j] = jnp.transpose(jnp.where(same_head, both, 0.0))

    x = jnp.concatenate([x_ref[s] for s in seqs], axis=0)
    hb = _bf(x * lax.rsqrt(jnp.mean(x * x, -1, keepdims=True) + RMS_EPS) * npre[...])

    lane = lax.broadcasted_iota(jnp.int32, (1, LANES), 1)
    head0 = lane < RWKV_HEAD
    tlane = lax.broadcasted_iota(jnp.int32, (1, L2), 1)
    tcol0 = tlane < L
    row = lax.broadcasted_iota(jnp.int32, (L, L2), 0)
    col = lax.broadcasted_iota(jnp.int32, (L, L2), 1) & (L - 1)
    eye = row == col
    strict_lower = row > col
    lower = row >= col
    crow = lax.broadcasted_iota(jnp.int32, (L, 1), 0)
    trow = lax.broadcasted_iota(jnp.int32, (TT, 1), 0)

    def level_mask(hsz, t_idx, s_idx):
        sh = hsz.bit_length()
        same = (t_idx >> sh) == (s_idx >> sh)
        return same & (((t_idx >> (sh - 1)) & 1) == 1) & (((s_idx >> (sh - 1)) & 1) == 0)

    lmask = [level_mask(hsz, row, col) for hsz in levels]
    src_t = lax.broadcasted_iota(jnp.int32, (L2, L), 0) & (L - 1)
    tok_t = lax.broadcasted_iota(jnp.int32, (L2, L), 1)
    eye_t = src_t == tok_t
    same_head_t = ((lax.broadcasted_iota(jnp.int32, (LANES, L2), 0) < RWKV_HEAD)
                   == (lax.broadcasted_iota(jnp.int32, (LANES, L2), 1) < L))
    lmask_t = [level_mask(hsz, tok_t, src_t) for hsz in levels]
    second_half = [((crow >> (hsz.bit_length() - 1)) & 1) == 1 for hsz in levels]

    def rows(c):
        return slice(c * L, (c + 1) * L)

    def pick(v, c, j):
        return v[rows(c), j * LANES:(j + 1) * LANES]

    def project(c0, c1):
        w_ref, base = ((win_gla, C_Q) if c1 <= C_SH else
                       (win_rwkv, C_SH) if c1 <= C_GD else (win_gd, C_GD))
        pbuf[:, c0:c1] = _mm(hb, w_ref[:, c0 - base:c1 - base])

    def stack(v):
        return _pair_rows(v, head0)

    def stack_t(v):
        return _pair_rows(v, tcol0)

    def segsum(v):
        vs = jnp.concatenate([v[:, i * LANES:(i + 1) * LANES] for i in range(4)], axis=0)
        s = _mm(jnp.concatenate(_split2(vs), axis=1), seg_ref[...])
        return jnp.concatenate([s[i * TT:(i + 1) * TT] for i in range(4)], axis=1)

    def chunk_cumsum(mat_ref, parts, c):
        return _mm(mat_ref[...], jnp.concatenate([p[rows(c)] for p in parts], axis=0))

    def last_row_bcast(v):
        return jnp.concatenate(
            [jnp.broadcast_to(v[(c + 1) * L - 1:(c + 1) * L], (L, v.shape[1])) for c in chunks], 0)

    def sigmoid_of_double(half_v):
        return 0.5 * jnp.tanh(half_v) + 0.5

    def silu(v):
        hv = 0.5 * v
        return hv * (jnp.tanh(hv) + 1.0)

    def rwkv_stream():
        project(C_SH, C_GD)
        yield "proj"
        cur = pbuf[:, C_SH:C_SH + SHIFT_W]
        first = jnp.concatenate(
            [jnp.broadcast_to(ss_out[s], (seq_rows, SHIFT_W)) for s in seqs], axis=0)
        prev = jnp.where((trow & (seq_rows - 1)) == 0, first, pltpu.roll(cur, 1, 0))
        for s in seqs:
            ss_out[s] = cur[(s + 1) * seq_rows - 1:(s + 1) * seq_rows]
        yield None
        xm = cur + mu_ref[...] * (prev - cur)
        yield None
        r, kr, vr = xm[:, 0:W], xm[:, W:2 * W], xm[:, 2 * W:3 * W]
        wa = xm[:, 3 * W:3 * W + LANES]
        wa = jnp.where(head0, jnp.tanh(wa), wa)
        wa_pre = _mm(_bf(wa), waup[...])
        yield None
        log2_w = HALF_LOG2_DECAY * jnp.tanh(w0[...] + wa_pre[:, 0:W]) + HALF_LOG2_DECAY
        yield None
        a = sigmoid_of_double(a0[...] + wa_pre[:, W:2 * W])
        yield None
        kk = kr * kk_w[...]
        kr = kr * (1.0 + (a - 1.0) * ka_w[...])
        yield None
        kk = kk * jnp.minimum(lax.rsqrt(segsum(kk * kk)), 1.0 / KK_NORM_FLOOR)
        yield None
        parts = _split2(log2_w)
        cum = jnp.concatenate([chunk_cumsum(tri_ref, parts, c) for c in chunks], axis=0)
        cum_last = last_row_bcast(cum)
        yield None
        g_inv = jnp.exp2(-cum)
        g_last = jnp.exp2(cum_last)
        beta = a * kk
        yield None
        alpha_t = -kk * jnp.exp2(cum - log2_w)
        yield None
        r_t = r * jnp.exp2(cum)
        yield None
        beta_h, k_h = beta * g_inv, kr * g_inv
        yield None
        beta_r, k_r = beta_h * g_last, k_h * g_last
        yield "elem"

        items = [(c, j) for c in chunks for j in range(N_RWKV_PAIRS)]

        def picked(v):
            return [_bf(pick(v, *cj)) for cj in items]

        def stacked(v):
            return _each(lambda cj: stack(_bf(pick(v, *cj))), items)

        def stacked_t(v):
            def one(cj):
                v2 = pick(v, *cj)
                vt = _bf(jnp.transpose(jnp.concatenate([v2, v2], axis=0)))
                return jnp.where(same_head_t, vt, jnp.zeros((), BF16))

            return _each(one, items)

        rt = picked(r_t)
        al_rt = [jnp.concatenate([p, q], axis=0) for p, q in zip(picked(alpha_t), rt)]
        bh_t = yield from stacked_t(beta_h)
        ab = yield from _each(_mm, al_rt, bh_t)
        a_ab = [v[0:L] for v in ab]
        a_rb = [_bf(jnp.where(lower, v[L:L2], 0.0)) for v in ab]
        eyef = jnp.where(eye, 1.0, 0.0)
        d = yield from _each(lambda v: eyef + jnp.where(lmask[-1], v, 0.0), a_ab)
        kh_t = yield from stacked_t(k_h)
        ak = yield from _each(_mm, al_rt, kh_t)
        a_ak = [_bf(jnp.where(strict_lower, v[0:L], 0.0)) for v in ak]
        a_rk = [_bf(jnp.where(lower, v[L:L2], 0.0)) for v in ak]
        v_s = yield from stacked(vr)
        kr_t = yield from stacked_t(k_r)
        by_v = yield from _each(
            lambda p, q, t, v: _mm(jnp.concatenate([p, q, t], axis=0), v), a_ak, a_rk, kr_t, v_s)
        akv = [v[0:L] for v in by_v]
        y_kv = [v[L:L2] for v in by_v]
        s_kv = [v[L2:L2 + LANES] for v in by_v]
        for li in range(len(levels) - 2, -1, -1):
            msk = lmask[li]
            db = yield from _each(_bf, d)
            dm = yield from _each(
                lambda p, v: _bf(_mm(p, stack_t(_bf(jnp.where(msk, v, 0.0))))), db, a_ab)
            d = yield from _each(lambda v, p, q: v + _mm(p, stack_t(q)), d, dm, db)
        al_s = yield from stacked(alpha_t)
        wu = yield from _each(
            lambda t, p, q: _mm(_bf(t), jnp.concatenate([p, stack(_bf(q))], axis=1)), d, al_s, akv)
        w_t = [_bf(v[:, 0:LANES]) for v in wu]
        u_p = [v[:, LANES:2 * LANES] for v in wu]
        br_t = yield from stacked_t(beta_r)
        by_u = [jnp.concatenate([p, q], axis=0) for p, q in zip(br_t, a_rb)]
        g_col = yield from _each(
            lambda cj: jnp.transpose(jnp.broadcast_to(pick(g_last, *cj)[0:1], (LANES, LANES))), items)
        yield "mats"

        def rwkv_step(idx, states):
            stb = [_bf(s) for s in states]
            wr = [_mm(jnp.concatenate([w_t[i], rt[i]], axis=0), sb) for i, sb in zip(idx, stb)]
            yield None
            ub = [stack(_bf(v[0:L] + u_p[i])) for i, v in zip(idx, wr)]
            bu = [_mm(by_u[i], u) for i, u in zip(idx, ub)]
            new = [s * g_col[i] + v[0:LANES] + s_kv[i] for i, s, v in zip(idx, states, bu)]
            yield None
            ys = [v[L:L2] + t[LANES:LANES + L] + y_kv[i] for i, v, t in zip(idx, wr, bu)]
            yield None
            return ys, new

        pairs = [(s, j) for s in seqs for j in range(N_RWKV_PAIRS)]
        y_items = [None] * len(items)
        states = [rst[sj] for sj in pairs]
        for c in range(nc):
            idx = [(s * nc + c) * N_RWKV_PAIRS + j for s, j in pairs]
            outs, states = yield from rwkv_step(idx, states)
            for i, o in zip(idx, outs):
                y_items[i] = o
        for sj, st in zip(pairs, states):
            rst[sj] = st
        y = jnp.concatenate(
            [jnp.concatenate([y_items[c * N_RWKV_PAIRS + j] for j in range(N_RWKV_PAIRS)], axis=1)
             for c in chunks], axis=0)
        yield None
        mean = segsum(y) * (1.0 / RWKV_HEAD)
        yc = y - mean
        yield None
        var = segsum(yc * yc) * (1.0 / RWKV_HEAD)
        yield None
        y = yc * lax.rsqrt(var + RWKV_GN_EPS) * lnw[...] + lnb[...]
        yield None
        bonus = segsum(r * kr * rk_w[...]) * vr
        yield None
        zr = pbuf[:, C_ZR:C_ZR + W]
        mbuf[:, GLA_WIDTH:GLA_WIDTH + W] = _bf((y + bonus) * silu(zr))
        while True:
            yield "end"

    def gla_stream():
        each1 = functools.partial(_each, group=1)
        project(C_GD, N_COLS)
        project(C_Q, C_V)
        yield None
        gl = _mm(_bf(pbuf[:, C_GD:C_GD + LANES]), aup[...]) + abias[...]
        log2_a = ((jnp.minimum(gl, 0.0) - jnp.log1p(jnp.exp(-jnp.abs(gl))))
                  * (LOG2E / GLA_TEMP))
        parts = _split2(log2_a)
        yield None
        project(C_V, C_ZG)
        yield None
        ex = []
        for c in chunks:
            ex.append(jnp.exp2(chunk_cumsum(e_ref, parts, c)))
            yield None
        project(C_ZG, C_SH)
        yield "decays"

        gitems = [(c, j) for c in chunks for j in range(N_GLA_PAIRS)]
        q = pbuf[:, C_Q:C_Q + GLA_KEY] * (GLA_DK ** -0.5)
        k = pbuf[:, C_K:C_K + GLA_KEY]
        qj = [pick(q, c, j) for c, j in gitems]
        kj = [pick(k, c, j) for c, j in gitems]

        def exl(blk, c, j):
            return ex[c][blk * L:(blk + 1) * L, j * LANES:(j + 1) * LANES]

        att = yield from each1(
            lambda p, t: jnp.where(eye_t, _nt(stack(_bf(t)), _bf(p)), 0.0), qj, kj)
        for li in range(len(levels)):
            sec, msk = second_half[li], lmask_t[li]

            def level_term(v, p, t, cj):
                xl = jnp.where(sec, p, t) * exl(2 + li, *cj)
                ql = _bf(jnp.where(sec, xl, 0.0))
                kl = stack(_bf(jnp.where(sec, 0.0, xl)))
                return v + jnp.where(msk, _nt(kl, ql), 0.0)

            att = yield from each1(level_term, att, qj, kj, gitems)
        yield "att"

        def value_rows(cj):
            c, j = cj
            v0 = C_V + 2 * j * GLA_DV
            return _bf(jnp.concatenate(
                [pbuf[rows(c), v0:v0 + GLA_DV], pbuf[rows(c), v0 + GLA_DV:v0 + 2 * GLA_DV]], 0))

        gv_s = yield from each1(value_rows, gitems)
        g_kv = yield from each1(
            lambda t, p, cj: _tn(stack(_bf(p * exl(1, *cj))), t), gv_s, kj, gitems)
        q_e = yield from each1(lambda p, cj: p * exl(0, *cj), qj, gitems)
        decay = yield from each1(
            lambda cj: jnp.transpose(jnp.broadcast_to(exl(0, *cj)[L - 1:L], (LANES, LANES))), gitems)

        g_start = [None] * len(gitems)
        for s in seqs:
            for j in range(N_GLA_PAIRS):
                st = sg_out[s, j]
                for c in range(nc):
                    i = (s * nc + c) * N_GLA_PAIRS + j
                    g_start[i] = st
                    st = st * decay[i] + g_kv[i]
                sg_out[s, j] = st
        yield None
        for i, (c, j) in enumerate(gitems):
            gsb = _bf(g_start[i])
            attb = _bf(att[i])
            for hh in range(2):
                mine_k = head0 if hh == 0 else jnp.logical_not(head0)
                o = (_tn(attb[hh * L:(hh + 1) * L], gv_s[i][hh * L:(hh + 1) * L])
                     + _mm(_bf(jnp.where(mine_k, q_e[i], 0.0)), gsb))
                o = o * lax.rsqrt(jnp.mean(o * o, -1, keepdims=True) + RMS_EPS) * gnorm[...]
                c0 = (2 * j + hh) * GLA_DV
                zg = pbuf[rows(c), C_ZG + c0:C_ZG + c0 + GLA_DV]
                mbuf[rows(c), c0:c0 + GLA_DV] = _bf(o * silu(zg))
            yield None
        while True:
            yield "end"

    rw, gl = rwkv_stream(), gla_stream()
    while next(rw) != "proj":
        pass
    _alternate(rw, gl, "elem", "decays", STAGE_STEPS[0])
    _alternate(rw, gl, "mats", "att", STAGE_STEPS[1])
    _alternate(rw, gl, "end", "end", STAGE_STEPS[2])

    o = _mm(mbuf[...], wout[...])
    o = o * lax.rsqrt(jnp.mean(o * o, -1, keepdims=True) + RMS_EPS) * npost[...]
    y = x + o
    for s in seqs:
        y_ref[s] = y[s * seq_rows:(s + 1) * seq_rows]

    @pl.when(pl.program_id(1) == last_tile)
    def _():
        r_i = lax.broadcasted_iota(jnp.int32, (LANES, RWKV_HEAD), 0)
        c_i = lax.broadcasted_iota(jnp.int32, (LANES, RWKV_HEAD), 1)
        fold = _bf(jnp.where((r_i & (RWKV_HEAD - 1)) == c_i, 1.0, 0.0))
        for s in seqs:
            for j in range(N_RWKV_PAIRS):
                hi, mid, lo = _split3(jnp.transpose(rst[s, j]))
                sr_out[s, j] = _mm(hi, fold) + _mm(mid, fold) + _mm(lo, fold)


def _layer_call(layer, state_layer, x3d, sg, sr, ss, params, consts, *, seqs_per_tile, seq_rows, L):
    n_seq, n_tok, _ = x3d.shape
    tiles_per_group = n_tok // seq_rows
    tile_rows = seqs_per_tile * seq_rows

    x_spec = pl.BlockSpec((seqs_per_tile, seq_rows, D_MODEL), lambda g, t: (g, t, 0))

    def state_in_spec(arr):
        nd = arr.ndim
        return pl.BlockSpec((None, seqs_per_tile) + arr.shape[2:],
                            lambda g, t: (state_layer, g) + (0,) * (nd - 2))

    def state_out_spec(arr):
        nd = arr.ndim - 1
        return pl.BlockSpec((seqs_per_tile,) + arr.shape[2:], lambda g, t: (g,) + (0,) * (nd - 1))

    def layer_spec(arr):
        return pl.BlockSpec((None,) + arr.shape[1:], lambda g, t: (layer,) + (0,) * (arr.ndim - 1))

    def const_spec(arr):
        return pl.BlockSpec(arr.shape, lambda g, t: (0,) * arr.ndim)

    states = (sg, sr, ss)
    in_specs = ([x_spec] + [state_in_spec(s) for s in states]
                + [layer_spec(p) for p in params] + [const_spec(c) for c in consts])
    out_specs = [x_spec] + [state_out_spec(s) for s in states]
    out_shape = ([jax.ShapeDtypeStruct(x3d.shape, F32)]
                 + [jax.ShapeDtypeStruct(s.shape[1:], F32) for s in states])
    return pl.pallas_call(
        functools.partial(_layer_kernel, L=L, last_tile=tiles_per_group - 1),
        grid=(n_seq // seqs_per_tile, tiles_per_group),
        in_specs=in_specs,
        out_specs=out_specs,
        out_shape=out_shape,
        scratch_shapes=[pltpu.VMEM((tile_rows, N_COLS), F32), pltpu.VMEM((tile_rows, D_MODEL), BF16),
                        pltpu.VMEM((seqs_per_tile, N_RWKV_PAIRS, LANES, LANES), F32)],
        compiler_params=pltpu.CompilerParams(
            dimension_semantics=("arbitrary", "arbitrary"), vmem_limit_bytes=VMEM_LIMIT_BYTES),
    )(x3d, sg, sr, ss, *params, *consts)


def _gla_state_to_kernel(s):
    return s.reshape(s.shape[:-3] + (N_GLA_PAIRS, LANES, GLA_DV))


def _gla_state_from_kernel(s):
    return s.reshape(s.shape[:-3] + (GLA_HEADS, GLA_DK, GLA_DV))


def _row(p):
    return p.reshape(DEPTH, 1, -1)


def _constants(L):
    seg = np.kron(np.eye(2, dtype=np.float32), np.ones((RWKV_HEAD, RWKV_HEAD), np.float32))
    e_mat = _gla_exponent_matrix(L)
    tri = np.tril(np.ones((L, L), np.float32))
    return (jnp.asarray(np.tile(e_mat, (1, 2)), dtype=BF16),
            jnp.asarray(np.tile(tri, (1, 2)), dtype=BF16),
            jnp.asarray(np.concatenate([seg, seg], 0), dtype=BF16))


def _tiling(bp, seq, bs):
    prompt_seqs = 2 if bp % 2 == 0 else 1
    prompt_rows = PROMPT_STEP_ROWS // prompt_seqs if seq % PROMPT_STEP_ROWS == 0 else CHUNK
    sample_seqs = SAMPLE_SEQS_PER_STEP if bs % SAMPLE_SEQS_PER_STEP == 0 else 1
    return prompt_seqs, prompt_rows, sample_seqs


def kernel(x_prompt, x_sample, state_gla, state_rwkv, state_shift, norm_pre, norm_post, w_in,
           gla_a_up, gla_a_bias, gla_norm, rwkv_mu, rwkv_w_up, rwkv_w0, rwkv_a_up, rwkv_a0,
           rwkv_k_k, rwkv_k_a, rwkv_r_k, rwkv_ln_w, rwkv_ln_b, w_out):
    bp, seq, _ = x_prompt.shape
    bs, dec_seq, _ = x_sample.shape

    gd0 = 2 * GLA_KEY + 2 * GLA_WIDTH
    w_in = w_in.astype(BF16)
    w_gla = w_in[:, :, :gd0]
    w_rwkv = w_in[:, :, gd0 + GLA_RANK:]
    w_gd = jnp.pad(w_in[:, :, gd0:gd0 + GLA_RANK], ((0, 0), (0, 0), (0, LANES - GLA_RANK)))
    aup = jnp.pad(gla_a_up, ((0, 0), (0, LANES - GLA_RANK), (0, 0))).astype(BF16)
    zpad = jnp.zeros_like(rwkv_w_up)
    waup = (0.5 * jnp.concatenate([jnp.concatenate([rwkv_w_up, zpad], 1),
                                   jnp.concatenate([zpad, rwkv_a_up], 1)], 2)).astype(BF16)
    params = (_row(norm_pre), _row(norm_post), w_gla, w_rwkv, w_gd, aup, _row(gla_a_bias),
              _row(gla_norm), _row(rwkv_mu), waup, _row(0.5 * rwkv_w0), _row(0.5 * rwkv_a0), _row(rwkv_k_k),
              _row(rwkv_k_a), _row(rwkv_r_k), _row(rwkv_ln_w), _row(rwkv_ln_b), w_out.astype(BF16))

    prompt_seqs, prompt_rows, sample_seqs = _tiling(bp, seq, bs)
    consts_p = _constants(CHUNK)
    consts_s = _constants(dec_seq)

    yp, ys = x_prompt, x_sample
    zero_states = (jnp.zeros((1, bp, N_GLA_PAIRS, LANES, GLA_DV), F32),
                   jnp.zeros((1, bp) + RWKV_PAIR_STATE, F32),
                   jnp.zeros((1, bp, 1, SHIFT_W), F32))
    sample_states = (_gla_state_to_kernel(state_gla),
                     state_rwkv.reshape(state_rwkv.shape[:2] + RWKV_PAIR_STATE),
                     state_shift[:, :, None, :])
    outs = [[] for _ in range(6)]
    for l in range(DEPTH):
        yp, g1, r1, s1 = _layer_call(
            l, 0, yp, *zero_states, params, consts_p, seqs_per_tile=prompt_seqs,
            seq_rows=prompt_rows, L=CHUNK)
        ys, g2, r2, s2 = _layer_call(
            l, l, ys, *sample_states, params, consts_s, seqs_per_tile=sample_seqs,
            seq_rows=dec_seq, L=dec_seq)
        for lst, val in zip(outs, (g1, g2, r1, r2, s1, s2)):
            lst.append(val)
    gla_p, gla_s, rwkv_p, rwkv_s, shift_p, shift_s = (jnp.stack(v) for v in outs)
    return (yp, ys,
            _gla_state_from_kernel(gla_p), _gla_state_from_kernel(gla_s),
            rwkv_p.reshape((DEPTH, bp, RWKV_HEADS, RWKV_HEAD, RWKV_HEAD)),
            rwkv_s.reshape((DEPTH, bs, RWKV_HEADS, RWKV_HEAD, RWKV_HEAD)),
            shift_p[:, :, 0], shift_s[:, :, 0])
```

```python
import functools
import math

import numpy as np
import jax
import jax.numpy as jnp
from jax import lax
from jax.experimental import pallas as pl
from jax.experimental.pallas import tpu as pltpu

F32 = jnp.float32
BF16 = jnp.bfloat16

D_MODEL = 1024
DEPTH = 4
CHUNK = 64
RMS_EPS = 1e-6
GLA_WIDTH = 512
GLA_HEADS = 4
GLA_DV = 128
GLA_DK = 64
GLA_KEY = 256
GLA_RANK = 16
GLA_TEMP = 16.0
RWKV_WIDTH = 512
RWKV_HEAD = 64
RWKV_HEADS = 8
RWKV_RANK = 64
SHIFT_W = 3 * RWKV_WIDTH + 2 * RWKV_RANK
RWKV_GN_EPS = 64e-5
RWKV_DECAY_SCALE = 0.606531
LOG2E = math.log2(math.e)

LANES = 128
N_GLA_PAIRS = GLA_HEADS // 2
N_RWKV_PAIRS = RWKV_HEADS // 2
RWKV_PAIR_STATE = (N_RWKV_PAIRS, 2 * RWKV_HEAD, RWKV_HEAD)

C_Q = 0
C_K = C_Q + GLA_KEY
C_V = C_K + GLA_KEY
C_ZG = C_V + GLA_WIDTH
C_SH = C_ZG + GLA_WIDTH
C_ZR = C_SH + SHIFT_W
C_GD = C_ZR + RWKV_WIDTH
N_COLS = C_GD + LANES

VMEM_LIMIT_BYTES = 56 * 1024 * 1024
PROMPT_STEP_ROWS = 512
SAMPLE_SEQS_PER_STEP = 8
ITEMS_PER_STEP = 4
STAGE_STEPS = ((2, 1), (2, 1), (1, 2))
KK_NORM_FLOOR = 1e-12
HALF_LOG2_DECAY = -0.5 * RWKV_DECAY_SCALE * LOG2E


def _nt(a, b):
    return lax.dot_general(a, b, (((1,), (1,)), ((), ())), preferred_element_type=F32)


def _tn(a, b):
    return lax.dot_general(a, b, (((0,), (0,)), ((), ())), preferred_element_type=F32)


def _mm(a, b):
    return jnp.dot(a, b, preferred_element_type=F32)


def _bf(v):
    return v.astype(BF16)


def _split3(v):
    hi = _bf(v)
    r1 = v - hi.astype(F32)
    mid = _bf(r1)
    return hi, mid, _bf(r1 - mid.astype(F32))


def _split2(v):
    hi = _bf(v)
    return hi, _bf(v - hi.astype(F32))


def _each(fn, *lists, group=ITEMS_PER_STEP):
    out = []
    for i, args in enumerate(zip(*lists)):
        out.append(fn(*args))
        if i % group == group - 1:
            yield None
    return out


def _alternate(ga, gb, stop_a, stop_b, ratio):
    done_a = done_b = False
    while not (done_a and done_b):
        for _ in range(ratio[0]):
            if not done_a:
                done_a = next(ga) == stop_a
        for _ in range(ratio[1]):
            if not done_b:
                done_b = next(gb) == stop_b


def _pair_rows(v, first=None):
    if first is None:
        first = lax.broadcasted_iota(jnp.int32, (1, LANES), 1) < RWKV_HEAD
    zero = jnp.zeros((), v.dtype)
    return jnp.concatenate([jnp.where(first, v, zero), jnp.where(first, zero, v)], axis=0)


def _levels(L):
    out, h = [], L // 2
    while h >= 1:
        out.append(h)
        h //= 2
    return out


def _gla_exponent_matrix(L):
    t = np.arange(L)[:, None]
    i = np.arange(L)[None, :]
    blocks = [i <= t, i > t]
    for h in _levels(L):
        mid = (t // (2 * h)) * (2 * h) + h
        second = t >= mid
        blocks.append(np.where(second, (i >= mid) & (i <= t), (i > t) & (i < mid)))
    return np.concatenate(blocks, 0).astype(np.float32)


def _layer_kernel(x_ref, sg_in, sr_in, ss_in, npre, npost, win, aup, abias,
                  gnorm, mu_ref, waup, w0, a0, kk_w, ka_w, rk_w, lnw, lnb, wout, e_ref, tri_ref, seg_ref,
                  y_ref, sg_out, sr_out, ss_out, pbuf, mbuf, rst, *, L, last_tile):
    nb, seq_rows, _ = x_ref.shape
    nc = seq_rows // L
    TT = nb * seq_rows
    L2 = 2 * L
    W = RWKV_WIDTH
    levels = _levels(L)
    seqs = range(nb)
    chunks = range(nb * nc)

    @pl.when(pl.program_id(1) == 0)
    def _():
        sg_out[...] = sg_in[...]
        ss_out[...] = ss_in[...]
        dup = _bf(jnp.where(
            lax.broadcasted_iota(jnp.int32, (RWKV_HEAD, LANES), 0)
            == (lax.broadcasted_iota(jnp.int32, (RWKV_HEAD, LANES), 1) & (RWKV_HEAD - 1)), 1.0, 0.0))
        same_head = ((lax.broadcasted_iota(jnp.int32, (LANES, LANES), 0) < RWKV_HEAD)
                     == (lax.broadcasted_iota(jnp.int32, (LANES, LANES), 1) < RWKV_HEAD))
        for s in seqs:
            for j in range(N_RWKV_PAIRS):
                hi, mid, lo = _split3(sr_in[s, j])
                both = _mm(hi, dup) + _mm(mid, dup) + _mm(lo, dup)
                rst[s, j] = jnp.transpose(jnp.where(same_head, both, 0.0))

    x = jnp.concatenate([x_ref[s] for s in seqs], axis=0)
    hb = _bf(x * lax.rsqrt(jnp.mean(x * x, -1, keepdims=True) + RMS_EPS) * npre[...])

    lane = lax.broadcasted_iota(jnp.int32, (1, LANES), 1)
    head0 = lane < RWKV_HEAD
    tlane = lax.broadcasted_iota(jnp.int32, (1, L2), 1)
    tcol0 = tlane < L
    row = lax.broadcasted_iota(jnp.int32, (L, L2), 0)
    col = lax.broadcasted_iota(jnp.int32, (L, L2), 1) & (L - 1)
    eye = row == col
    strict_lower = row > col
    lower = row >= col
    crow = lax.broadcasted_iota(jnp.int32, (L, 1), 0)
    trow = lax.broadcasted_iota(jnp.int32, (TT, 1), 0)

    def level_mask(hsz, t_idx, s_idx):
        sh = hsz.bit_length()
        same = (t_idx >> sh) == (s_idx >> sh)
        return same & (((t_idx >> (sh - 1)) & 1) == 1) & (((s_idx >> (sh - 1)) & 1) == 0)

    lmask = [level_mask(hsz, row, col) for hsz in levels]
    src_t = lax.broadcasted_iota(jnp.int32, (L2, L), 0) & (L - 1)
    tok_t = lax.broadcasted_iota(jnp.int32, (L2, L), 1)
    eye_t = src_t == tok_t
    same_head_t = ((lax.broadcasted_iota(jnp.int32, (LANES, L2), 0) < RWKV_HEAD)
                   == (lax.broadcasted_iota(jnp.int32, (LANES, L2), 1) < L))
    lmask_t = [level_mask(hsz, tok_t, src_t) for hsz in levels]
    second_half = [((crow >> (hsz.bit_length() - 1)) & 1) == 1 for hsz in levels]

    def rows(c):
        return slice(c * L, (c + 1) * L)

    def pick(v, c, j):
        return v[rows(c), j * LANES:(j + 1) * LANES]

    def project(c0, c1):
        pbuf[:, c0:c1] = _mm(hb, win[:, c0:c1])

    def project_rwkv():
        p = _mm(hb, win[:, C_SH:])
        pbuf[:, C_GD:N_COLS] = p[:, 0:LANES]
        pbuf[:, C_SH:C_GD] = p[:, GLA_RANK:GLA_RANK + C_GD - C_SH]

    def stack(v):
        return _pair_rows(v, head0)

    def stack_t(v):
        return _pair_rows(v, tcol0)

    def segsum(v):
        vs = jnp.concatenate([v[:, i * LANES:(i + 1) * LANES] for i in range(4)], axis=0)
        s = _mm(jnp.concatenate(_split2(vs), axis=1), seg_ref[...])
        return jnp.concatenate([s[i * TT:(i + 1) * TT] for i in range(4)], axis=1)

    def chunk_cumsum(mat_ref, parts, c):
        return _mm(mat_ref[...], jnp.concatenate([p[rows(c)] for p in parts], axis=0))

    def last_row_bcast(v):
        return jnp.concatenate(
            [jnp.broadcast_to(v[(c + 1) * L - 1:(c + 1) * L], (L, v.shape[1])) for c in chunks], 0)

    def sigmoid_of_double(half_v):
        return 0.5 * jnp.tanh(half_v) + 0.5

    def silu(v):
        hv = 0.5 * v
        return hv * (jnp.tanh(hv) + 1.0)

    def rwkv_stream():
        project_rwkv()
        yield "proj"
        cur = pbuf[:, C_SH:C_SH + SHIFT_W]
        first = jnp.concatenate(
            [jnp.broadcast_to(ss_out[s], (seq_rows, SHIFT_W)) for s in seqs], axis=0)
        prev = jnp.where((trow & (seq_rows - 1)) == 0, first, pltpu.roll(cur, 1, 0))
        for s in seqs:
            ss_out[s] = cur[(s + 1) * seq_rows - 1:(s + 1) * seq_rows]
        yield None
        xm = cur + mu_ref[...] * (prev - cur)
        yield None
        r, kr, vr = xm[:, 0:W], xm[:, W:2 * W], xm[:, 2 * W:3 * W]
        wa = xm[:, 3 * W:3 * W + LANES]
        wa = jnp.where(head0, jnp.tanh(wa), wa)
        wa_pre = _mm(_bf(wa), waup[...])
        yield None
        log2_w = HALF_LOG2_DECAY * jnp.tanh(w0[...] + wa_pre[:, 0:W]) + HALF_LOG2_DECAY
        yield None
        a = sigmoid_of_double(a0[...] + wa_pre[:, W:2 * W])
        yield None
        kk = kr * kk_w[...]
        kr = kr * (1.0 + (a - 1.0) * ka_w[...])
        yield None
        kk = kk * jnp.minimum(lax.rsqrt(segsum(kk * kk)), 1.0 / KK_NORM_FLOOR)
        yield None
        parts = _split2(log2_w)
        cum = jnp.concatenate([chunk_cumsum(tri_ref, parts, c) for c in chunks], axis=0)
        cum_last = last_row_bcast(cum)
        yield None
        g_inv = jnp.exp2(-cum)
        g_last = jnp.exp2(cum_last)
        beta = a * kk
        yield None
        alpha_t = -kk * jnp.exp2(cum - log2_w)
        yield None
        r_t = r * jnp.exp2(cum)
        yield None
        beta_h, k_h = beta * g_inv, kr * g_inv
        yield None
        beta_r, k_r = beta_h * g_last, k_h * g_last
        yield "elem"

        items = [(c, j) for c in chunks for j in range(N_RWKV_PAIRS)]

        def picked(v):
            return [_bf(pick(v, *cj)) for cj in items]

        def stacked(v):
            return _each(lambda cj: stack(_bf(pick(v, *cj))), items)

        def stacked_t(v):
            def one(cj):
                v2 = pick(v, *cj)
                vt = _bf(jnp.transpose(jnp.concatenate([v2, v2], axis=0)))
                return jnp.where(same_head_t, vt, jnp.zeros((), BF16))

            return _each(one, items)

        rt = picked(r_t)
        al_rt = [jnp.concatenate([p, q], axis=0) for p, q in zip(picked(alpha_t), rt)]
        bh_t = yield from stacked_t(beta_h)
        ab = yield from _each(_mm, al_rt, bh_t)
        a_ab = [v[0:L] for v in ab]
        a_rb = [_bf(jnp.where(lower, v[L:L2], 0.0)) for v in ab]
        eyef = jnp.where(eye, 1.0, 0.0)
        d = yield from _each(lambda v: eyef + jnp.where(lmask[-1], v, 0.0), a_ab)
        kh_t = yield from stacked_t(k_h)
        ak = yield from _each(_mm, al_rt, kh_t)
        a_ak = [_bf(jnp.where(strict_lower, v[0:L], 0.0)) for v in ak]
        a_rk = [_bf(jnp.where(lower, v[L:L2], 0.0)) for v in ak]
        v_s = yield from stacked(vr)
        kr_t = yield from stacked_t(k_r)
        by_v = yield from _each(
            lambda p, q, t, v: _mm(jnp.concatenate([p, q, t], axis=0), v), a_ak, a_rk, kr_t, v_s)
        akv = [v[0:L] for v in by_v]
        y_kv = [v[L:L2] for v in by_v]
        s_kv = [v[L2:L2 + LANES] for v in by_v]
        for li in range(len(levels) - 2, -1, -1):
            msk = lmask[li]
            db = yield from _each(_bf, d)
            dm = yield from _each(
                lambda p, v: _bf(_mm(p, stack_t(_bf(jnp.where(msk, v, 0.0))))), db, a_ab)
            d = yield from _each(lambda v, p, q: v + _mm(p, stack_t(q)), d, dm, db)
        al_s = yield from stacked(alpha_t)
        wu = yield from _each(
            lambda t, p, q: _mm(_bf(t), jnp.concatenate([p, stack(_bf(q))], axis=1)), d, al_s, akv)
        w_t = [_bf(v[:, 0:LANES]) for v in wu]
        u_p = [v[:, LANES:2 * LANES] for v in wu]
        br_t = yield from stacked_t(beta_r)
        by_u = [jnp.concatenate([p, q], axis=0) for p, q in zip(br_t, a_rb)]
        g_col = yield from _each(
            lambda cj: jnp.transpose(jnp.broadcast_to(pick(g_last, *cj)[0:1], (LANES, LANES))), items)
        yield "mats"

        def rwkv_step(idx, states):
            stb = [_bf(s) for s in states]
            wr = [_mm(jnp.concatenate([w_t[i], rt[i]], axis=0), sb) for i, sb in zip(idx, stb)]
            yield None
            ub = [stack(_bf(v[0:L] + u_p[i])) for i, v in zip(idx, wr)]
            bu = [_mm(by_u[i], u) for i, u in zip(idx, ub)]
            new = [s * g_col[i] + v[0:LANES] + s_kv[i] for i, s, v in zip(idx, states, bu)]
            yield None
            ys = [v[L:L2] + t[LANES:LANES + L] + y_kv[i] for i, v, t in zip(idx, wr, bu)]
            yield None
            return ys, new

        pairs = [(s, j) for s in seqs for j in range(N_RWKV_PAIRS)]
        y_items = [None] * len(items)
        states = [rst[sj] for sj in pairs]
        for c in range(nc):
            idx = [(s * nc + c) * N_RWKV_PAIRS + j for s, j in pairs]
            outs, states = yield from rwkv_step(idx, states)
            for i, o in zip(idx, outs):
                y_items[i] = o
        for sj, st in zip(pairs, states):
            rst[sj] = st
        y = jnp.concatenate(
            [jnp.concatenate([y_items[c * N_RWKV_PAIRS + j] for j in range(N_RWKV_PAIRS)], axis=1)
             for c in chunks], axis=0)
        yield None
        mean = segsum(y) * (1.0 / RWKV_HEAD)
        yc = y - mean
        yield None
        var = segsum(yc * yc) * (1.0 / RWKV_HEAD)
        yield None
        y = yc * lax.rsqrt(var + RWKV_GN_EPS) * lnw[...] + lnb[...]
        yield None
        bonus = segsum(r * kr * rk_w[...]) * vr
        yield None
        zr = pbuf[:, C_ZR:C_ZR + W]
        mbuf[:, GLA_WIDTH:GLA_WIDTH + W] = _bf((y + bonus) * silu(zr))
        while True:
            yield "end"

    def gla_stream():
        each1 = functools.partial(_each, group=1)
        project(C_Q, C_V)
        yield None
        gl = _mm(_bf(pbuf[:, C_GD:C_GD + LANES]), aup[...]) + abias[...]
        log2_a = ((jnp.minimum(gl, 0.0) - jnp.log1p(jnp.exp(-jnp.abs(gl))))
                  * (LOG2E / GLA_TEMP))
        parts = _split2(log2_a)
        yield None
        project(C_V, C_ZG)
        yield None
        ex = []
        for c in chunks:
            ex.append(jnp.exp2(chunk_cumsum(e_ref, parts, c)))
            yield None
        project(C_ZG, C_SH)
        yield "decays"

        gitems = [(c, j) for c in chunks for j in range(N_GLA_PAIRS)]
        q = pbuf[:, C_Q:C_Q + GLA_KEY] * (GLA_DK ** -0.5)
        k = pbuf[:, C_K:C_K + GLA_KEY]
        qj = [pick(q, c, j) for c, j in gitems]
        kj = [pick(k, c, j) for c, j in gitems]

        def exl(blk, c, j):
            return ex[c][blk * L:(blk + 1) * L, j * LANES:(j + 1) * LANES]

        att = yield from each1(
            lambda p, t: jnp.where(eye_t, _nt(stack(_bf(t)), _bf(p)), 0.0), qj, kj)
        for li in range(len(levels)):
            sec, msk = second_half[li], lmask_t[li]

            def level_term(v, p, t, cj):
                xl = jnp.where(sec, p, t) * exl(2 + li, *cj)
                ql = _bf(jnp.where(sec, xl, 0.0))
                kl = stack(_bf(jnp.where(sec, 0.0, xl)))
                return v + jnp.where(msk, _nt(kl, ql), 0.0)

            att = yield from each1(level_term, att, qj, kj, gitems)
        yield "att"

        def value_rows(cj):
            c, j = cj
            v0 = C_V + 2 * j * GLA_DV
            return _bf(jnp.concatenate(
                [pbuf[rows(c), v0:v0 + GLA_DV], pbuf[rows(c), v0 + GLA_DV:v0 + 2 * GLA_DV]], 0))

        gv_s = yield from each1(value_rows, gitems)
        g_kv = yield from each1(
            lambda t, p, cj: _tn(stack(_bf(p * exl(1, *cj))), t), gv_s, kj, gitems)
        q_e = yield from each1(lambda p, cj: p * exl(0, *cj), qj, gitems)
        decay = yield from each1(
            lambda cj: jnp.transpose(jnp.broadcast_to(exl(0, *cj)[L - 1:L], (LANES, LANES))), gitems)

        g_start = [None] * len(gitems)
        for s in seqs:
            for j in range(N_GLA_PAIRS):
                st = sg_out[s, j]
                for c in range(nc):
                    i = (s * nc + c) * N_GLA_PAIRS + j
                    g_start[i] = st
                    st = st * decay[i] + g_kv[i]
                sg_out[s, j] = st
        yield None
        for i, (c, j) in enumerate(gitems):
            gsb = _bf(g_start[i])
            attb = _bf(att[i])
            for hh in range(2):
                mine_k = head0 if hh == 0 else jnp.logical_not(head0)
                o = (_tn(attb[hh * L:(hh + 1) * L], gv_s[i][hh * L:(hh + 1) * L])
                     + _mm(_bf(jnp.where(mine_k, q_e[i], 0.0)), gsb))
                o = o * lax.rsqrt(jnp.mean(o * o, -1, keepdims=True) + RMS_EPS) * gnorm[...]
                c0 = (2 * j + hh) * GLA_DV
                zg = pbuf[rows(c), C_ZG + c0:C_ZG + c0 + GLA_DV]
                mbuf[rows(c), c0:c0 + GLA_DV] = _bf(o * silu(zg))
            yield None
        while True:
            yield "end"

    rw, gl = rwkv_stream(), gla_stream()
    while next(rw) != "proj":
        pass
    _alternate(rw, gl, "elem", "decays", STAGE_STEPS[0])
    _alternate(rw, gl, "mats", "att", STAGE_STEPS[1])
    _alternate(rw, gl, "end", "end", STAGE_STEPS[2])

    o = _mm(mbuf[...], wout[...])
    o = o * lax.rsqrt(jnp.mean(o * o, -1, keepdims=True) + RMS_EPS) * npost[...]
    y = x + o
    for s in seqs:
        y_ref[s] = y[s * seq_rows:(s + 1) * seq_rows]

    @pl.when(pl.program_id(1) == last_tile)
    def _():
        r_i = lax.broadcasted_iota(jnp.int32, (LANES, RWKV_HEAD), 0)
        c_i = lax.broadcasted_iota(jnp.int32, (LANES, RWKV_HEAD), 1)
        fold = _bf(jnp.where((r_i & (RWKV_HEAD - 1)) == c_i, 1.0, 0.0))
        for s in seqs:
            for j in range(N_RWKV_PAIRS):
                hi, mid, lo = _split3(jnp.transpose(rst[s, j]))
                sr_out[s, j] = _mm(hi, fold) + _mm(mid, fold) + _mm(lo, fold)


def _layer_call(layer, state_layer, x3d, sg, sr, ss, params, consts, *, seqs_per_tile, seq_rows, L):
    n_seq, n_tok, _ = x3d.shape
    tiles_per_group = n_tok // seq_rows
    tile_rows = seqs_per_tile * seq_rows

    x_spec = pl.BlockSpec((seqs_per_tile, seq_rows, D_MODEL), lambda g, t: (g, t, 0))

    def state_in_spec(arr):
        nd = arr.ndim
        return pl.BlockSpec((None, seqs_per_tile) + arr.shape[2:],
                            lambda g, t: (state_layer, g) + (0,) * (nd - 2))

    def state_out_spec(arr):
        nd = arr.ndim - 1
        return pl.BlockSpec((seqs_per_tile,) + arr.shape[2:], lambda g, t: (g,) + (0,) * (nd - 1))

    def layer_spec(arr):
        return pl.BlockSpec((None,) + arr.shape[1:], lambda g, t: (layer,) + (0,) * (arr.ndim - 1))

    def const_spec(arr):
        return pl.BlockSpec(arr.shape, lambda g, t: (0,) * arr.ndim)

    states = (sg, sr, ss)
    in_specs = ([x_spec] + [state_in_spec(s) for s in states]
                + [layer_spec(p) for p in params] + [const_spec(c) for c in consts])
    out_specs = [x_spec] + [state_out_spec(s) for s in states]
    out_shape = ([jax.ShapeDtypeStruct(x3d.shape, F32)]
                 + [jax.ShapeDtypeStruct(s.shape[1:], F32) for s in states])
    return pl.pallas_call(
        functools.partial(_layer_kernel, L=L, last_tile=tiles_per_group - 1),
        grid=(n_seq // seqs_per_tile, tiles_per_group),
        in_specs=in_specs,
        out_specs=out_specs,
        out_shape=out_shape,
        scratch_shapes=[pltpu.VMEM((tile_rows, N_COLS), F32), pltpu.VMEM((tile_rows, D_MODEL), BF16),
                        pltpu.VMEM((seqs_per_tile, N_RWKV_PAIRS, LANES, LANES), F32)],
        compiler_params=pltpu.CompilerParams(
            dimension_semantics=("arbitrary", "arbitrary"), vmem_limit_bytes=VMEM_LIMIT_BYTES),
    )(x3d, sg, sr, ss, *params, *consts)


def _gla_state_to_kernel(s):
    return s.reshape(s.shape[:-3] + (N_GLA_PAIRS, LANES, GLA_DV))


def _gla_state_from_kernel(s):
    return s.reshape(s.shape[:-3] + (GLA_HEADS, GLA_DK, GLA_DV))


def _row(p):
    return p.reshape(DEPTH, 1, -1)


def _constants(L):
    seg = np.kron(np.eye(2, dtype=np.float32), np.ones((RWKV_HEAD, RWKV_HEAD), np.float32))
    e_mat = _gla_exponent_matrix(L)
    tri = np.tril(np.ones((L, L), np.float32))
    return (jnp.asarray(np.tile(e_mat, (1, 2)), dtype=BF16),
            jnp.asarray(np.tile(tri, (1, 2)), dtype=BF16),
            jnp.asarray(np.concatenate([seg, seg], 0), dtype=BF16))


def _tiling(bp, seq, bs):
    prompt_seqs = 2 if bp % 2 == 0 else 1
    prompt_rows = PROMPT_STEP_ROWS // prompt_seqs if seq % PROMPT_STEP_ROWS == 0 else CHUNK
    sample_seqs = SAMPLE_SEQS_PER_STEP if bs % SAMPLE_SEQS_PER_STEP == 0 else 1
    return prompt_seqs, prompt_rows, sample_seqs


def kernel(x_prompt, x_sample, state_gla, state_rwkv, state_shift, norm_pre, norm_post, w_in,
           gla_a_up, gla_a_bias, gla_norm, rwkv_mu, rwkv_w_up, rwkv_w0, rwkv_a_up, rwkv_a0,
           rwkv_k_k, rwkv_k_a, rwkv_r_k, rwkv_ln_w, rwkv_ln_b, w_out):
    bp, seq, _ = x_prompt.shape
    bs, dec_seq, _ = x_sample.shape

    aup = jnp.pad(gla_a_up, ((0, 0), (0, LANES - GLA_RANK), (0, 0))).astype(BF16)
    zpad = jnp.zeros_like(rwkv_w_up)
    waup = (0.5 * jnp.concatenate([jnp.concatenate([rwkv_w_up, zpad], 1),
                                   jnp.concatenate([zpad, rwkv_a_up], 1)], 2)).astype(BF16)
    params = (_row(norm_pre), _row(norm_post), w_in.astype(BF16), aup, _row(gla_a_bias),
              _row(gla_norm), _row(rwkv_mu), waup, _row(0.5 * rwkv_w0), _row(0.5 * rwkv_a0), _row(rwkv_k_k),
              _row(rwkv_k_a), _row(rwkv_r_k), _row(rwkv_ln_w), _row(rwkv_ln_b), w_out.astype(BF16))

    prompt_seqs, prompt_rows, sample_seqs = _tiling(bp, seq, bs)
    consts_p = _constants(CHUNK)
    consts_s = _constants(dec_seq)

    yp, ys = x_prompt, x_sample
    zero_states = (jnp.zeros((1, bp, N_GLA_PAIRS, LANES, GLA_DV), F32),
                   jnp.zeros((1, bp) + RWKV_PAIR_STATE, F32),
                   jnp.zeros((1, bp, 1, SHIFT_W), F32))
    sample_states = (_gla_state_to_kernel(state_gla),
                     state_rwkv.reshape(state_rwkv.shape[:2] + RWKV_PAIR_STATE),
                     state_shift[:, :, None, :])
    outs = [[] for _ in range(6)]
    for l in range(DEPTH):
        yp, g1, r1, s1 = _layer_call(
            l, 0, yp, *zero_states, params, consts_p, seqs_per_tile=prompt_seqs,
            seq_rows=prompt_rows, L=CHUNK)
        ys, g2, r2, s2 = _layer_call(
            l, l, ys, *sample_states, params, consts_s, seqs_per_tile=sample_seqs,
            seq_rows=dec_seq, L=dec_seq)
        for lst, val in zip(outs, (g1, g2, r1, r2, s1, s2)):
            lst.append(val)
    gla_p, gla_s, rwkv_p, rwkv_s, shift_p, shift_s = (jnp.stack(v) for v in outs)
    return (yp, ys,
            _gla_state_from_kernel(gla_p), _gla_state_from_kernel(gla_s),
            rwkv_p.reshape((DEPTH, bp, RWKV_HEADS, RWKV_HEAD, RWKV_HEAD)),
            rwkv_s.reshape((DEPTH, bs, RWKV_HEADS, RWKV_HEAD, RWKV_HEAD)),
            shift_p[:, :, 0], shift_s[:, :, 0])
```

```python
import functools
import math

import numpy as np
import jax
import jax.numpy as jnp
from jax import lax
from jax.experimental import pallas as pl
from jax.experimental.pallas import tpu as pltpu

F32 = jnp.float32
BF16 = jnp.bfloat16

D_MODEL = 1024
DEPTH = 4
CHUNK = 64
RMS_EPS = 1e-6
GLA_WIDTH = 512
GLA_HEADS = 4
GLA_DV = 128
GLA_DK = 64
GLA_KEY = 256
GLA_RANK = 16
GLA_TEMP = 16.0
RWKV_WIDTH = 512
RWKV_HEAD = 64
RWKV_HEADS = 8
RWKV_RANK = 64
SHIFT_W = 3 * RWKV_WIDTH + 2 * RWKV_RANK
RWKV_GN_EPS = 64e-5
RWKV_DECAY_SCALE = 0.606531
LOG2E = math.log2(math.e)

LANES = 128
N_GLA_PAIRS = GLA_HEADS // 2
N_RWKV_PAIRS = RWKV_HEADS // 2
RWKV_PAIR_STATE = (N_RWKV_PAIRS, 2 * RWKV_HEAD, RWKV_HEAD)

C_Q = 0
C_K = C_Q + GLA_KEY
C_V = C_K + GLA_KEY
C_ZG = C_V + GLA_WIDTH
C_SH = C_ZG + GLA_WIDTH
C_ZR = C_SH + SHIFT_W
C_GD = C_ZR + RWKV_WIDTH
N_COLS = C_GD + LANES

VMEM_LIMIT_BYTES = 56 * 1024 * 1024
PROMPT_STEP_ROWS = 512
SAMPLE_SEQS_PER_STEP = 8
ITEMS_PER_STEP = 4
STAGE_STEPS = ((2, 1), (2, 1), (1, 2))
KK_NORM_FLOOR = 1e-12
HALF_LOG2_DECAY = -0.5 * RWKV_DECAY_SCALE * LOG2E


def _nt(a, b):
    return lax.dot_general(a, b, (((1,), (1,)), ((), ())), preferred_element_type=F32)


def _tn(a, b):
    return lax.dot_general(a, b, (((0,), (0,)), ((), ())), preferred_element_type=F32)


def _mm(a, b):
    return jnp.dot(a, b, preferred_element_type=F32)


def _bf(v):
    return v.astype(BF16)


def _split3(v):
    hi = _bf(v)
    r1 = v - hi.astype(F32)
    mid = _bf(r1)
    return hi, mid, _bf(r1 - mid.astype(F32))


def _split2(v):
    hi = _bf(v)
    return hi, _bf(v - hi.astype(F32))


def _each(fn, *lists, group=ITEMS_PER_STEP):
    out = []
    for i, args in enumerate(zip(*lists)):
        out.append(fn(*args))
        if i % group == group - 1:
            yield None
    return out


def _alternate(ga, gb, stop_a, stop_b, ratio):
    done_a = done_b = False
    while not (done_a and done_b):
        for _ in range(ratio[0]):
            if not done_a:
                done_a = next(ga) == stop_a
        for _ in range(ratio[1]):
            if not done_b:
                done_b = next(gb) == stop_b


def _pair_rows(v, first=None):
    if first is None:
        first = lax.broadcasted_iota(jnp.int32, (1, LANES), 1) < RWKV_HEAD
    zero = jnp.zeros((), v.dtype)
    return jnp.concatenate([jnp.where(first, v, zero), jnp.where(first, zero, v)], axis=0)


def _levels(L):
    out, h = [], L // 2
    while h >= 1:
        out.append(h)
        h //= 2
    return out


def _gla_exponent_matrix(L):
    t = np.arange(L)[:, None]
    i = np.arange(L)[None, :]
    blocks = [i <= t, i > t]
    for h in _levels(L):
        mid = (t // (2 * h)) * (2 * h) + h
        second = t >= mid
        blocks.append(np.where(second, (i >= mid) & (i <= t), (i > t) & (i < mid)))
    return np.concatenate(blocks, 0).astype(np.float32)


def _layer_kernel(x_ref, sg_in, sr_in, ss_in, npre, npost, win, aup, abias,
                  gnorm, mu_ref, waup, w0, a0, kk_w, ka_w, rk_w, lnw, lnb, wout, e_ref, tri_ref, seg_ref,
                  acc_g, acc_r, acc_s, y_ref, sg_out, sr_out, ss_out, pbuf, mbuf, rst, *, L, last_tile):
    del acc_g, acc_r, acc_s
    nb, seq_rows, _ = x_ref.shape
    nc = seq_rows // L
    TT = nb * seq_rows
    L2 = 2 * L
    W = RWKV_WIDTH
    levels = _levels(L)
    seqs = range(nb)
    chunks = range(nb * nc)

    @pl.when(pl.program_id(1) == 0)
    def _():
        sg_out[...] = sg_in[...]
        ss_out[...] = ss_in[...]
        dup = _bf(jnp.where(
            lax.broadcasted_iota(jnp.int32, (RWKV_HEAD, LANES), 0)
            == (lax.broadcasted_iota(jnp.int32, (RWKV_HEAD, LANES), 1) & (RWKV_HEAD - 1)), 1.0, 0.0))
        same_head = ((lax.broadcasted_iota(jnp.int32, (LANES, LANES), 0) < RWKV_HEAD)
                     == (lax.broadcasted_iota(jnp.int32, (LANES, LANES), 1) < RWKV_HEAD))
        for s in seqs:
            for j in range(N_RWKV_PAIRS):
                hi, mid, lo = _split3(sr_in[s, j])
                both = _mm(hi, dup) + _mm(mid, dup) + _mm(lo, dup)
                rst[s, j] = jnp.transpose(jnp.where(same_head, both, 0.0))

    x = jnp.concatenate([x_ref[s] for s in seqs], axis=0)
    hb = _bf(x * lax.rsqrt(jnp.mean(x * x, -1, keepdims=True) + RMS_EPS) * npre[...])

    lane = lax.broadcasted_iota(jnp.int32, (1, LANES), 1)
    head0 = lane < RWKV_HEAD
    tlane = lax.broadcasted_iota(jnp.int32, (1, L2), 1)
    tcol0 = tlane < L
    row = lax.broadcasted_iota(jnp.int32, (L, L2), 0)
    col = lax.broadcasted_iota(jnp.int32, (L, L2), 1) & (L - 1)
    eye = row == col
    strict_lower = row > col
    lower = row >= col
    crow = lax.broadcasted_iota(jnp.int32, (L, 1), 0)
    trow = lax.broadcasted_iota(jnp.int32, (TT, 1), 0)

    def level_mask(hsz, t_idx, s_idx):
        sh = hsz.bit_length()
        same = (t_idx >> sh) == (s_idx >> sh)
        return same & (((t_idx >> (sh - 1)) & 1) == 1) & (((s_idx >> (sh - 1)) & 1) == 0)

    lmask = [level_mask(hsz, row, col) for hsz in levels]
    src_t = lax.broadcasted_iota(jnp.int32, (L2, L), 0) & (L - 1)
    tok_t = lax.broadcasted_iota(jnp.int32, (L2, L), 1)
    eye_t = src_t == tok_t
    same_head_t = ((lax.broadcasted_iota(jnp.int32, (LANES, L2), 0) < RWKV_HEAD)
                   == (lax.broadcasted_iota(jnp.int32, (LANES, L2), 1) < L))
    lmask_t = [level_mask(hsz, tok_t, src_t) for hsz in levels]
    second_half = [((crow >> (hsz.bit_length() - 1)) & 1) == 1 for hsz in levels]

    def rows(c):
        return slice(c * L, (c + 1) * L)

    def pick(v, c, j):
        return v[rows(c), j * LANES:(j + 1) * LANES]

    def project(c0, c1):
        pbuf[:, c0:c1] = _mm(hb, win[:, c0:c1])

    def project_rwkv():
        p = _mm(hb, win[:, C_SH:])
        pbuf[:, C_GD:N_COLS] = p[:, 0:LANES]
        pbuf[:, C_SH:C_GD] = p[:, GLA_RANK:GLA_RANK + C_GD - C_SH]

    def stack(v):
        return _pair_rows(v, head0)

    def stack_t(v):
        return _pair_rows(v, tcol0)

    def segsum(v):
        vs = jnp.concatenate([v[:, i * LANES:(i + 1) * LANES] for i in range(4)], axis=0)
        s = _mm(jnp.concatenate(_split2(vs), axis=1), seg_ref[...])
        return jnp.concatenate([s[i * TT:(i + 1) * TT] for i in range(4)], axis=1)

    def chunk_cumsum(mat_ref, parts, c):
        return _mm(mat_ref[...], jnp.concatenate([p[rows(c)] for p in parts], axis=0))

    def last_row_bcast(v):
        return jnp.concatenate(
            [jnp.broadcast_to(v[(c + 1) * L - 1:(c + 1) * L], (L, v.shape[1])) for c in chunks], 0)

    def sigmoid_of_double(half_v):
        return 0.5 * jnp.tanh(half_v) + 0.5

    def silu(v):
        hv = 0.5 * v
        return hv * (jnp.tanh(hv) + 1.0)

    def rwkv_stream():
        project_rwkv()
        yield "proj"
        cur = pbuf[:, C_SH:C_SH + SHIFT_W]
        first = jnp.concatenate(
            [jnp.broadcast_to(ss_out[s], (seq_rows, SHIFT_W)) for s in seqs], axis=0)
        prev = jnp.where((trow & (seq_rows - 1)) == 0, first, pltpu.roll(cur, 1, 0))
        for s in seqs:
            ss_out[s] = cur[(s + 1) * seq_rows - 1:(s + 1) * seq_rows]
        yield None
        xm = cur + mu_ref[...] * (prev - cur)
        yield None
        r, kr, vr = xm[:, 0:W], xm[:, W:2 * W], xm[:, 2 * W:3 * W]
        wa = xm[:, 3 * W:3 * W + LANES]
        wa = jnp.where(head0, jnp.tanh(wa), wa)
        wa_pre = _mm(_bf(wa), waup[...])
        yield None
        log2_w = HALF_LOG2_DECAY * jnp.tanh(w0[...] + wa_pre[:, 0:W]) + HALF_LOG2_DECAY
        yield None
        a = sigmoid_of_double(a0[...] + wa_pre[:, W:2 * W])
        yield None
        kk = kr * kk_w[...]
        kr = kr * (1.0 + (a - 1.0) * ka_w[...])
        yield None
        kk = kk * jnp.minimum(lax.rsqrt(segsum(kk * kk)), 1.0 / KK_NORM_FLOOR)
        yield None
        parts = _split2(log2_w)
        cum = jnp.concatenate([chunk_cumsum(tri_ref, parts, c) for c in chunks], axis=0)
        cum_last = last_row_bcast(cum)
        yield None
        g_inv = jnp.exp2(-cum)
        g_last = jnp.exp2(cum_last)
        beta = a * kk
        yield None
        alpha_t = -kk * jnp.exp2(cum - log2_w)
        yield None
        r_t = r * jnp.exp2(cum)
        yield None
        beta_h, k_h = beta * g_inv, kr * g_inv
        yield None
        beta_r, k_r = beta_h * g_last, k_h * g_last
        yield "elem"

        items = [(c, j) for c in chunks for j in range(N_RWKV_PAIRS)]

        def picked(v):
            return [_bf(pick(v, *cj)) for cj in items]

        def stacked(v):
            return _each(lambda cj: stack(_bf(pick(v, *cj))), items)

        def stacked_t(v):
            def one(cj):
                v2 = pick(v, *cj)
                vt = _bf(jnp.transpose(jnp.concatenate([v2, v2], axis=0)))
                return jnp.where(same_head_t, vt, jnp.zeros((), BF16))

            return _each(one, items)

        rt = picked(r_t)
        al_rt = [jnp.concatenate([p, q], axis=0) for p, q in zip(picked(alpha_t), rt)]
        bh_t = yield from stacked_t(beta_h)
        ab = yield from _each(_mm, al_rt, bh_t)
        a_ab = [v[0:L] for v in ab]
        a_rb = [_bf(jnp.where(lower, v[L:L2], 0.0)) for v in ab]
        eyef = jnp.where(eye, 1.0, 0.0)
        d = yield from _each(lambda v: eyef + jnp.where(lmask[-1], v, 0.0), a_ab)
        kh_t = yield from stacked_t(k_h)
        ak = yield from _each(_mm, al_rt, kh_t)
        a_ak = [_bf(jnp.where(strict_lower, v[0:L], 0.0)) for v in ak]
        a_rk = [_bf(jnp.where(lower, v[L:L2], 0.0)) for v in ak]
        v_s = yield from stacked(vr)
        kr_t = yield from stacked_t(k_r)
        by_v = yield from _each(
            lambda p, q, t, v: _mm(jnp.concatenate([p, q, t], axis=0), v), a_ak, a_rk, kr_t, v_s)
        akv = [v[0:L] for v in by_v]
        y_kv = [v[L:L2] for v in by_v]
        s_kv = [v[L2:L2 + LANES] for v in by_v]
        for li in range(len(levels) - 2, -1, -1):
            msk = lmask[li]
            db = yield from _each(_bf, d)
            dm = yield from _each(
                lambda p, v: _bf(_mm(p, stack_t(_bf(jnp.where(msk, v, 0.0))))), db, a_ab)
            d = yield from _each(lambda v, p, q: v + _mm(p, stack_t(q)), d, dm, db)
        al_s = yield from stacked(alpha_t)
        wu = yield from _each(
            lambda t, p, q: _mm(_bf(t), jnp.concatenate([p, stack(_bf(q))], axis=1)), d, al_s, akv)
        w_t = [_bf(v[:, 0:LANES]) for v in wu]
        u_p = [v[:, LANES:2 * LANES] for v in wu]
        br_t = yield from stacked_t(beta_r)
        by_u = [jnp.concatenate([p, q], axis=0) for p, q in zip(br_t, a_rb)]
        g_col = yield from _each(
            lambda cj: jnp.transpose(jnp.broadcast_to(pick(g_last, *cj)[0:1], (LANES, LANES))), items)
        yield "mats"

        def rwkv_step(idx, states):
            stb = [_bf(s) for s in states]
            wr = [_mm(jnp.concatenate([w_t[i], rt[i]], axis=0), sb) for i, sb in zip(idx, stb)]
            yield None
            ub = [stack(_bf(v[0:L] + u_p[i])) for i, v in zip(idx, wr)]
            bu = [_mm(by_u[i], u) for i, u in zip(idx, ub)]
            new = [s * g_col[i] + v[0:LANES] + s_kv[i] for i, s, v in zip(idx, states, bu)]
            yield None
            ys = [v[L:L2] + t[LANES:LANES + L] + y_kv[i] for i, v, t in zip(idx, wr, bu)]
            yield None
            return ys, new

        pairs = [(s, j) for s in seqs for j in range(N_RWKV_PAIRS)]
        y_items = [None] * len(items)
        states = [rst[sj] for sj in pairs]
        for c in range(nc):
            idx = [(s * nc + c) * N_RWKV_PAIRS + j for s, j in pairs]
            outs, states = yield from rwkv_step(idx, states)
            for i, o in zip(idx, outs):
                y_items[i] = o
        for sj, st in zip(pairs, states):
            rst[sj] = st
        y = jnp.concatenate(
            [jnp.concatenate([y_items[c * N_RWKV_PAIRS + j] for j in range(N_RWKV_PAIRS)], axis=1)
             for c in chunks], axis=0)
        yield None
        mean = segsum(y) * (1.0 / RWKV_HEAD)
        yc = y - mean
        yield None
        var = segsum(yc * yc) * (1.0 / RWKV_HEAD)
        yield None
        y = yc * lax.rsqrt(var + RWKV_GN_EPS) * lnw[...] + lnb[...]
        yield None
        bonus = segsum(r * kr * rk_w[...]) * vr
        yield None
        zr = pbuf[:, C_ZR:C_ZR + W]
        mbuf[:, GLA_WIDTH:GLA_WIDTH + W] = _bf((y + bonus) * silu(zr))
        while True:
            yield "end"

    def gla_stream():
        each1 = functools.partial(_each, group=1)
        project(C_Q, C_V)
        yield None
        gl = _mm(_bf(pbuf[:, C_GD:C_GD + LANES]), aup[...]) + abias[...]
        log2_a = ((jnp.minimum(gl, 0.0) - jnp.log1p(jnp.exp(-jnp.abs(gl))))
                  * (LOG2E / GLA_TEMP))
        parts = _split2(log2_a)
        yield None
        project(C_V, C_ZG)
        yield None
        ex = []
        for c in chunks:
            ex.append(jnp.exp2(chunk_cumsum(e_ref, parts, c)))
            yield None
        project(C_ZG, C_SH)
        yield "decays"

        gitems = [(c, j) for c in chunks for j in range(N_GLA_PAIRS)]
        q = pbuf[:, C_Q:C_Q + GLA_KEY] * (GLA_DK ** -0.5)
        k = pbuf[:, C_K:C_K + GLA_KEY]
        qj = [pick(q, c, j) for c, j in gitems]
        kj = [pick(k, c, j) for c, j in gitems]

        def exl(blk, c, j):
            return ex[c][blk * L:(blk + 1) * L, j * LANES:(j + 1) * LANES]

        att = yield from each1(
            lambda p, t: jnp.where(eye_t, _nt(stack(_bf(t)), _bf(p)), 0.0), qj, kj)
        for li in range(len(levels)):
            sec, msk = second_half[li], lmask_t[li]

            def level_term(v, p, t, cj):
                xl = jnp.where(sec, p, t) * exl(2 + li, *cj)
                ql = _bf(jnp.where(sec, xl, 0.0))
                kl = stack(_bf(jnp.where(sec, 0.0, xl)))
                return v + jnp.where(msk, _nt(kl, ql), 0.0)

            att = yield from each1(level_term, att, qj, kj, gitems)
        yield "att"

        def value_rows(cj):
            c, j = cj
            v0 = C_V + 2 * j * GLA_DV
            return _bf(jnp.concatenate(
                [pbuf[rows(c), v0:v0 + GLA_DV], pbuf[rows(c), v0 + GLA_DV:v0 + 2 * GLA_DV]], 0))

        gv_s = yield from each1(value_rows, gitems)
        g_kv = yield from each1(
            lambda t, p, cj: _tn(stack(_bf(p * exl(1, *cj))), t), gv_s, kj, gitems)
        q_e = yield from each1(lambda p, cj: p * exl(0, *cj), qj, gitems)
        decay = yield from each1(
            lambda cj: jnp.transpose(jnp.broadcast_to(exl(0, *cj)[L - 1:L], (LANES, LANES))), gitems)

        g_start = [None] * len(gitems)
        for s in seqs:
            for j in range(N_GLA_PAIRS):
                st = sg_out[s, j]
                for c in range(nc):
                    i = (s * nc + c) * N_GLA_PAIRS + j
                    g_start[i] = st
                    st = st * decay[i] + g_kv[i]
                sg_out[s, j] = st
        yield None
        for i, (c, j) in enumerate(gitems):
            gsb = _bf(g_start[i])
            attb = _bf(att[i])
            for hh in range(2):
                mine_k = head0 if hh == 0 else jnp.logical_not(head0)
                o = (_tn(attb[hh * L:(hh + 1) * L], gv_s[i][hh * L:(hh + 1) * L])
                     + _mm(_bf(jnp.where(mine_k, q_e[i], 0.0)), gsb))
                o = o * lax.rsqrt(jnp.mean(o * o, -1, keepdims=True) + RMS_EPS) * gnorm[...]
                c0 = (2 * j + hh) * GLA_DV
                zg = pbuf[rows(c), C_ZG + c0:C_ZG + c0 + GLA_DV]
                mbuf[rows(c), c0:c0 + GLA_DV] = _bf(o * silu(zg))
            yield None
        while True:
            yield "end"

    rw, gl = rwkv_stream(), gla_stream()
    while next(rw) != "proj":
        pass
    _alternate(rw, gl, "elem", "decays", STAGE_STEPS[0])
    _alternate(rw, gl, "mats", "att", STAGE_STEPS[1])
    _alternate(rw, gl, "end", "end", STAGE_STEPS[2])

    o = _mm(mbuf[...], wout[...])
    o = o * lax.rsqrt(jnp.mean(o * o, -1, keepdims=True) + RMS_EPS) * npost[...]
    y = x + o
    for s in seqs:
        y_ref[s] = y[s * seq_rows:(s + 1) * seq_rows]

    @pl.when(pl.program_id(1) == last_tile)
    def _():
        r_i = lax.broadcasted_iota(jnp.int32, (LANES, RWKV_HEAD), 0)
        c_i = lax.broadcasted_iota(jnp.int32, (LANES, RWKV_HEAD), 1)
        fold = _bf(jnp.where((r_i & (RWKV_HEAD - 1)) == c_i, 1.0, 0.0))
        for s in seqs:
            for j in range(N_RWKV_PAIRS):
                hi, mid, lo = _split3(jnp.transpose(rst[s, j]))
                sr_out[s, j] = _mm(hi, fold) + _mm(mid, fold) + _mm(lo, fold)


def _layer_call(layer, state_layer, x3d, sg, sr, ss, acc, params, consts, *, seqs_per_tile,
                seq_rows, L):
    n_seq, n_tok, _ = x3d.shape
    tiles_per_group = n_tok // seq_rows
    tile_rows = seqs_per_tile * seq_rows

    x_spec = pl.BlockSpec((seqs_per_tile, seq_rows, D_MODEL), lambda g, t: (g, t, 0))

    def state_in_spec(arr):
        nd = arr.ndim
        return pl.BlockSpec((None, seqs_per_tile) + arr.shape[2:],
                            lambda g, t: (state_layer, g) + (0,) * (nd - 2))

    def state_out_spec(arr):
        nd = arr.ndim
        return pl.BlockSpec((None, seqs_per_tile) + arr.shape[2:],
                            lambda g, t: (layer, g) + (0,) * (nd - 2))

    def layer_spec(arr):
        return pl.BlockSpec((None,) + arr.shape[1:], lambda g, t: (layer,) + (0,) * (arr.ndim - 1))

    def const_spec(arr):
        return pl.BlockSpec(arr.shape, lambda g, t: (0,) * arr.ndim)

    states = (sg, sr, ss)
    in_specs = ([x_spec] + [state_in_spec(s) for s in states]
                + [layer_spec(p) for p in params] + [const_spec(c) for c in consts]
                + [pl.BlockSpec(memory_space=pl.ANY)] * len(acc))
    first_acc = len(in_specs) - len(acc)
    out_specs = [x_spec] + [state_out_spec(a) for a in acc]
    out_shape = ([jax.ShapeDtypeStruct(x3d.shape, F32)]
                 + [jax.ShapeDtypeStruct(a.shape, F32) for a in acc])
    return pl.pallas_call(
        functools.partial(_layer_kernel, L=L, last_tile=tiles_per_group - 1),
        grid=(n_seq // seqs_per_tile, tiles_per_group),
        in_specs=in_specs,
        out_specs=out_specs,
        out_shape=out_shape,
        input_output_aliases={first_acc + k: 1 + k for k in range(len(acc))},
        scratch_shapes=[pltpu.VMEM((tile_rows, N_COLS), F32), pltpu.VMEM((tile_rows, D_MODEL), BF16),
                        pltpu.VMEM((seqs_per_tile, N_RWKV_PAIRS, LANES, LANES), F32)],
        compiler_params=pltpu.CompilerParams(
            dimension_semantics=("arbitrary", "arbitrary"), vmem_limit_bytes=VMEM_LIMIT_BYTES),
    )(x3d, sg, sr, ss, *params, *consts, *acc)


def _gla_state_to_kernel(s):
    return s.reshape(s.shape[:-3] + (N_GLA_PAIRS, LANES, GLA_DV))


def _gla_state_from_kernel(s):
    return s.reshape(s.shape[:-3] + (GLA_HEADS, GLA_DK, GLA_DV))


def _row(p):
    return p.reshape(DEPTH, 1, -1)


def _constants(L):
    seg = np.kron(np.eye(2, dtype=np.float32), np.ones((RWKV_HEAD, RWKV_HEAD), np.float32))
    e_mat = _gla_exponent_matrix(L)
    tri = np.tril(np.ones((L, L), np.float32))
    return (jnp.asarray(np.tile(e_mat, (1, 2)), dtype=BF16),
            jnp.asarray(np.tile(tri, (1, 2)), dtype=BF16),
            jnp.asarray(np.concatenate([seg, seg], 0), dtype=BF16))


def _tiling(bp, seq, bs):
    prompt_seqs = 2 if bp % 2 == 0 else 1
    prompt_rows = PROMPT_STEP_ROWS // prompt_seqs if seq % PROMPT_STEP_ROWS == 0 else CHUNK
    sample_seqs = SAMPLE_SEQS_PER_STEP if bs % SAMPLE_SEQS_PER_STEP == 0 else 1
    return prompt_seqs, prompt_rows, sample_seqs


def kernel(x_prompt, x_sample, state_gla, state_rwkv, state_shift, norm_pre, norm_post, w_in,
           gla_a_up, gla_a_bias, gla_norm, rwkv_mu, rwkv_w_up, rwkv_w0, rwkv_a_up, rwkv_a0,
           rwkv_k_k, rwkv_k_a, rwkv_r_k, rwkv_ln_w, rwkv_ln_b, w_out):
    bp, seq, _ = x_prompt.shape
    bs, dec_seq, _ = x_sample.shape

    aup = jnp.pad(gla_a_up, ((0, 0), (0, LANES - GLA_RANK), (0, 0))).astype(BF16)
    zpad = jnp.zeros_like(rwkv_w_up)
    waup = (0.5 * jnp.concatenate([jnp.concatenate([rwkv_w_up, zpad], 1),
                                   jnp.concatenate([zpad, rwkv_a_up], 1)], 2)).astype(BF16)
    params = (_row(norm_pre), _row(norm_post), w_in.astype(BF16), aup, _row(gla_a_bias),
              _row(gla_norm), _row(rwkv_mu), waup, _row(0.5 * rwkv_w0), _row(0.5 * rwkv_a0), _row(rwkv_k_k),
              _row(rwkv_k_a), _row(rwkv_r_k), _row(rwkv_ln_w), _row(rwkv_ln_b), w_out.astype(BF16))

    prompt_seqs, prompt_rows, sample_seqs = _tiling(bp, seq, bs)
    consts_p = _constants(CHUNK)
    consts_s = _constants(dec_seq)

    yp, ys = x_prompt, x_sample
    zero_states = (jnp.zeros((1, bp, N_GLA_PAIRS, LANES, GLA_DV), F32),
                   jnp.zeros((1, bp) + RWKV_PAIR_STATE, F32),
                   jnp.zeros((1, bp, 1, SHIFT_W), F32))
    sample_states = (_gla_state_to_kernel(state_gla),
                     state_rwkv.reshape(state_rwkv.shape[:2] + RWKV_PAIR_STATE),
                     state_shift[:, :, None, :])
    acc_p = tuple(jnp.zeros((DEPTH,) + s.shape[1:], F32) for s in zero_states)
    acc_s = tuple(jnp.zeros(s.shape, F32) for s in sample_states)
    for l in range(DEPTH):
        yp, *acc_p = _layer_call(
            l, 0, yp, *zero_states, acc_p, params, consts_p, seqs_per_tile=prompt_seqs,
            seq_rows=prompt_rows, L=CHUNK)
        ys, *acc_s = _layer_call(
            l, l, ys, *sample_states, acc_s, params, consts_s, seqs_per_tile=sample_seqs,
            seq_rows=dec_seq, L=dec_seq)
    (gla_p, rwkv_p, shift_p), (gla_s, rwkv_s, shift_s) = acc_p, acc_s
    return (yp, ys,
            _gla_state_from_kernel(gla_p), _gla_state_from_kernel(gla_s),
            rwkv_p.reshape((DEPTH, bp, RWKV_HEADS, RWKV_HEAD, RWKV_HEAD)),
            rwkv_s.reshape((DEPTH, bs, RWKV_HEADS, RWKV_HEAD, RWKV_HEAD)),
            shift_p[:, :, 0], shift_s[:, :, 0])
```

```python
import functools
import math

import numpy as np
import jax
import jax.numpy as jnp
from jax import lax
from jax.experimental import pallas as pl
from jax.experimental.pallas import tpu as pltpu

F32 = jnp.float32
BF16 = jnp.bfloat16

D_MODEL = 1024
DEPTH = 4
CHUNK = 64
RMS_EPS = 1e-6
GLA_WIDTH = 512
GLA_HEADS = 4
GLA_DV = 128
GLA_DK = 64
GLA_KEY = 256
GLA_RANK = 16
GLA_TEMP = 16.0
RWKV_WIDTH = 512
RWKV_HEAD = 64
RWKV_HEADS = 8
RWKV_RANK = 64
SHIFT_W = 3 * RWKV_WIDTH + 2 * RWKV_RANK
RWKV_GN_EPS = 64e-5
RWKV_DECAY_SCALE = 0.606531
LOG2E = math.log2(math.e)

LANES = 128
N_GLA_PAIRS = GLA_HEADS // 2
N_RWKV_PAIRS = RWKV_HEADS // 2
RWKV_PAIR_STATE = (N_RWKV_PAIRS, 2 * RWKV_HEAD, RWKV_HEAD)

C_Q = 0
C_K = C_Q + GLA_KEY
C_V = C_K + GLA_KEY
C_ZG = C_V + GLA_WIDTH
C_SH = C_ZG + GLA_WIDTH
C_ZR = C_SH + SHIFT_W
C_GD = C_ZR + RWKV_WIDTH
N_COLS = C_GD + LANES

VMEM_LIMIT_BYTES = 56 * 1024 * 1024
PROMPT_STEP_ROWS = 512
SAMPLE_SEQS_PER_STEP = 8
ITEMS_PER_STEP = 4
STAGE_STEPS = ((2, 1), (2, 1), (1, 2))
KK_NORM_FLOOR = 1e-12
HALF_LOG2_DECAY = -0.5 * RWKV_DECAY_SCALE * LOG2E


def _nt(a, b):
    return lax.dot_general(a, b, (((1,), (1,)), ((), ())), preferred_element_type=F32)


def _tn(a, b):
    return lax.dot_general(a, b, (((0,), (0,)), ((), ())), preferred_element_type=F32)


def _mm(a, b):
    return jnp.dot(a, b, preferred_element_type=F32)


def _bf(v):
    return v.astype(BF16)


def _split3(v):
    hi = _bf(v)
    r1 = v - hi.astype(F32)
    mid = _bf(r1)
    return hi, mid, _bf(r1 - mid.astype(F32))


def _split2(v):
    hi = _bf(v)
    return hi, _bf(v - hi.astype(F32))


def _each(fn, *lists, group=ITEMS_PER_STEP):
    out = []
    for i, args in enumerate(zip(*lists)):
        out.append(fn(*args))
        if i % group == group - 1:
            yield None
    return out


def _alternate(ga, gb, stop_a, stop_b, ratio):
    done_a = done_b = False
    while not (done_a and done_b):
        for _ in range(ratio[0]):
            if not done_a:
                done_a = next(ga) == stop_a
        for _ in range(ratio[1]):
            if not done_b:
                done_b = next(gb) == stop_b


def _pair_rows(v, first=None):
    if first is None:
        first = lax.broadcasted_iota(jnp.int32, (1, LANES), 1) < RWKV_HEAD
    zero = jnp.zeros((), v.dtype)
    return jnp.concatenate([jnp.where(first, v, zero), jnp.where(first, zero, v)], axis=0)


def _levels(L):
    out, h = [], L // 2
    while h >= 1:
        out.append(h)
        h //= 2
    return out


def _gla_exponent_matrix(L):
    t = np.arange(L)[:, None]
    i = np.arange(L)[None, :]
    blocks = [i <= t, i > t]
    for h in _levels(L):
        mid = (t // (2 * h)) * (2 * h) + h
        second = t >= mid
        blocks.append(np.where(second, (i >= mid) & (i <= t), (i > t) & (i < mid)))
    return np.concatenate(blocks, 0).astype(np.float32)


def _layer_kernel(x_ref, sg_in, sr_in, ss_in, npre, npost, win, aup, abias,
                  gnorm, mu_ref, waup, w0, a0, kk_w, ka_w, rk_w, lnw, lnb, wout, e_ref, tri_ref, seg_ref,
                  acc_g, acc_r, acc_s, y_ref, sg_out, sr_out, ss_out, pbuf, mbuf, rst, *, L, last_tile):
    del acc_g, acc_r, acc_s
    nb, seq_rows, _ = x_ref.shape
    nc = seq_rows // L
    TT = nb * seq_rows
    L2 = 2 * L
    W = RWKV_WIDTH
    levels = _levels(L)
    seqs = range(nb)
    chunks = range(nb * nc)

    @pl.when(pl.program_id(1) == 0)
    def _():
        sg_out[...] = sg_in[...]
        ss_out[...] = ss_in[...]
        dup = _bf(jnp.where(
            lax.broadcasted_iota(jnp.int32, (RWKV_HEAD, LANES), 0)
            == (lax.broadcasted_iota(jnp.int32, (RWKV_HEAD, LANES), 1) & (RWKV_HEAD - 1)), 1.0, 0.0))
        same_head = ((lax.broadcasted_iota(jnp.int32, (LANES, LANES), 0) < RWKV_HEAD)
                     == (lax.broadcasted_iota(jnp.int32, (LANES, LANES), 1) < RWKV_HEAD))
        for s in seqs:
            for j in range(N_RWKV_PAIRS):
                hi, mid, lo = _split3(sr_in[s, j])
                both = _mm(hi, dup) + _mm(mid, dup) + _mm(lo, dup)
                rst[s, j] = jnp.transpose(jnp.where(same_head, both, 0.0))

    x = jnp.concatenate([x_ref[s] for s in seqs], axis=0)
    hb = _bf(x * lax.rsqrt(jnp.mean(x * x, -1, keepdims=True) + RMS_EPS) * npre[...])

    lane = lax.broadcasted_iota(jnp.int32, (1, LANES), 1)
    head0 = lane < RWKV_HEAD
    tlane = lax.broadcasted_iota(jnp.int32, (1, L2), 1)
    tcol0 = tlane < L
    row = lax.broadcasted_iota(jnp.int32, (L, L2), 0)
    col = lax.broadcasted_iota(jnp.int32, (L, L2), 1) & (L - 1)
    eye = row == col
    strict_lower = row > col
    lower = row >= col
    crow = lax.broadcasted_iota(jnp.int32, (L, 1), 0)
    trow = lax.broadcasted_iota(jnp.int32, (TT, 1), 0)

    def level_mask(hsz, t_idx, s_idx):
        sh = hsz.bit_length()
        same = (t_idx >> sh) == (s_idx >> sh)
        return same & (((t_idx >> (sh - 1)) & 1) == 1) & (((s_idx >> (sh - 1)) & 1) == 0)

    lmask = [level_mask(hsz, row, col) for hsz in levels]
    src_t = lax.broadcasted_iota(jnp.int32, (L2, L), 0) & (L - 1)
    tok_t = lax.broadcasted_iota(jnp.int32, (L2, L), 1)
    eye_t = src_t == tok_t
    same_head_t = ((lax.broadcasted_iota(jnp.int32, (LANES, L2), 0) < RWKV_HEAD)
                   == (lax.broadcasted_iota(jnp.int32, (LANES, L2), 1) < L))
    lmask_t = [level_mask(hsz, tok_t, src_t) for hsz in levels]
    second_half = [((crow >> (hsz.bit_length() - 1)) & 1) == 1 for hsz in levels]

    def rows(c):
        return slice(c * L, (c + 1) * L)

    def pick(v, c, j):
        return v[rows(c), j * LANES:(j + 1) * LANES]

    def project(c0, c1):
        pbuf[:, c0:c1] = _mm(hb, win[:, c0:c1])

    def project_rwkv():
        p = _mm(hb, win[:, C_SH:])
        pbuf[:, C_GD:N_COLS] = p[:, 0:LANES]
        pbuf[:, C_SH:C_GD] = p[:, GLA_RANK:GLA_RANK + C_GD - C_SH]

    def stack(v):
        return _pair_rows(v, head0)

    def stack_t(v):
        return _pair_rows(v, tcol0)

    def segsum(v):
        vs = jnp.concatenate([v[:, i * LANES:(i + 1) * LANES] for i in range(4)], axis=0)
        s = _mm(_bf(vs), seg_ref[0:LANES])
        return jnp.concatenate([s[i * TT:(i + 1) * TT] for i in range(4)], axis=1)

    def chunk_cumsum(mat_ref, parts, c):
        return _mm(mat_ref[...], jnp.concatenate([p[rows(c)] for p in parts], axis=0))

    def last_row_bcast(v):
        return jnp.concatenate(
            [jnp.broadcast_to(v[(c + 1) * L - 1:(c + 1) * L], (L, v.shape[1])) for c in chunks], 0)

    def sigmoid_of_double(half_v):
        return 0.5 * jnp.tanh(half_v) + 0.5

    def silu(v):
        hv = 0.5 * v
        return hv * (jnp.tanh(hv) + 1.0)

    def rwkv_stream():
        project_rwkv()
        yield "proj"
        cur = pbuf[:, C_SH:C_SH + SHIFT_W]
        first = jnp.concatenate(
            [jnp.broadcast_to(ss_out[s], (seq_rows, SHIFT_W)) for s in seqs], axis=0)
        prev = jnp.where((trow & (seq_rows - 1)) == 0, first, pltpu.roll(cur, 1, 0))
        for s in seqs:
            ss_out[s] = cur[(s + 1) * seq_rows - 1:(s + 1) * seq_rows]
        yield None
        xm = cur + mu_ref[...] * (prev - cur)
        yield None
        r, kr, vr = xm[:, 0:W], xm[:, W:2 * W], xm[:, 2 * W:3 * W]
        wa = xm[:, 3 * W:3 * W + LANES]
        wa = jnp.where(head0, jnp.tanh(wa), wa)
        wa_pre = _mm(_bf(wa), waup[...])
        yield None
        log2_w = HALF_LOG2_DECAY * jnp.tanh(w0[...] + wa_pre[:, 0:W]) + HALF_LOG2_DECAY
        yield None
        a = sigmoid_of_double(a0[...] + wa_pre[:, W:2 * W])
        yield None
        kk = kr * kk_w[...]
        kr = kr * (1.0 + (a - 1.0) * ka_w[...])
        yield None
        kk = kk * jnp.minimum(lax.rsqrt(segsum(kk * kk)), 1.0 / KK_NORM_FLOOR)
        yield None
        parts = _split2(log2_w)
        cum = jnp.concatenate([chunk_cumsum(tri_ref, parts, c) for c in chunks], axis=0)
        cum_last = last_row_bcast(cum)
        yield None
        g_inv = jnp.exp2(-cum)
        g_last = jnp.exp2(cum_last)
        beta = a * kk
        yield None
        alpha_t = -kk * jnp.exp2(cum - log2_w)
        yield None
        r_t = r * jnp.exp2(cum)
        yield None
        beta_h, k_h = beta * g_inv, kr * g_inv
        yield None
        beta_r, k_r = beta_h * g_last, k_h * g_last
        yield "elem"

        items = [(c, j) for c in chunks for j in range(N_RWKV_PAIRS)]

        def picked(v):
            return [_bf(pick(v, *cj)) for cj in items]

        def stacked(v):
            return _each(lambda cj: stack(_bf(pick(v, *cj))), items)

        def stacked_t(v):
            def one(cj):
                v2 = pick(v, *cj)
                vt = _bf(jnp.transpose(jnp.concatenate([v2, v2], axis=0)))
                return jnp.where(same_head_t, vt, jnp.zeros((), BF16))

            return _each(one, items)

        rt = picked(r_t)
        al_rt = [jnp.concatenate([p, q], axis=0) for p, q in zip(picked(alpha_t), rt)]
        bh_t = yield from stacked_t(beta_h)
        ab = yield from _each(_mm, al_rt, bh_t)
        a_ab = [v[0:L] for v in ab]
        a_rb = [_bf(jnp.where(lower, v[L:L2], 0.0)) for v in ab]
        eyef = jnp.where(eye, 1.0, 0.0)
        d = yield from _each(lambda v: eyef + jnp.where(lmask[-1], v, 0.0), a_ab)
        kh_t = yield from stacked_t(k_h)
        ak = yield from _each(_mm, al_rt, kh_t)
        a_ak = [_bf(jnp.where(strict_lower, v[0:L], 0.0)) for v in ak]
        a_rk = [_bf(jnp.where(lower, v[L:L2], 0.0)) for v in ak]
        v_s = yield from stacked(vr)
        kr_t = yield from stacked_t(k_r)
        by_v = yield from _each(
            lambda p, q, t, v: _mm(jnp.concatenate([p, q, t], axis=0), v), a_ak, a_rk, kr_t, v_s)
        akv = [v[0:L] for v in by_v]
        y_kv = [v[L:L2] for v in by_v]
        s_kv = [v[L2:L2 + LANES] for v in by_v]
        for li in range(len(levels) - 2, -1, -1):
            msk = lmask[li]
            db = yield from _each(_bf, d)
            dm = yield from _each(
                lambda p, v: _bf(_mm(p, stack_t(_bf(jnp.where(msk, v, 0.0))))), db, a_ab)
            d = yield from _each(lambda v, p, q: v + _mm(p, stack_t(q)), d, dm, db)
        al_s = yield from stacked(alpha_t)
        wu = yield from _each(
            lambda t, p, q: _mm(_bf(t), jnp.concatenate([p, stack(_bf(q))], axis=1)), d, al_s, akv)
        w_t = [_bf(v[:, 0:LANES]) for v in wu]
        u_p = [v[:, LANES:2 * LANES] for v in wu]
        br_t = yield from stacked_t(beta_r)
        by_u = [jnp.concatenate([p, q], axis=0) for p, q in zip(br_t, a_rb)]
        g_col = yield from _each(
            lambda cj: jnp.transpose(jnp.broadcast_to(pick(g_last, *cj)[0:1], (LANES, LANES))), items)
        yield "mats"

        def rwkv_step(idx, states):
            stb = [_bf(s) for s in states]
            wr = [_mm(jnp.concatenate([w_t[i], rt[i]], axis=0), sb) for i, sb in zip(idx, stb)]
            yield None
            ub = [stack(_bf(v[0:L] + u_p[i])) for i, v in zip(idx, wr)]
            bu = [_mm(by_u[i], u) for i, u in zip(idx, ub)]
            new = [s * g_col[i] + v[0:LANES] + s_kv[i] for i, s, v in zip(idx, states, bu)]
            yield None
            ys = [v[L:L2] + t[LANES:LANES + L] + y_kv[i] for i, v, t in zip(idx, wr, bu)]
            yield None
            return ys, new

        pairs = [(s, j) for s in seqs for j in range(N_RWKV_PAIRS)]
        y_items = [None] * len(items)
        states = [rst[sj] for sj in pairs]
        for c in range(nc):
            idx = [(s * nc + c) * N_RWKV_PAIRS + j for s, j in pairs]
            outs, states = yield from rwkv_step(idx, states)
            for i, o in zip(idx, outs):
                y_items[i] = o
        for sj, st in zip(pairs, states):
            rst[sj] = st
        y = jnp.concatenate(
            [jnp.concatenate([y_items[c * N_RWKV_PAIRS + j] for j in range(N_RWKV_PAIRS)], axis=1)
             for c in chunks], axis=0)
        yield None
        mean = segsum(y) * (1.0 / RWKV_HEAD)
        yc = y - mean
        yield None
        var = segsum(yc * yc) * (1.0 / RWKV_HEAD)
        yield None
        y = yc * lax.rsqrt(var + RWKV_GN_EPS) * lnw[...] + lnb[...]
        yield None
        bonus = segsum(r * kr * rk_w[...]) * vr
        yield None
        zr = pbuf[:, C_ZR:C_ZR + W]
        mbuf[:, GLA_WIDTH:GLA_WIDTH + W] = _bf((y + bonus) * silu(zr))
        while True:
            yield "end"

    def gla_stream():
        each1 = functools.partial(_each, group=1)
        project(C_Q, C_V)
        yield None
        gl = _mm(_bf(pbuf[:, C_GD:C_GD + LANES]), aup[...]) + abias[...]
        log2_a = ((jnp.minimum(gl, 0.0) - jnp.log1p(jnp.exp(-jnp.abs(gl))))
                  * (LOG2E / GLA_TEMP))
        parts = _split2(log2_a)
        yield None
        project(C_V, C_ZG)
        yield None
        ex = []
        for c in chunks:
            ex.append(jnp.exp2(chunk_cumsum(e_ref, parts, c)))
            yield None
        project(C_ZG, C_SH)
        yield "decays"

        gitems = [(c, j) for c in chunks for j in range(N_GLA_PAIRS)]
        q = pbuf[:, C_Q:C_Q + GLA_KEY] * (GLA_DK ** -0.5)
        k = pbuf[:, C_K:C_K + GLA_KEY]
        qj = [pick(q, c, j) for c, j in gitems]
        kj = [pick(k, c, j) for c, j in gitems]

        def exl(blk, c, j):
            return ex[c][blk * L:(blk + 1) * L, j * LANES:(j + 1) * LANES]

        att = yield from each1(
            lambda p, t: jnp.where(eye_t, _nt(stack(_bf(t)), _bf(p)), 0.0), qj, kj)
        for li in range(len(levels)):
            sec, msk = second_half[li], lmask_t[li]

            def level_term(v, p, t, cj):
                xl = jnp.where(sec, p, t) * exl(2 + li, *cj)
                ql = _bf(jnp.where(sec, xl, 0.0))
                kl = stack(_bf(jnp.where(sec, 0.0, xl)))
                return v + jnp.where(msk, _nt(kl, ql), 0.0)

            att = yield from each1(level_term, att, qj, kj, gitems)
        yield "att"

        def value_rows(cj):
            c, j = cj
            v0 = C_V + 2 * j * GLA_DV
            return _bf(jnp.concatenate(
                [pbuf[rows(c), v0:v0 + GLA_DV], pbuf[rows(c), v0 + GLA_DV:v0 + 2 * GLA_DV]], 0))

        gv_s = yield from each1(value_rows, gitems)
        g_kv = yield from each1(
            lambda t, p, cj: _tn(stack(_bf(p * exl(1, *cj))), t), gv_s, kj, gitems)
        q_e = yield from each1(lambda p, cj: p * exl(0, *cj), qj, gitems)
        decay = yield from each1(
            lambda cj: jnp.transpose(jnp.broadcast_to(exl(0, *cj)[L - 1:L], (LANES, LANES))), gitems)

        g_start = [None] * len(gitems)
        for s in seqs:
            for j in range(N_GLA_PAIRS):
                st = sg_out[s, j]
                for c in range(nc):
                    i = (s * nc + c) * N_GLA_PAIRS + j
                    g_start[i] = st
                    st = st * decay[i] + g_kv[i]
                sg_out[s, j] = st
        yield None
        for i, (c, j) in enumerate(gitems):
            gsb = _bf(g_start[i])
            attb = _bf(att[i])
            for hh in range(2):
                mine_k = head0 if hh == 0 else jnp.logical_not(head0)
                o = (_tn(attb[hh * L:(hh + 1) * L], gv_s[i][hh * L:(hh + 1) * L])
                     + _mm(_bf(jnp.where(mine_k, q_e[i], 0.0)), gsb))
                o = o * lax.rsqrt(jnp.mean(o * o, -1, keepdims=True) + RMS_EPS) * gnorm[...]
                c0 = (2 * j + hh) * GLA_DV
                zg = pbuf[rows(c), C_ZG + c0:C_ZG + c0 + GLA_DV]
                mbuf[rows(c), c0:c0 + GLA_DV] = _bf(o * silu(zg))
            yield None
        while True:
            yield "end"

    rw, gl = rwkv_stream(), gla_stream()
    while next(rw) != "proj":
        pass
    _alternate(rw, gl, "elem", "decays", STAGE_STEPS[0])
    _alternate(rw, gl, "mats", "att", STAGE_STEPS[1])
    _alternate(rw, gl, "end", "end", STAGE_STEPS[2])

    o = _mm(mbuf[...], wout[...])
    o = o * lax.rsqrt(jnp.mean(o * o, -1, keepdims=True) + RMS_EPS) * npost[...]
    y = x + o
    for s in seqs:
        y_ref[s] = y[s * seq_rows:(s + 1) * seq_rows]

    @pl.when(pl.program_id(1) == last_tile)
    def _():
        r_i = lax.broadcasted_iota(jnp.int32, (LANES, RWKV_HEAD), 0)
        c_i = lax.broadcasted_iota(jnp.int32, (LANES, RWKV_HEAD), 1)
        fold = _bf(jnp.where((r_i & (RWKV_HEAD - 1)) == c_i, 1.0, 0.0))
        for s in seqs:
            for j in range(N_RWKV_PAIRS):
                hi, mid, lo = _split3(jnp.transpose(rst[s, j]))
                sr_out[s, j] = _mm(hi, fold) + _mm(mid, fold) + _mm(lo, fold)


def _layer_call(layer, state_layer, x3d, sg, sr, ss, acc, params, consts, *, seqs_per_tile,
                seq_rows, L):
    n_seq, n_tok, _ = x3d.shape
    tiles_per_group = n_tok // seq_rows
    tile_rows = seqs_per_tile * seq_rows

    x_spec = pl.BlockSpec((seqs_per_tile, seq_rows, D_MODEL), lambda g, t: (g, t, 0))

    def state_in_spec(arr):
        nd = arr.ndim
        return pl.BlockSpec((None, seqs_per_tile) + arr.shape[2:],
                            lambda g, t: (state_layer, g) + (0,) * (nd - 2))

    def state_out_spec(arr):
        nd = arr.ndim
        return pl.BlockSpec((None, seqs_per_tile) + arr.shape[2:],
                            lambda g, t: (layer, g) + (0,) * (nd - 2))

    def layer_spec(arr):
        return pl.BlockSpec((None,) + arr.shape[1:], lambda g, t: (layer,) + (0,) * (arr.ndim - 1))

    def const_spec(arr):
        return pl.BlockSpec(arr.shape, lambda g, t: (0,) * arr.ndim)

    states = (sg, sr, ss)
    in_specs = ([x_spec] + [state_in_spec(s) for s in states]
                + [layer_spec(p) for p in params] + [const_spec(c) for c in consts]
                + [pl.BlockSpec(memory_space=pl.ANY)] * len(acc))
    first_acc = len(in_specs) - len(acc)
    out_specs = [x_spec] + [state_out_spec(a) for a in acc]
    out_shape = ([jax.ShapeDtypeStruct(x3d.shape, F32)]
                 + [jax.ShapeDtypeStruct(a.shape, F32) for a in acc])
    return pl.pallas_call(
        functools.partial(_layer_kernel, L=L, last_tile=tiles_per_group - 1),
        grid=(n_seq // seqs_per_tile, tiles_per_group),
        in_specs=in_specs,
        out_specs=out_specs,
        out_shape=out_shape,
        input_output_aliases={first_acc + k: 1 + k for k in range(len(acc))},
        scratch_shapes=[pltpu.VMEM((tile_rows, N_COLS), F32), pltpu.VMEM((tile_rows, D_MODEL), BF16),
                        pltpu.VMEM((seqs_per_tile, N_RWKV_PAIRS, LANES, LANES), F32)],
        compiler_params=pltpu.CompilerParams(
            dimension_semantics=("arbitrary", "arbitrary"), vmem_limit_bytes=VMEM_LIMIT_BYTES),
    )(x3d, sg, sr, ss, *params, *consts, *acc)


def _gla_state_to_kernel(s):
    return s.reshape(s.shape[:-3] + (N_GLA_PAIRS, LANES, GLA_DV))


def _gla_state_from_kernel(s):
    return s.reshape(s.shape[:-3] + (GLA_HEADS, GLA_DK, GLA_DV))


def _row(p):
    return p.reshape(DEPTH, 1, -1)


def _constants(L):
    seg = np.kron(np.eye(2, dtype=np.float32), np.ones((RWKV_HEAD, RWKV_HEAD), np.float32))
    e_mat = _gla_exponent_matrix(L)
    tri = np.tril(np.ones((L, L), np.float32))
    return (jnp.asarray(np.tile(e_mat, (1, 2)), dtype=BF16),
            jnp.asarray(np.tile(tri, (1, 2)), dtype=BF16),
            jnp.asarray(np.concatenate([seg, seg], 0), dtype=BF16))


def _tiling(bp, seq, bs):
    prompt_seqs = 2 if bp % 2 == 0 else 1
    prompt_rows = PROMPT_STEP_ROWS // prompt_seqs if seq % PROMPT_STEP_ROWS == 0 else CHUNK
    sample_seqs = SAMPLE_SEQS_PER_STEP if bs % SAMPLE_SEQS_PER_STEP == 0 else 1
    return prompt_seqs, prompt_rows, sample_seqs


def kernel(x_prompt, x_sample, state_gla, state_rwkv, state_shift, norm_pre, norm_post, w_in,
           gla_a_up, gla_a_bias, gla_norm, rwkv_mu, rwkv_w_up, rwkv_w0, rwkv_a_up, rwkv_a0,
           rwkv_k_k, rwkv_k_a, rwkv_r_k, rwkv_ln_w, rwkv_ln_b, w_out):
    bp, seq, _ = x_prompt.shape
    bs, dec_seq, _ = x_sample.shape

    aup = jnp.pad(gla_a_up, ((0, 0), (0, LANES - GLA_RANK), (0, 0))).astype(BF16)
    zpad = jnp.zeros_like(rwkv_w_up)
    waup = (0.5 * jnp.concatenate([jnp.concatenate([rwkv_w_up, zpad], 1),
                                   jnp.concatenate([zpad, rwkv_a_up], 1)], 2)).astype(BF16)
    params = (_row(norm_pre), _row(norm_post), w_in.astype(BF16), aup, _row(gla_a_bias),
              _row(gla_norm), _row(rwkv_mu), waup, _row(0.5 * rwkv_w0), _row(0.5 * rwkv_a0), _row(rwkv_k_k),
              _row(rwkv_k_a), _row(rwkv_r_k), _row(rwkv_ln_w), _row(rwkv_ln_b), w_out.astype(BF16))

    prompt_seqs, prompt_rows, sample_seqs = _tiling(bp, seq, bs)
    consts_p = _constants(CHUNK)
    consts_s = _constants(dec_seq)

    yp, ys = x_prompt, x_sample
    zero_states = (jnp.zeros((1, bp, N_GLA_PAIRS, LANES, GLA_DV), F32),
                   jnp.zeros((1, bp) + RWKV_PAIR_STATE, F32),
                   jnp.zeros((1, bp, 1, SHIFT_W), F32))
    sample_states = (_gla_state_to_kernel(state_gla),
                     state_rwkv.reshape(state_rwkv.shape[:2] + RWKV_PAIR_STATE),
                     state_shift[:, :, None, :])
    acc_p = tuple(jnp.zeros((DEPTH,) + s.shape[1:], F32) for s in zero_states)
    acc_s = tuple(jnp.zeros(s.shape, F32) for s in sample_states)
    for l in range(DEPTH):
        yp, *acc_p = _layer_call(
            l, 0, yp, *zero_states, acc_p, params, consts_p, seqs_per_tile=prompt_seqs,
            seq_rows=prompt_rows, L=CHUNK)
        ys, *acc_s = _layer_call(
            l, l, ys, *sample_states, acc_s, params, consts_s, seqs_per_tile=sample_seqs,
            seq_rows=dec_seq, L=dec_seq)
    (gla_p, rwkv_p, shift_p), (gla_s, rwkv_s, shift_s) = acc_p, acc_s
    return (yp, ys,
            _gla_state_from_kernel(gla_p), _gla_state_from_kernel(gla_s),
            rwkv_p.reshape((DEPTH, bp, RWKV_HEADS, RWKV_HEAD, RWKV_HEAD)),
            rwkv_s.reshape((DEPTH, bs, RWKV_HEADS, RWKV_HEAD, RWKV_HEAD)),
            shift_p[:, :, 0], shift_s[:, :, 0])
```

```python
import functools
import math

import numpy as np
import jax
import jax.numpy as jnp
from jax import lax
from jax.experimental import pallas as pl
from jax.experimental.pallas import tpu as pltpu

F32 = jnp.float32
BF16 = jnp.bfloat16

D_MODEL = 1024
DEPTH = 4
CHUNK = 64
RMS_EPS = 1e-6
GLA_WIDTH = 512
GLA_HEADS = 4
GLA_DV = 128
GLA_DK = 64
GLA_KEY = 256
GLA_RANK = 16
GLA_TEMP = 16.0
RWKV_WIDTH = 512
RWKV_HEAD = 64
RWKV_HEADS = 8
RWKV_RANK = 64
SHIFT_W = 3 * RWKV_WIDTH + 2 * RWKV_RANK
RWKV_GN_EPS = 64e-5
RWKV_DECAY_SCALE = 0.606531
LOG2E = math.log2(math.e)

LANES = 128
N_GLA_PAIRS = GLA_HEADS // 2
N_RWKV_PAIRS = RWKV_HEADS // 2
RWKV_PAIR_STATE = (N_RWKV_PAIRS, 2 * RWKV_HEAD, RWKV_HEAD)

C_Q = 0
C_K = C_Q + GLA_KEY
C_V = C_K + GLA_KEY
C_ZG = C_V + GLA_WIDTH
C_SH = C_ZG + GLA_WIDTH
C_ZR = C_SH + SHIFT_W
C_GD = C_ZR + RWKV_WIDTH
N_COLS = C_GD + LANES

VMEM_LIMIT_BYTES = 56 * 1024 * 1024
PROMPT_STEP_ROWS = 512
SAMPLE_SEQS_PER_STEP = 8
ITEMS_PER_STEP = 4
STAGE_STEPS = ((2, 1), (2, 1), (1, 2))
KK_NORM_FLOOR = 1e-12
HALF_LOG2_DECAY = -0.5 * RWKV_DECAY_SCALE * LOG2E


def _nt(a, b):
    return lax.dot_general(a, b, (((1,), (1,)), ((), ())), preferred_element_type=F32)


def _tn(a, b):
    return lax.dot_general(a, b, (((0,), (0,)), ((), ())), preferred_element_type=F32)


def _mm(a, b):
    return jnp.dot(a, b, preferred_element_type=F32)


def _bf(v):
    return v.astype(BF16)


def _split3(v):
    hi = _bf(v)
    r1 = v - hi.astype(F32)
    mid = _bf(r1)
    return hi, mid, _bf(r1 - mid.astype(F32))


def _split2(v):
    hi = _bf(v)
    return hi, _bf(v - hi.astype(F32))


def _each(fn, *lists, group=ITEMS_PER_STEP):
    out = []
    for i, args in enumerate(zip(*lists)):
        out.append(fn(*args))
        if i % group == group - 1:
            yield None
    return out


def _alternate(ga, gb, stop_a, stop_b, ratio):
    done_a = done_b = False
    while not (done_a and done_b):
        for _ in range(ratio[0]):
            if not done_a:
                done_a = next(ga) == stop_a
        for _ in range(ratio[1]):
            if not done_b:
                done_b = next(gb) == stop_b


def _pair_rows(v, first=None):
    if first is None:
        first = lax.broadcasted_iota(jnp.int32, (1, LANES), 1) < RWKV_HEAD
    zero = jnp.zeros((), v.dtype)
    return jnp.concatenate([jnp.where(first, v, zero), jnp.where(first, zero, v)], axis=0)


def _levels(L):
    out, h = [], L // 2
    while h >= 1:
        out.append(h)
        h //= 2
    return out


def _gla_exponent_matrix(L):
    t = np.arange(L)[:, None]
    i = np.arange(L)[None, :]
    blocks = [i <= t, i > t]
    for h in _levels(L):
        mid = (t // (2 * h)) * (2 * h) + h
        second = t >= mid
        blocks.append(np.where(second, (i >= mid) & (i <= t), (i > t) & (i < mid)))
    return np.concatenate(blocks, 0).astype(np.float32)


def _layer_kernel(x_ref, sg_in, sr_in, ss_in, npre, npost, win, aup, abias,
                  gnorm, mu_ref, waup, w0, a0, kk_w, ka_w, rk_w, lnw, lnb, wout, e_ref, tri_ref, seg_ref,
                  acc_g, acc_r, acc_s, y_ref, sg_out, sr_out, ss_out, pbuf, mbuf, rst, *, L, last_tile):
    del acc_g, acc_r, acc_s
    nb, seq_rows, _ = x_ref.shape
    nc = seq_rows // L
    TT = nb * seq_rows
    L2 = 2 * L
    W = RWKV_WIDTH
    levels = _levels(L)
    seqs = range(nb)
    chunks = range(nb * nc)

    @pl.when(pl.program_id(1) == 0)
    def _():
        sg_out[...] = sg_in[...]
        ss_out[...] = ss_in[...]
        dup = _bf(jnp.where(
            lax.broadcasted_iota(jnp.int32, (RWKV_HEAD, LANES), 0)
            == (lax.broadcasted_iota(jnp.int32, (RWKV_HEAD, LANES), 1) & (RWKV_HEAD - 1)), 1.0, 0.0))
        same_head = ((lax.broadcasted_iota(jnp.int32, (LANES, LANES), 0) < RWKV_HEAD)
                     == (lax.broadcasted_iota(jnp.int32, (LANES, LANES), 1) < RWKV_HEAD))
        for s in seqs:
            for j in range(N_RWKV_PAIRS):
                hi, mid, lo = _split3(sr_in[s, j])
                both = _mm(hi, dup) + _mm(mid, dup) + _mm(lo, dup)
                rst[s, j] = jnp.transpose(jnp.where(same_head, both, 0.0))

    x = jnp.concatenate([x_ref[s] for s in seqs], axis=0)
    hb = _bf(x * lax.rsqrt(jnp.mean(x * x, -1, keepdims=True) + RMS_EPS) * npre[...])

    lane = lax.broadcasted_iota(jnp.int32, (1, LANES), 1)
    head0 = lane < RWKV_HEAD
    tlane = lax.broadcasted_iota(jnp.int32, (1, L2), 1)
    tcol0 = tlane < L
    row = lax.broadcasted_iota(jnp.int32, (L, L2), 0)
    col = lax.broadcasted_iota(jnp.int32, (L, L2), 1) & (L - 1)
    eye = row == col
    strict_lower = row > col
    lower = row >= col
    crow = lax.broadcasted_iota(jnp.int32, (L, 1), 0)
    trow = lax.broadcasted_iota(jnp.int32, (TT, 1), 0)

    def level_mask(hsz, t_idx, s_idx):
        sh = hsz.bit_length()
        same = (t_idx >> sh) == (s_idx >> sh)
        return same & (((t_idx >> (sh - 1)) & 1) == 1) & (((s_idx >> (sh - 1)) & 1) == 0)

    lmask = [level_mask(hsz, row, col) for hsz in levels]
    src_t = lax.broadcasted_iota(jnp.int32, (L2, L), 0) & (L - 1)
    tok_t = lax.broadcasted_iota(jnp.int32, (L2, L), 1)
    eye_t = src_t == tok_t
    same_head_t = ((lax.broadcasted_iota(jnp.int32, (LANES, L2), 0) < RWKV_HEAD)
                   == (lax.broadcasted_iota(jnp.int32, (LANES, L2), 1) < L))
    lmask_t = [level_mask(hsz, tok_t, src_t) for hsz in levels]
    second_half = [((crow >> (hsz.bit_length() - 1)) & 1) == 1 for hsz in levels]

    def rows(c):
        return slice(c * L, (c + 1) * L)

    def pick(v, c, j):
        return v[rows(c), j * LANES:(j + 1) * LANES]

    def project(c0, c1):
        pbuf[:, c0:c1] = _mm(hb, win[:, c0:c1])

    def project_rwkv():
        p = _mm(hb, win[:, C_SH:])
        pbuf[:, C_GD:N_COLS] = p[:, 0:LANES]
        pbuf[:, C_SH:C_GD] = p[:, GLA_RANK:GLA_RANK + C_GD - C_SH]

    def stack(v):
        return _pair_rows(v, head0)

    def stack_t(v):
        return _pair_rows(v, tcol0)

    def segsum(v):
        vs = jnp.concatenate([v[:, i * LANES:(i + 1) * LANES] for i in range(4)], axis=0)
        s = _mm(_bf(vs), seg_ref[0:LANES])
        return jnp.concatenate([s[i * TT:(i + 1) * TT] for i in range(4)], axis=1)

    def chunk_cumsum(mat_ref, parts, c):
        return _mm(mat_ref[...], jnp.concatenate([p[rows(c)] for p in parts], axis=0))

    def last_row_bcast(v):
        return jnp.concatenate(
            [jnp.broadcast_to(v[(c + 1) * L - 1:(c + 1) * L], (L, v.shape[1])) for c in chunks], 0)

    def sigmoid_of_double(half_v):
        return 0.5 * jnp.tanh(half_v) + 0.5

    def silu(v):
        hv = 0.5 * v
        return hv * (jnp.tanh(hv) + 1.0)

    def rwkv_stream():
        project_rwkv()
        yield "proj"
        cur = pbuf[:, C_SH:C_SH + SHIFT_W]
        first = jnp.concatenate(
            [jnp.broadcast_to(ss_out[s], (seq_rows, SHIFT_W)) for s in seqs], axis=0)
        prev = jnp.where((trow & (seq_rows - 1)) == 0, first, pltpu.roll(cur, 1, 0))
        for s in seqs:
            ss_out[s] = cur[(s + 1) * seq_rows - 1:(s + 1) * seq_rows]
        yield None
        xm = cur + mu_ref[...] * (prev - cur)
        yield None
        r, kr, vr = xm[:, 0:W], xm[:, W:2 * W], xm[:, 2 * W:3 * W]
        wa = xm[:, 3 * W:3 * W + LANES]
        wa = jnp.where(head0, jnp.tanh(wa), wa)
        wa_pre = _mm(_bf(wa), waup[...])
        yield None
        log2_w = HALF_LOG2_DECAY * jnp.tanh(w0[...] + wa_pre[:, 0:W]) + HALF_LOG2_DECAY
        yield None
        a = sigmoid_of_double(a0[...] + wa_pre[:, W:2 * W])
        yield None
        kk = kr * kk_w[...]
        kr = kr * (1.0 + (a - 1.0) * ka_w[...])
        yield None
        kk = kk * jnp.minimum(lax.rsqrt(segsum(kk * kk)), 1.0 / KK_NORM_FLOOR)
        yield None
        parts = _split2(log2_w)
        cum = jnp.concatenate([chunk_cumsum(tri_ref, parts, c) for c in chunks], axis=0)
        yield None
        g_inv = jnp.exp2(-cum)
        g_last = jnp.concatenate(
            [jnp.broadcast_to(jnp.exp2(cum[(c + 1) * L - 1:(c + 1) * L]), (L, W)) for c in chunks], 0)
        beta = a * kk
        yield None
        alpha_t = -kk * jnp.exp2(cum - log2_w)
        yield None
        r_t = r * jnp.exp2(cum)
        yield None
        beta_h, k_h = beta * g_inv, kr * g_inv
        yield None
        beta_r, k_r = beta_h * g_last, k_h * g_last
        yield "elem"

        items = [(c, j) for c in chunks for j in range(N_RWKV_PAIRS)]

        def picked(v):
            return [_bf(pick(v, *cj)) for cj in items]

        def stacked(v):
            return _each(lambda cj: stack(_bf(pick(v, *cj))), items)

        def stacked_t(v):
            def one(cj):
                v2 = pick(v, *cj)
                vt = _bf(jnp.transpose(jnp.concatenate([v2, v2], axis=0)))
                return jnp.where(same_head_t, vt, jnp.zeros((), BF16))

            return _each(one, items)

        rt = picked(r_t)
        al_rt = [jnp.concatenate([p, q], axis=0) for p, q in zip(picked(alpha_t), rt)]
        bh_t = yield from stacked_t(beta_h)
        ab = yield from _each(_mm, al_rt, bh_t)
        a_ab = [v[0:L] for v in ab]
        a_rb = [_bf(jnp.where(lower, v[L:L2], 0.0)) for v in ab]
        eyef = jnp.where(eye, 1.0, 0.0)
        d = yield from _each(lambda v: eyef + jnp.where(lmask[-1], v, 0.0), a_ab)
        kh_t = yield from stacked_t(k_h)
        ak = yield from _each(_mm, al_rt, kh_t)
        a_ak = [_bf(jnp.where(strict_lower, v[0:L], 0.0)) for v in ak]
        a_rk = [_bf(jnp.where(lower, v[L:L2], 0.0)) for v in ak]
        v_s = yield from stacked(vr)
        kr_t = yield from stacked_t(k_r)
        by_v = yield from _each(
            lambda p, q, t, v: _mm(jnp.concatenate([p, q, t], axis=0), v), a_ak, a_rk, kr_t, v_s)
        akv = [v[0:L] for v in by_v]
        y_kv = [v[L:L2] for v in by_v]
        s_kv = [v[L2:L2 + LANES] for v in by_v]
        for li in range(len(levels) - 2, -1, -1):
            msk = lmask[li]
            db = yield from _each(_bf, d)
            dm = yield from _each(
                lambda p, v: _bf(_mm(p, stack_t(_bf(jnp.where(msk, v, 0.0))))), db, a_ab)
            d = yield from _each(lambda v, p, q: v + _mm(p, stack_t(q)), d, dm, db)
        al_s = yield from stacked(alpha_t)
        wu = yield from _each(
            lambda t, p, q: _mm(_bf(t), jnp.concatenate([p, stack(_bf(q))], axis=1)), d, al_s, akv)
        w_t = [_bf(v[:, 0:LANES]) for v in wu]
        u_p = [v[:, LANES:2 * LANES] for v in wu]
        br_t = yield from stacked_t(beta_r)
        by_u = [jnp.concatenate([p, q], axis=0) for p, q in zip(br_t, a_rb)]
        g_col = yield from _each(
            lambda cj: jnp.transpose(jnp.broadcast_to(pick(g_last, *cj)[0:1], (LANES, LANES))), items)
        yield "mats"

        def rwkv_step(idx, states):
            stb = [_bf(s) for s in states]
            wr = [_mm(jnp.concatenate([w_t[i], rt[i]], axis=0), sb) for i, sb in zip(idx, stb)]
            yield None
            ub = [stack(_bf(v[0:L] + u_p[i])) for i, v in zip(idx, wr)]
            bu = [_mm(by_u[i], u) for i, u in zip(idx, ub)]
            new = [s * g_col[i] + v[0:LANES] + s_kv[i] for i, s, v in zip(idx, states, bu)]
            yield None
            ys = [v[L:L2] + t[LANES:LANES + L] + y_kv[i] for i, v, t in zip(idx, wr, bu)]
            yield None
            return ys, new

        pairs = [(s, j) for s in seqs for j in range(N_RWKV_PAIRS)]
        y_items = [None] * len(items)
        states = [rst[sj] for sj in pairs]
        for c in range(nc):
            idx = [(s * nc + c) * N_RWKV_PAIRS + j for s, j in pairs]
            outs, states = yield from rwkv_step(idx, states)
            for i, o in zip(idx, outs):
                y_items[i] = o
        for sj, st in zip(pairs, states):
            rst[sj] = st
        y = jnp.concatenate(
            [jnp.concatenate([y_items[c * N_RWKV_PAIRS + j] for j in range(N_RWKV_PAIRS)], axis=1)
             for c in chunks], axis=0)
        yield None
        mean = segsum(y) * (1.0 / RWKV_HEAD)
        yc = y - mean
        yield None
        var = segsum(yc * yc) * (1.0 / RWKV_HEAD)
        yield None
        y = yc * lax.rsqrt(var + RWKV_GN_EPS) * lnw[...] + lnb[...]
        yield None
        bonus = segsum(r * kr * rk_w[...]) * vr
        yield None
        zr = pbuf[:, C_ZR:C_ZR + W]
        mbuf[:, GLA_WIDTH:GLA_WIDTH + W] = _bf((y + bonus) * silu(zr))
        while True:
            yield "end"

    def gla_stream():
        each1 = functools.partial(_each, group=1)
        project(C_Q, C_V)
        yield None
        gl = _mm(_bf(pbuf[:, C_GD:C_GD + LANES]), aup[...]) + abias[...]
        log2_a = ((jnp.minimum(gl, 0.0) - jnp.log1p(jnp.exp(-jnp.abs(gl))))
                  * (LOG2E / GLA_TEMP))
        parts = _split2(log2_a)
        yield None
        project(C_V, C_ZG)
        yield None
        ex = []
        for c in chunks:
            ex.append(jnp.exp2(chunk_cumsum(e_ref, parts, c)))
            yield None
        project(C_ZG, C_SH)
        yield "decays"

        gitems = [(c, j) for c in chunks for j in range(N_GLA_PAIRS)]
        q = pbuf[:, C_Q:C_Q + GLA_KEY] * (GLA_DK ** -0.5)
        k = pbuf[:, C_K:C_K + GLA_KEY]
        qj = [pick(q, c, j) for c, j in gitems]
        kj = [pick(k, c, j) for c, j in gitems]

        def exl(blk, c, j):
            return ex[c][blk * L:(blk + 1) * L, j * LANES:(j + 1) * LANES]

        att = yield from each1(
            lambda p, t: jnp.where(eye_t, _nt(stack(_bf(t)), _bf(p)), 0.0), qj, kj)
        for li in range(len(levels)):
            sec, msk = second_half[li], lmask_t[li]

            def level_term(v, p, t, cj):
                xl = jnp.where(sec, p, t) * exl(2 + li, *cj)
                ql = _bf(jnp.where(sec, xl, 0.0))
                kl = stack(_bf(jnp.where(sec, 0.0, xl)))
                return v + jnp.where(msk, _nt(kl, ql), 0.0)

            att = yield from each1(level_term, att, qj, kj, gitems)
        yield "att"

        def value_rows(cj):
            c, j = cj
            v0 = C_V + 2 * j * GLA_DV
            return _bf(jnp.concatenate(
                [pbuf[rows(c), v0:v0 + GLA_DV], pbuf[rows(c), v0 + GLA_DV:v0 + 2 * GLA_DV]], 0))

        gv_s = yield from each1(value_rows, gitems)
        g_kv = yield from each1(
            lambda t, p, cj: _tn(stack(_bf(p * exl(1, *cj))), t), gv_s, kj, gitems)
        q_e = yield from each1(lambda p, cj: p * exl(0, *cj), qj, gitems)
        decay = yield from each1(
            lambda cj: jnp.transpose(jnp.broadcast_to(exl(0, *cj)[L - 1:L], (LANES, LANES))), gitems)

        g_start = [None] * len(gitems)
        for s in seqs:
            for j in range(N_GLA_PAIRS):
                st = sg_out[s, j]
                for c in range(nc):
                    i = (s * nc + c) * N_GLA_PAIRS + j
                    g_start[i] = st
                    st = st * decay[i] + g_kv[i]
                sg_out[s, j] = st
        yield None
        for i, (c, j) in enumerate(gitems):
            gsb = _bf(g_start[i])
            attb = _bf(att[i])
            for hh in range(2):
                mine_k = head0 if hh == 0 else jnp.logical_not(head0)
                o = (_tn(attb[hh * L:(hh + 1) * L], gv_s[i][hh * L:(hh + 1) * L])
                     + _mm(_bf(jnp.where(mine_k, q_e[i], 0.0)), gsb))
                o = o * lax.rsqrt(jnp.mean(o * o, -1, keepdims=True) + RMS_EPS) * gnorm[...]
                c0 = (2 * j + hh) * GLA_DV
                zg = pbuf[rows(c), C_ZG + c0:C_ZG + c0 + GLA_DV]
                mbuf[rows(c), c0:c0 + GLA_DV] = _bf(o * silu(zg))
            yield None
        while True:
            yield "end"

    rw, gl = rwkv_stream(), gla_stream()
    while next(rw) != "proj":
        pass
    _alternate(rw, gl, "elem", "decays", STAGE_STEPS[0])
    _alternate(rw, gl, "mats", "att", STAGE_STEPS[1])
    _alternate(rw, gl, "end", "end", STAGE_STEPS[2])

    o = _mm(mbuf[...], wout[...])
    o = o * lax.rsqrt(jnp.mean(o * o, -1, keepdims=True) + RMS_EPS) * npost[...]
    y = x + o
    for s in seqs:
        y_ref[s] = y[s * seq_rows:(s + 1) * seq_rows]

    @pl.when(pl.program_id(1) == last_tile)
    def _():
        r_i = lax.broadcasted_iota(jnp.int32, (LANES, RWKV_HEAD), 0)
        c_i = lax.broadcasted_iota(jnp.int32, (LANES, RWKV_HEAD), 1)
        fold = _bf(jnp.where((r_i & (RWKV_HEAD - 1)) == c_i, 1.0, 0.0))
        for s in seqs:
            for j in range(N_RWKV_PAIRS):
                hi, mid, lo = _split3(jnp.transpose(rst[s, j]))
                sr_out[s, j] = _mm(hi, fold) + _mm(mid, fold) + _mm(lo, fold)


def _layer_call(layer, state_layer, x3d, sg, sr, ss, acc, params, consts, *, seqs_per_tile,
                seq_rows, L):
    n_seq, n_tok, _ = x3d.shape
    tiles_per_group = n_tok // seq_rows
    tile_rows = seqs_per_tile * seq_rows

    x_spec = pl.BlockSpec((seqs_per_tile, seq_rows, D_MODEL), lambda g, t: (g, t, 0))

    def state_in_spec(arr):
        nd = arr.ndim
        return pl.BlockSpec((None, seqs_per_tile) + arr.shape[2:],
                            lambda g, t: (state_layer, g) + (0,) * (nd - 2))

    def state_out_spec(arr):
        nd = arr.ndim
        return pl.BlockSpec((None, seqs_per_tile) + arr.shape[2:],
                            lambda g, t: (layer, g) + (0,) * (nd - 2))

    def layer_spec(arr):
        return pl.BlockSpec((None,) + arr.shape[1:], lambda g, t: (layer,) + (0,) * (arr.ndim - 1))

    def const_spec(arr):
        return pl.BlockSpec(arr.shape, lambda g, t: (0,) * arr.ndim)

    states = (sg, sr, ss)
    in_specs = ([x_spec] + [state_in_spec(s) for s in states]
                + [layer_spec(p) for p in params] + [const_spec(c) for c in consts]
                + [pl.BlockSpec(memory_space=pl.ANY)] * len(acc))
    first_acc = len(in_specs) - len(acc)
    out_specs = [x_spec] + [state_out_spec(a) for a in acc]
    out_shape = ([jax.ShapeDtypeStruct(x3d.shape, F32)]
                 + [jax.ShapeDtypeStruct(a.shape, F32) for a in acc])
    return pl.pallas_call(
        functools.partial(_layer_kernel, L=L, last_tile=tiles_per_group - 1),
        grid=(n_seq // seqs_per_tile, tiles_per_group),
        in_specs=in_specs,
        out_specs=out_specs,
        out_shape=out_shape,
        input_output_aliases={first_acc + k: 1 + k for k in range(len(acc))},
        scratch_shapes=[pltpu.VMEM((tile_rows, N_COLS), F32), pltpu.VMEM((tile_rows, D_MODEL), BF16),
                        pltpu.VMEM((seqs_per_tile, N_RWKV_PAIRS, LANES, LANES), F32)],
        compiler_params=pltpu.CompilerParams(
            dimension_semantics=("arbitrary", "arbitrary"), vmem_limit_bytes=VMEM_LIMIT_BYTES),
    )(x3d, sg, sr, ss, *params, *consts, *acc)


def _gla_state_to_kernel(s):
    return s.reshape(s.shape[:-3] + (N_GLA_PAIRS, LANES, GLA_DV))


def _gla_state_from_kernel(s):
    return s.reshape(s.shape[:-3] + (GLA_HEADS, GLA_DK, GLA_DV))


def _row(p):
    return p.reshape(DEPTH, 1, -1)


def _constants(L):
    seg = np.kron(np.eye(2, dtype=np.float32), np.ones((RWKV_HEAD, RWKV_HEAD), np.float32))
    e_mat = _gla_exponent_matrix(L)
    tri = np.tril(np.ones((L, L), np.float32))
    return (jnp.asarray(np.tile(e_mat, (1, 2)), dtype=BF16),
            jnp.asarray(np.tile(tri, (1, 2)), dtype=BF16),
            jnp.asarray(np.concatenate([seg, seg], 0), dtype=BF16))


def _tiling(bp, seq, bs):
    prompt_seqs = 2 if bp % 2 == 0 else 1
    prompt_rows = PROMPT_STEP_ROWS // prompt_seqs if seq % PROMPT_STEP_ROWS == 0 else CHUNK
    sample_seqs = SAMPLE_SEQS_PER_STEP if bs % SAMPLE_SEQS_PER_STEP == 0 else 1
    return prompt_seqs, prompt_rows, sample_seqs


def kernel(x_prompt, x_sample, state_gla, state_rwkv, state_shift, norm_pre, norm_post, w_in,
           gla_a_up, gla_a_bias, gla_norm, rwkv_mu, rwkv_w_up, rwkv_w0, rwkv_a_up, rwkv_a0,
           rwkv_k_k, rwkv_k_a, rwkv_r_k, rwkv_ln_w, rwkv_ln_b, w_out):
    bp, seq, _ = x_prompt.shape
    bs, dec_seq, _ = x_sample.shape

    aup = jnp.pad(gla_a_up, ((0, 0), (0, LANES - GLA_RANK), (0, 0))).astype(BF16)
    zpad = jnp.zeros_like(rwkv_w_up)
    waup = (0.5 * jnp.concatenate([jnp.concatenate([rwkv_w_up, zpad], 1),
                                   jnp.concatenate([zpad, rwkv_a_up], 1)], 2)).astype(BF16)
    params = (_row(norm_pre), _row(norm_post), w_in.astype(BF16), aup, _row(gla_a_bias),
              _row(gla_norm), _row(rwkv_mu), waup, _row(0.5 * rwkv_w0), _row(0.5 * rwkv_a0), _row(rwkv_k_k),
              _row(rwkv_k_a), _row(rwkv_r_k), _row(rwkv_ln_w), _row(rwkv_ln_b), w_out.astype(BF16))

    prompt_seqs, prompt_rows, sample_seqs = _tiling(bp, seq, bs)
    consts_p = _constants(CHUNK)
    consts_s = _constants(dec_seq)

    yp, ys = x_prompt, x_sample
    zero_states = (jnp.zeros((1, bp, N_GLA_PAIRS, LANES, GLA_DV), F32),
                   jnp.zeros((1, bp) + RWKV_PAIR_STATE, F32),
                   jnp.zeros((1, bp, 1, SHIFT_W), F32))
    sample_states = (_gla_state_to_kernel(state_gla),
                     state_rwkv.reshape(state_rwkv.shape[:2] + RWKV_PAIR_STATE),
                     state_shift[:, :, None, :])
    acc_p = tuple(jnp.zeros((DEPTH,) + s.shape[1:], F32) for s in zero_states)
    acc_s = tuple(jnp.zeros(s.shape, F32) for s in sample_states)
    for l in range(DEPTH):
        yp, *acc_p = _layer_call(
            l, 0, yp, *zero_states, acc_p, params, consts_p, seqs_per_tile=prompt_seqs,
            seq_rows=prompt_rows, L=CHUNK)
        ys, *acc_s = _layer_call(
            l, l, ys, *sample_states, acc_s, params, consts_s, seqs_per_tile=sample_seqs,
            seq_rows=dec_seq, L=dec_seq)
    (gla_p, rwkv_p, shift_p), (gla_s, rwkv_s, shift_s) = acc_p, acc_s
    return (yp, ys,
            _gla_state_from_kernel(gla_p), _gla_state_from_kernel(gla_s),
            rwkv_p.reshape((DEPTH, bp, RWKV_HEADS, RWKV_HEAD, RWKV_HEAD)),
            rwkv_s.reshape((DEPTH, bs, RWKV_HEADS, RWKV_HEAD, RWKV_HEAD)),
            shift_p[:, :, 0], shift_s[:, :, 0])
```

```python
import functools
import math

import numpy as np
import jax
import jax.numpy as jnp
from jax import lax
from jax.experimental import pallas as pl
from jax.experimental.pallas import tpu as pltpu

F32 = jnp.float32
BF16 = jnp.bfloat16

D_MODEL = 1024
DEPTH = 4
CHUNK = 64
RMS_EPS = 1e-6
GLA_WIDTH = 512
GLA_HEADS = 4
GLA_DV = 128
GLA_DK = 64
GLA_KEY = 256
GLA_RANK = 16
GLA_TEMP = 16.0
RWKV_WIDTH = 512
RWKV_HEAD = 64
RWKV_HEADS = 8
RWKV_RANK = 64
SHIFT_W = 3 * RWKV_WIDTH + 2 * RWKV_RANK
RWKV_GN_EPS = 64e-5
RWKV_DECAY_SCALE = 0.606531
LOG2E = math.log2(math.e)

LANES = 128
N_GLA_PAIRS = GLA_HEADS // 2
N_RWKV_PAIRS = RWKV_HEADS // 2
RWKV_PAIR_STATE = (N_RWKV_PAIRS, 2 * RWKV_HEAD, RWKV_HEAD)

C_Q = 0
C_K = C_Q + GLA_KEY
C_V = C_K + GLA_KEY
C_ZG = C_V + GLA_WIDTH
C_SH = C_ZG + GLA_WIDTH
C_ZR = C_SH + SHIFT_W
C_GD = C_ZR + RWKV_WIDTH
N_COLS = C_GD + LANES

VMEM_LIMIT_BYTES = 56 * 1024 * 1024
PROMPT_STEP_ROWS = 512
SAMPLE_SEQS_PER_STEP = 8
ITEMS_PER_STEP = 4
STAGE_STEPS = ((2, 1), (2, 1), (1, 2))
KK_NORM_FLOOR = 1e-12
HALF_LOG2_DECAY = -0.5 * RWKV_DECAY_SCALE * LOG2E


def _nt(a, b):
    return lax.dot_general(a, b, (((1,), (1,)), ((), ())), preferred_element_type=F32)


def _tn(a, b):
    return lax.dot_general(a, b, (((0,), (0,)), ((), ())), preferred_element_type=F32)


def _mm(a, b):
    return jnp.dot(a, b, preferred_element_type=F32)


def _bf(v):
    return v.astype(BF16)


def _split3(v):
    hi = _bf(v)
    r1 = v - hi.astype(F32)
    mid = _bf(r1)
    return hi, mid, _bf(r1 - mid.astype(F32))


def _split2(v):
    hi = _bf(v)
    return hi, _bf(v - hi.astype(F32))


def _each(fn, *lists, group=ITEMS_PER_STEP):
    out = []
    for i, args in enumerate(zip(*lists)):
        out.append(fn(*args))
        if i % group == group - 1:
            yield None
    return out


def _alternate(ga, gb, stop_a, stop_b, ratio):
    done_a = done_b = False
    while not (done_a and done_b):
        for _ in range(ratio[0]):
            if not done_a:
                done_a = next(ga) == stop_a
        for _ in range(ratio[1]):
            if not done_b:
                done_b = next(gb) == stop_b


def _pair_rows(v, first=None):
    if first is None:
        first = lax.broadcasted_iota(jnp.int32, (1, LANES), 1) < RWKV_HEAD
    zero = jnp.zeros((), v.dtype)
    return jnp.concatenate([jnp.where(first, v, zero), jnp.where(first, zero, v)], axis=0)


def _levels(L):
    out, h = [], L // 2
    while h >= 1:
        out.append(h)
        h //= 2
    return out


def _gla_exponent_matrix(L):
    t = np.arange(L)[:, None]
    i = np.arange(L)[None, :]
    blocks = [i <= t, i > t]
    for h in _levels(L):
        mid = (t // (2 * h)) * (2 * h) + h
        second = t >= mid
        blocks.append(np.where(second, (i >= mid) & (i <= t), (i > t) & (i < mid)))
    return np.concatenate(blocks, 0).astype(np.float32)


def _layer_kernel(x_ref, sg_in, sr_in, ss_in, npre, npost, win, aup, abias,
                  gnorm, mu_ref, waup, w0, a0, kk_w, ka_w, rk_w, lnw, lnb, wout, e_ref, tri_ref, seg_ref,
                  acc_g, acc_r, acc_s, y_ref, sg_out, sr_out, ss_out, pbuf, mbuf, rst, *, L, last_tile):
    del acc_g, acc_r, acc_s
    nb, seq_rows, _ = x_ref.shape
    nc = seq_rows // L
    TT = nb * seq_rows
    L2 = 2 * L
    W = RWKV_WIDTH
    levels = _levels(L)
    seqs = range(nb)
    chunks = range(nb * nc)

    @pl.when(pl.program_id(1) == 0)
    def _():
        sg_out[...] = sg_in[...]
        ss_out[...] = ss_in[...]
        dup = _bf(jnp.where(
            lax.broadcasted_iota(jnp.int32, (RWKV_HEAD, LANES), 0)
            == (lax.broadcasted_iota(jnp.int32, (RWKV_HEAD, LANES), 1) & (RWKV_HEAD - 1)), 1.0, 0.0))
        same_head = ((lax.broadcasted_iota(jnp.int32, (LANES, LANES), 0) < RWKV_HEAD)
                     == (lax.broadcasted_iota(jnp.int32, (LANES, LANES), 1) < RWKV_HEAD))
        for s in seqs:
            for j in range(N_RWKV_PAIRS):
                hi, mid, lo = _split3(sr_in[s, j])
                both = _mm(hi, dup) + _mm(mid, dup) + _mm(lo, dup)
                rst[s, j] = jnp.transpose(jnp.where(same_head, both, 0.0))

    x = jnp.concatenate([x_ref[s] for s in seqs], axis=0)
    hb = _bf(x * lax.rsqrt(jnp.mean(x * x, -1, keepdims=True) + RMS_EPS) * npre[...])

    lane = lax.broadcasted_iota(jnp.int32, (1, LANES), 1)
    head0 = lane < RWKV_HEAD
    tlane = lax.broadcasted_iota(jnp.int32, (1, L2), 1)
    tcol0 = tlane < L
    row = lax.broadcasted_iota(jnp.int32, (L, L2), 0)
    col = lax.broadcasted_iota(jnp.int32, (L, L2), 1) & (L - 1)
    eye = row == col
    strict_lower = row > col
    lower = row >= col
    crow = lax.broadcasted_iota(jnp.int32, (L, 1), 0)
    trow = lax.broadcasted_iota(jnp.int32, (TT, 1), 0)

    def level_mask(hsz, t_idx, s_idx):
        sh = hsz.bit_length()
        same = (t_idx >> sh) == (s_idx >> sh)
        return same & (((t_idx >> (sh - 1)) & 1) == 1) & (((s_idx >> (sh - 1)) & 1) == 0)

    lmask = [level_mask(hsz, row, col) for hsz in levels]
    src_t = lax.broadcasted_iota(jnp.int32, (L2, L), 0) & (L - 1)
    tok_t = lax.broadcasted_iota(jnp.int32, (L2, L), 1)
    eye_t = src_t == tok_t
    same_head_t = ((lax.broadcasted_iota(jnp.int32, (LANES, L2), 0) < RWKV_HEAD)
                   == (lax.broadcasted_iota(jnp.int32, (LANES, L2), 1) < L))
    lmask_t = [level_mask(hsz, tok_t, src_t) for hsz in levels]
    second_half = [((crow >> (hsz.bit_length() - 1)) & 1) == 1 for hsz in levels]

    def rows(c):
        return slice(c * L, (c + 1) * L)

    def pick(v, c, j):
        return v[rows(c), j * LANES:(j + 1) * LANES]

    def project(c0, c1):
        pbuf[:, c0:c1] = _mm(hb, win[:, c0:c1])

    def project_rwkv():
        p = _mm(hb, win[:, C_SH:])
        pbuf[:, C_GD:N_COLS] = p[:, 0:LANES]
        pbuf[:, C_SH:C_GD] = p[:, GLA_RANK:GLA_RANK + C_GD - C_SH]

    def stack(v):
        return _pair_rows(v, head0)

    def stack_t(v):
        return _pair_rows(v, tcol0)

    def segsum(v):
        vs = jnp.concatenate([v[:, i * LANES:(i + 1) * LANES] for i in range(4)], axis=0)
        s = _mm(_bf(vs), seg_ref[0:LANES])
        return jnp.concatenate([s[i * TT:(i + 1) * TT] for i in range(4)], axis=1)

    def chunk_cumsum(mat_ref, parts, c):
        return _mm(mat_ref[...], jnp.concatenate([p[rows(c)] for p in parts], axis=0))

    def last_row_bcast(v):
        return jnp.concatenate(
            [jnp.broadcast_to(v[(c + 1) * L - 1:(c + 1) * L], (L, v.shape[1])) for c in chunks], 0)

    def sigmoid_of_double(half_v):
        return 0.5 * jnp.tanh(half_v) + 0.5

    def silu(v):
        hv = 0.5 * v
        return hv * (jnp.tanh(hv) + 1.0)

    def rwkv_stream():
        project_rwkv()
        yield "proj"
        cur = pbuf[:, C_SH:C_SH + SHIFT_W]
        first = jnp.concatenate(
            [jnp.broadcast_to(ss_out[s], (seq_rows, SHIFT_W)) for s in seqs], axis=0)
        prev = jnp.where((trow & (seq_rows - 1)) == 0, first, pltpu.roll(cur, 1, 0))
        for s in seqs:
            ss_out[s] = cur[(s + 1) * seq_rows - 1:(s + 1) * seq_rows]
        yield None
        xm = cur + mu_ref[...] * (prev - cur)
        yield None
        r, kr, vr = xm[:, 0:W], xm[:, W:2 * W], xm[:, 2 * W:3 * W]
        wa = xm[:, 3 * W:3 * W + LANES]
        wa = jnp.where(head0, jnp.tanh(wa), wa)
        wa_pre = _mm(_bf(wa), waup[...])
        yield None
        log2_w = HALF_LOG2_DECAY * jnp.tanh(w0[...] + wa_pre[:, 0:W]) + HALF_LOG2_DECAY
        yield None
        a = sigmoid_of_double(a0[...] + wa_pre[:, W:2 * W])
        yield None
        kk = kr * kk_w[...]
        kr = kr * (1.0 + (a - 1.0) * ka_w[...])
        yield None
        kk = kk * jnp.minimum(lax.rsqrt(segsum(kk * kk)), 1.0 / KK_NORM_FLOOR)
        yield None
        parts = _split2(log2_w)
        cum = jnp.concatenate([chunk_cumsum(tri_ref, parts, c) for c in chunks], axis=0)
        cum_last = last_row_bcast(cum)
        yield None
        g_inv = jnp.exp2(-cum)
        g_last = jnp.exp2(cum_last)
        beta = a * kk
        yield None
        alpha_t = -kk * jnp.exp2(cum - log2_w)
        yield None
        r_t = r * jnp.exp2(cum)
        yield None
        beta_h, k_h = beta * g_inv, kr * g_inv
        yield None
        beta_r, k_r = beta_h * g_last, k_h * g_last
        yield "elem"

        items = [(c, j) for c in chunks for j in range(N_RWKV_PAIRS)]

        def picked(v):
            return [_bf(pick(v, *cj)) for cj in items]

        def stacked(v):
            return _each(lambda cj: stack(_bf(pick(v, *cj))), items)

        def stacked_t(v):
            def one(cj):
                v2 = pick(v, *cj)
                vt = _bf(jnp.transpose(jnp.concatenate([v2, v2], axis=0)))
                return jnp.where(same_head_t, vt, jnp.zeros((), BF16))

            return _each(one, items)

        rt = picked(r_t)
        al_rt = [jnp.concatenate([p, q], axis=0) for p, q in zip(picked(alpha_t), rt)]
        bh_t = yield from stacked_t(beta_h)
        ab = yield from _each(_mm, al_rt, bh_t)
        a_ab = [v[0:L] for v in ab]
        a_rb = [_bf(jnp.where(lower, v[L:L2], 0.0)) for v in ab]
        eyef = jnp.where(eye, 1.0, 0.0)
        d = yield from _each(lambda v: eyef + jnp.where(lmask[-1], v, 0.0), a_ab)
        kh_t = yield from stacked_t(k_h)
        ak = yield from _each(_mm, al_rt, kh_t)
        a_ak = [_bf(jnp.where(strict_lower, v[0:L], 0.0)) for v in ak]
        a_rk = [_bf(jnp.where(lower, v[L:L2], 0.0)) for v in ak]
        v_s = yield from stacked(vr)
        kr_t = yield from stacked_t(k_r)
        by_v = yield from _each(
            lambda p, q, t, v: _mm(jnp.concatenate([p, q, t], axis=0), v), a_ak, a_rk, kr_t, v_s)
        akv = [v[0:L] for v in by_v]
        y_kv = [v[L:L2] for v in by_v]
        s_kv = [v[L2:L2 + LANES] for v in by_v]
        for li in range(len(levels) - 2, -1, -1):
            msk = lmask[li]
            db = yield from _each(_bf, d)
            dm = yield from _each(
                lambda p, v: _bf(_mm(p, stack_t(_bf(jnp.where(msk, v, 0.0))))), db, a_ab)
            d = yield from _each(lambda v, p, q: v + _mm(p, stack_t(q)), d, dm, db)
        al_s = yield from stacked(alpha_t)
        wu = yield from _each(
            lambda t, p, q: _mm(_bf(t), jnp.concatenate([p, stack(_bf(q))], axis=1)), d, al_s, akv)
        w_t = [_bf(v[:, 0:LANES]) for v in wu]
        u_p = [v[:, LANES:2 * LANES] for v in wu]
        br_t = yield from stacked_t(beta_r)
        by_u = [jnp.concatenate([p, q], axis=0) for p, q in zip(br_t, a_rb)]
        g_col = yield from _each(
            lambda cj: jnp.transpose(jnp.broadcast_to(pick(g_last, *cj)[0:1], (LANES, LANES))), items)
        yield "mats"

        def rwkv_step(idx, states):
            stb = [_bf(s) for s in states]
            wr = [_mm(jnp.concatenate([w_t[i], rt[i]], axis=0), sb) for i, sb in zip(idx, stb)]
            yield None
            ub = [stack(_bf(v[0:L] + u_p[i])) for i, v in zip(idx, wr)]
            bu = [_mm(by_u[i], u) for i, u in zip(idx, ub)]
            new = [s * g_col[i] + v[0:LANES] + s_kv[i] for i, s, v in zip(idx, states, bu)]
            yield None
            ys = [v[L:L2] + t[LANES:LANES + L] + y_kv[i] for i, v, t in zip(idx, wr, bu)]
            yield None
            return ys, new

        pairs = [(s, j) for s in seqs for j in range(N_RWKV_PAIRS)]
        y_items = [None] * len(items)
        states = [rst[sj] for sj in pairs]
        for c in range(nc):
            idx = [(s * nc + c) * N_RWKV_PAIRS + j for s, j in pairs]
            outs, states = yield from rwkv_step(idx, states)
            for i, o in zip(idx, outs):
                y_items[i] = o
        for sj, st in zip(pairs, states):
            rst[sj] = st
        y = jnp.concatenate(
            [jnp.concatenate([y_items[c * N_RWKV_PAIRS + j] for j in range(N_RWKV_PAIRS)], axis=1)
             for c in chunks], axis=0)
        yield None
        mean = segsum(y) * (1.0 / RWKV_HEAD)
        yc = y - mean
        yield None
        var = segsum(yc * yc) * (1.0 / RWKV_HEAD)
        yield None
        y = yc * lax.rsqrt(var + RWKV_GN_EPS) * lnw[...] + lnb[...]
        yield None
        bonus = segsum(r * kr * rk_w[...]) * vr
        yield None
        zr = pbuf[:, C_ZR:C_ZR + W]
        mbuf[:, GLA_WIDTH:GLA_WIDTH + W] = _bf((y + bonus) * silu(zr))
        while True:
            yield "end"

    def gla_stream():
        each1 = functools.partial(_each, group=1)
        project(C_Q, C_V)
        yield None
        gl = _mm(_bf(pbuf[:, C_GD:C_GD + LANES]), aup[...]) + abias[...]
        log2_a = ((jnp.minimum(gl, 0.0) - jnp.log1p(jnp.exp(-jnp.abs(gl))))
                  * (LOG2E / GLA_TEMP))
        parts = _split2(log2_a)
        yield None
        project(C_V, C_ZG)
        yield None
        ex = []
        for c in chunks:
            ex.append(jnp.exp2(chunk_cumsum(e_ref, parts, c)))
            yield None
        project(C_ZG, C_SH)
        yield "decays"

        gitems = [(c, j) for c in chunks for j in range(N_GLA_PAIRS)]
        q = pbuf[:, C_Q:C_Q + GLA_KEY] * (GLA_DK ** -0.5)
        k = pbuf[:, C_K:C_K + GLA_KEY]
        qj = [pick(q, c, j) for c, j in gitems]
        kj = [pick(k, c, j) for c, j in gitems]

        def exl(blk, c, j):
            return ex[c][blk * L:(blk + 1) * L, j * LANES:(j + 1) * LANES]

        att = yield from each1(
            lambda p, t: jnp.where(eye_t, _nt(stack(_bf(t)), _bf(p)), 0.0), qj, kj)
        for li in range(len(levels)):
            sec, msk = second_half[li], lmask_t[li]

            def level_term(v, p, t, cj):
                xl = jnp.where(sec, p, t) * exl(2 + li, *cj)
                ql = _bf(jnp.where(sec, xl, 0.0))
                kl = stack(_bf(jnp.where(sec, 0.0, xl)))
                return v + jnp.where(msk, _nt(kl, ql), 0.0)

            att = yield from each1(level_term, att, qj, kj, gitems)
        yield "att"

        def value_rows(cj):
            c, j = cj
            v0 = C_V + 2 * j * GLA_DV
            return _bf(jnp.concatenate(
                [pbuf[rows(c), v0:v0 + GLA_DV], pbuf[rows(c), v0 + GLA_DV:v0 + 2 * GLA_DV]], 0))

        gv_s = yield from each1(value_rows, gitems)
        g_kv = yield from each1(
            lambda t, p, cj: _tn(stack(_bf(p * exl(1, *cj))), t), gv_s, kj, gitems)
        q_e = yield from each1(lambda p, cj: p * exl(0, *cj), qj, gitems)
        decay = yield from each1(
            lambda cj: jnp.transpose(jnp.broadcast_to(exl(0, *cj)[L - 1:L], (LANES, LANES))), gitems)

        g_start = [None] * len(gitems)
        for s in seqs:
            for j in range(N_GLA_PAIRS):
                st = sg_out[s, j]
                for c in range(nc):
                    i = (s * nc + c) * N_GLA_PAIRS + j
                    g_start[i] = st
                    st = st * decay[i] + g_kv[i]
                sg_out[s, j] = st
        yield None
        for i, (c, j) in enumerate(gitems):
            gsb = _bf(g_start[i])
            attb = _bf(att[i])
            for hh in range(2):
                mine_k = head0 if hh == 0 else jnp.logical_not(head0)
                o = (_tn(attb[hh * L:(hh + 1) * L], gv_s[i][hh * L:(hh + 1) * L])
                     + _mm(_bf(jnp.where(mine_k, q_e[i], 0.0)), gsb))
                o = o * lax.rsqrt(jnp.mean(o * o, -1, keepdims=True) + RMS_EPS) * gnorm[...]
                c0 = (2 * j + hh) * GLA_DV
                zg = pbuf[rows(c), C_ZG + c0:C_ZG + c0 + GLA_DV]
                mbuf[rows(c), c0:c0 + GLA_DV] = _bf(o * silu(zg))
            yield None
        while True:
            yield "end"

    rw, gl = rwkv_stream(), gla_stream()
    while next(rw) != "proj":
        pass
    _alternate(rw, gl, "elem", "decays", STAGE_STEPS[0])
    _alternate(rw, gl, "mats", "att", STAGE_STEPS[1])
    _alternate(rw, gl, "end", "end", STAGE_STEPS[2])

    o = _mm(mbuf[...], wout[...])
    o = o * lax.rsqrt(jnp.mean(o * o, -1, keepdims=True) + RMS_EPS) * npost[...]
    y = x + o
    for s in seqs:
        y_ref[s] = y[s * seq_rows:(s + 1) * seq_rows]

    @pl.when(pl.program_id(1) == last_tile)
    def _():
        r_i = lax.broadcasted_iota(jnp.int32, (LANES, RWKV_HEAD), 0)
        c_i = lax.broadcasted_iota(jnp.int32, (LANES, RWKV_HEAD), 1)
        fold = _bf(jnp.where((r_i & (RWKV_HEAD - 1)) == c_i, 1.0, 0.0))
        for s in seqs:
            for j in range(N_RWKV_PAIRS):
                hi, mid, lo = _split3(jnp.transpose(rst[s, j]))
                sr_out[s, j] = _mm(hi, fold) + _mm(mid, fold) + _mm(lo, fold)


def _layer_call(layer, state_layer, x3d, sg, sr, ss, acc, params, consts, *, seqs_per_tile,
                seq_rows, L):
    n_seq, n_tok, _ = x3d.shape
    tiles_per_group = n_tok // seq_rows
    tile_rows = seqs_per_tile * seq_rows

    x_spec = pl.BlockSpec((seqs_per_tile, seq_rows, D_MODEL), lambda g, t: (g, t, 0))

    def state_in_spec(arr):
        nd = arr.ndim
        return pl.BlockSpec((None, seqs_per_tile) + arr.shape[2:],
                            lambda g, t: (state_layer, g) + (0,) * (nd - 2))

    def state_out_spec(arr):
        nd = arr.ndim
        return pl.BlockSpec((None, seqs_per_tile) + arr.shape[2:],
                            lambda g, t: (layer, g) + (0,) * (nd - 2))

    def layer_spec(arr):
        return pl.BlockSpec((None,) + arr.shape[1:], lambda g, t: (layer,) + (0,) * (arr.ndim - 1),
                            pipeline_mode=pl.Buffered(1))

    def const_spec(arr):
        return pl.BlockSpec(arr.shape, lambda g, t: (0,) * arr.ndim, pipeline_mode=pl.Buffered(1))

    states = (sg, sr, ss)
    in_specs = ([x_spec] + [state_in_spec(s) for s in states]
                + [layer_spec(p) for p in params] + [const_spec(c) for c in consts]
                + [pl.BlockSpec(memory_space=pl.ANY)] * len(acc))
    first_acc = len(in_specs) - len(acc)
    out_specs = [x_spec] + [state_out_spec(a) for a in acc]
    out_shape = ([jax.ShapeDtypeStruct(x3d.shape, F32)]
                 + [jax.ShapeDtypeStruct(a.shape, F32) for a in acc])
    return pl.pallas_call(
        functools.partial(_layer_kernel, L=L, last_tile=tiles_per_group - 1),
        grid=(n_seq // seqs_per_tile, tiles_per_group),
        in_specs=in_specs,
        out_specs=out_specs,
        out_shape=out_shape,
        input_output_aliases={first_acc + k: 1 + k for k in range(len(acc))},
        scratch_shapes=[pltpu.VMEM((tile_rows, N_COLS), F32), pltpu.VMEM((tile_rows, D_MODEL), BF16),
                        pltpu.VMEM((seqs_per_tile, N_RWKV_PAIRS, LANES, LANES), F32)],
        compiler_params=pltpu.CompilerParams(
            dimension_semantics=("arbitrary", "arbitrary"), vmem_limit_bytes=VMEM_LIMIT_BYTES),
    )(x3d, sg, sr, ss, *params, *consts, *acc)


def _gla_state_to_kernel(s):
    return s.reshape(s.shape[:-3] + (N_GLA_PAIRS, LANES, GLA_DV))


def _gla_state_from_kernel(s):
    return s.reshape(s.shape[:-3] + (GLA_HEADS, GLA_DK, GLA_DV))


def _row(p):
    return p.reshape(DEPTH, 1, -1)


def _constants(L):
    seg = np.kron(np.eye(2, dtype=np.float32), np.ones((RWKV_HEAD, RWKV_HEAD), np.float32))
    e_mat = _gla_exponent_matrix(L)
    tri = np.tril(np.ones((L, L), np.float32))
    return (jnp.asarray(np.tile(e_mat, (1, 2)), dtype=BF16),
            jnp.asarray(np.tile(tri, (1, 2)), dtype=BF16),
            jnp.asarray(np.concatenate([seg, seg], 0), dtype=BF16))


def _tiling(bp, seq, bs):
    prompt_seqs = 2 if bp % 2 == 0 else 1
    prompt_rows = PROMPT_STEP_ROWS // prompt_seqs if seq % PROMPT_STEP_ROWS == 0 else CHUNK
    sample_seqs = SAMPLE_SEQS_PER_STEP if bs % SAMPLE_SEQS_PER_STEP == 0 else 1
    return prompt_seqs, prompt_rows, sample_seqs


def kernel(x_prompt, x_sample, state_gla, state_rwkv, state_shift, norm_pre, norm_post, w_in,
           gla_a_up, gla_a_bias, gla_norm, rwkv_mu, rwkv_w_up, rwkv_w0, rwkv_a_up, rwkv_a0,
           rwkv_k_k, rwkv_k_a, rwkv_r_k, rwkv_ln_w, rwkv_ln_b, w_out):
    bp, seq, _ = x_prompt.shape
    bs, dec_seq, _ = x_sample.shape

    aup = jnp.pad(gla_a_up, ((0, 0), (0, LANES - GLA_RANK), (0, 0))).astype(BF16)
    zpad = jnp.zeros_like(rwkv_w_up)
    waup = (0.5 * jnp.concatenate([jnp.concatenate([rwkv_w_up, zpad], 1),
                                   jnp.concatenate([zpad, rwkv_a_up], 1)], 2)).astype(BF16)
    params = (_row(norm_pre), _row(norm_post), w_in.astype(BF16), aup, _row(gla_a_bias),
              _row(gla_norm), _row(rwkv_mu), waup, _row(0.5 * rwkv_w0), _row(0.5 * rwkv_a0), _row(rwkv_k_k),
              _row(rwkv_k_a), _row(rwkv_r_k), _row(rwkv_ln_w), _row(rwkv_ln_b), w_out.astype(BF16))

    prompt_seqs, prompt_rows, sample_seqs = _tiling(bp, seq, bs)
    consts_p = _constants(CHUNK)
    consts_s = _constants(dec_seq)

    yp, ys = x_prompt, x_sample
    zero_states = (jnp.zeros((1, bp, N_GLA_PAIRS, LANES, GLA_DV), F32),
                   jnp.zeros((1, bp) + RWKV_PAIR_STATE, F32),
                   jnp.zeros((1, bp, 1, SHIFT_W), F32))
    sample_states = (_gla_state_to_kernel(state_gla),
                     state_rwkv.reshape(state_rwkv.shape[:2] + RWKV_PAIR_STATE),
                     state_shift[:, :, None, :])
    acc_p = tuple(jnp.zeros((DEPTH,) + s.shape[1:], F32) for s in zero_states)
    acc_s = tuple(jnp.zeros(s.shape, F32) for s in sample_states)
    for l in range(DEPTH):
        yp, *acc_p = _layer_call(
            l, 0, yp, *zero_states, acc_p, params, consts_p, seqs_per_tile=prompt_seqs,
            seq_rows=prompt_rows, L=CHUNK)
        ys, *acc_s = _layer_call(
            l, l, ys, *sample_states, acc_s, params, consts_s, seqs_per_tile=sample_seqs,
            seq_rows=dec_seq, L=dec_seq)
    (gla_p, rwkv_p, shift_p), (gla_s, rwkv_s, shift_s) = acc_p, acc_s
    return (yp, ys,
            _gla_state_from_kernel(gla_p), _gla_state_from_kernel(gla_s),
            rwkv_p.reshape((DEPTH, bp, RWKV_HEADS, RWKV_HEAD, RWKV_HEAD)),
            rwkv_s.reshape((DEPTH, bs, RWKV_HEADS, RWKV_HEAD, RWKV_HEAD)),
            shift_p[:, :, 0], shift_s[:, :, 0])
```
